```python
import jax, jax.numpy as jnp
from jax import lax
import numpy as np


D_MODEL = 1024
BATCH = 8
SEQ = 8192
DEPTH = 2

HEAD_DIM = 64
EPS = 1e-6
A_HEADS = 8
A_PATTERNS = ((128, 1), (512, 4), (2048, 16))
B_Q_HEADS = 8
B_KV_HEADS = 2
B_GROUP = B_Q_HEADS // B_KV_HEADS
B_WINDOW = 128
ATTN_BLOCK = 128
ROPE_THETA = 500000.0
ROT_DIM = HEAD_DIM // 4
C_HEADS = 8
C_KEY_DIM = D_MODEL // C_HEADS
C_VAL_DIM = D_MODEL // C_HEADS
C_CHUNK = 64
D_FF = (7 * D_MODEL) // 2
N_EXPERTS = 8
TOP_K = 2
MOE_BLOCK = 128

N_EVEN = (DEPTH + 1) // 2
N_ODD = DEPTH // 2
A_WIDTH = A_HEADS * HEAD_DIM
B_Q_WIDTH = B_Q_HEADS * HEAD_DIM
B_KV_WIDTH = B_KV_HEADS * HEAD_DIM
IN_ATTN = 3 * A_WIDTH + B_Q_WIDTH + 2 * B_KV_WIDTH
MIX_ATTN = A_WIDTH + B_Q_WIDTH
C_K_WIDTH = C_HEADS * C_KEY_DIM
C_V_WIDTH = C_HEADS * C_VAL_DIM
IN_REC = 2 * C_K_WIDTH + 2 * C_V_WIDTH

kernel_name = "hybrid_dilated_swa_hgrn2_moe"


def rms_norm(x, g):
    xf = x.astype(jnp.float32)
    y = xf * lax.rsqrt(jnp.mean(xf * xf, axis=-1, keepdims=True) + EPS)
    return (y * g.astype(jnp.float32)).astype(x.dtype)


def rope(x, cos, sin):
    half = ROT_DIM // 2
    xf = x.astype(jnp.float32)
    x1 = xf[..., :half]
    x2 = xf[..., half:ROT_DIM]
    out = jnp.concatenate([x1 * cos - x2 * sin, x2 * cos + x1 * sin, xf[..., ROT_DIM:]], axis=-1)
    return out.astype(x.dtype)


def banded_attention(q, k, v, n_back, sink=None):
    bsz, grp, rep, L, dh = q.shape
    blk = ATTN_BLOCK
    pad = (-L) % blk
    lp = L + pad
    nb = lp // blk
    q = jnp.pad(q, ((0, 0), (0, 0), (0, 0), (0, pad), (0, 0)))
    k = jnp.pad(k, ((0, 0), (0, 0), (0, pad), (0, 0)))
    v = jnp.pad(v, ((0, 0), (0, 0), (0, pad), (0, 0)))
    qb = q.reshape(bsz, grp, rep, nb, blk, dh)
    kb = k.reshape(bsz, grp, nb, blk, dh)
    vb = v.reshape(bsz, grp, nb, blk, dh)
    kk = jnp.concatenate([jnp.concatenate([jnp.zeros_like(kb[:, :, :1]), kb[:, :, :-1]], axis=2), kb], axis=3)
    vv = jnp.concatenate([jnp.concatenate([jnp.zeros_like(vb[:, :, :1]), vb[:, :, :-1]], axis=2), vb], axis=3)
    s = jnp.einsum('bgrnqd,bgnkd->bgrnqk', qb, kk, preferred_element_type=jnp.float32) * (dh ** -0.5)
    qi = jnp.arange(blk)[:, None]
    kj = jnp.arange(2 * blk)[None, :]
    dist = blk + qi - kj
    band = (dist >= 0) & (dist <= n_back)
    valid = band[None] & ((jnp.arange(nb)[:, None, None] > 0) | (kj >= blk)[None])
    s = jnp.where(valid, s, -jnp.inf)
    m = jnp.max(s, axis=-1, keepdims=True)
    if sink is not None:
        sk = sink.astype(jnp.float32)[None, :, :, None, None, None]
        m = jnp.maximum(m, sk)
    p = jnp.exp(s - m)
    l = jnp.sum(p, axis=-1, keepdims=True)
    if sink is not None:
        l = l + jnp.exp(sk - m)
    o = jnp.einsum('bgrnqk,bgnkd->bgrnqd', p, vv.astype(jnp.float32)) / l
    lse = (m + jnp.log(l))[..., 0]
    o = o.reshape(bsz, grp, rep, lp, dh)[:, :, :, :L].astype(q.dtype)
    lse = lse.reshape(bsz, grp, rep, lp)[:, :, :, :L]
    return o, lse


def dilated_attention(q, k, v):
    bsz, nh, S, dh = q.shape
    outs, lses = [], []
    for window, d in A_PATTERNS:
        L = S // d
        def to_res(t):
            return t.reshape(bsz, nh, L, d, dh).transpose(0, 1, 3, 2, 4).reshape(bsz, nh * d, L, dh)
        o, lse = banded_attention(to_res(q)[:, :, None], to_res(k), to_res(v), window // d)
        outs.append(o[:, :, 0].reshape(bsz, nh, d, L, dh).transpose(0, 1, 3, 2, 4).reshape(bsz, nh, S, dh))
        lses.append(lse[:, :, 0].reshape(bsz, nh, d, L).transpose(0, 1, 3, 2).reshape(bsz, nh, S))
    w = jax.nn.softmax(jnp.stack(lses, axis=0), axis=0)
    out = jnp.sum(w[..., None] * jnp.stack(outs, axis=0).astype(jnp.float32), axis=0)
    return out.astype(q.dtype)


def attn_mixer(h, w_in, sinks, w_out, cos, sin):
    bsz, S, _ = h.shape
    proj = h @ w_in
    splits = [A_WIDTH, 2 * A_WIDTH, 3 * A_WIDTH, 3 * A_WIDTH + B_Q_WIDTH, 3 * A_WIDTH + B_Q_WIDTH + B_KV_WIDTH]
    qa, ka, va, qb, kb, vb = jnp.split(proj, splits, axis=-1)
    heads = lambda t, n: t.reshape(bsz, S, n, HEAD_DIM)
    qa = rope(heads(qa, A_HEADS), cos, sin).transpose(0, 2, 1, 3)
    ka = rope(heads(ka, A_HEADS), cos, sin).transpose(0, 2, 1, 3)
    va = heads(va, A_HEADS).transpose(0, 2, 1, 3)
    oa = dilated_attention(qa, ka, va)
    qb = rope(heads(qb, B_Q_HEADS), cos, sin).reshape(bsz, S, B_KV_HEADS, B_GROUP, HEAD_DIM).transpose(0, 2, 3, 1, 4)
    kb = rope(heads(kb, B_KV_HEADS), cos, sin).transpose(0, 2, 1, 3)
    vb = heads(vb, B_KV_HEADS).transpose(0, 2, 1, 3)
    ob, _ = banded_attention(qb, kb, vb, B_WINDOW - 1, sinks.reshape(B_KV_HEADS, B_GROUP))
    mix = jnp.concatenate([oa.transpose(0, 2, 1, 3).reshape(bsz, S, A_WIDTH),
                           ob.transpose(0, 3, 1, 2, 4).reshape(bsz, S, B_Q_WIDTH)], axis=-1)
    return mix @ w_out


def hgrn2_mixer(h, w_in, lower_bound, norm_g, w_out):
    bsz, S, _ = h.shape
    proj = h @ w_in
    q, f, i, g = jnp.split(proj, [C_K_WIDTH, 2 * C_K_WIDTH, 2 * C_K_WIDTH + C_V_WIDTH], axis=-1)
    q = jax.nn.silu(q.astype(jnp.float32))
    fg = lower_bound + (1.0 - lower_bound) * jax.nn.sigmoid(f.astype(jnp.float32))
    k = 1.0 - fg
    logf = jnp.log(fg)
    nc = S // C_CHUNK
    def chunked(t, dim):
        return t.reshape(bsz, nc, C_CHUNK, C_HEADS, dim).transpose(1, 0, 3, 2, 4)
    qc = chunked(q, C_KEY_DIM)
    kc = chunked(k, C_KEY_DIM)
    vc = chunked(i.astype(jnp.float32), C_VAL_DIM)
    bc = jnp.cumsum(chunked(logf, C_KEY_DIM), axis=3)
    causal = jnp.tril(jnp.ones((C_CHUNK, C_CHUNK), dtype=bool))

    def step(state, xs):
        qt, kt, vt, bt = xs
        diff = bt[:, :, :, None, :] - bt[:, :, None, :, :]
        decay = jnp.exp(jnp.where(causal[:, :, None], diff, -jnp.inf))
        att = jnp.einsum('bhtk,bhsk,bhtsk->bhts', qt, kt, decay)
        o = jnp.einsum('bhts,bhsv->bhtv', att, vt) + jnp.einsum('bhtk,bhkv->bhtv', qt * jnp.exp(bt), state)
        b_last = bt[:, :, -1:, :]
        state = jnp.exp(b_last[:, :, 0, :])[..., None] * state + \
            jnp.einsum('bhsk,bhsv->bhkv', kt * jnp.exp(b_last - bt), vt)
        return state, o

    init = jnp.zeros((bsz, C_HEADS, C_KEY_DIM, C_VAL_DIM), jnp.float32)
    _, o = lax.scan(step, init, (qc, kc, vc, bc))
    o = o.transpose(1, 0, 3, 2, 4).reshape(bsz, S, C_HEADS, C_VAL_DIM)
    o = o * lax.rsqrt(jnp.mean(o * o, axis=-1, keepdims=True) + EPS)
    o = o.reshape(bsz, S, C_V_WIDTH) * norm_g.astype(jnp.float32) * jax.nn.silu(g.astype(jnp.float32))
    return o.astype(h.dtype) @ w_out


def swiglu(h, w_gate, w_up, w_down):
    return (jax.nn.silu(h @ w_gate) * (h @ w_up)) @ w_down


def moe_swiglu(h, w_router, w_gate, w_up, w_down):
    bsz, S, D = h.shape
    N = bsz * S
    t = h.reshape(N, D)
    logits = (t @ w_router).astype(jnp.float32)
    top_v, top_e = lax.top_k(logits, TOP_K)
    gates = jax.nn.softmax(top_v, axis=-1)
    e_flat = top_e.reshape(-1).astype(jnp.int32)
    g_flat = gates.reshape(-1)
    tok_flat = jnp.arange(N * TOP_K, dtype=jnp.int32) // TOP_K
    order = jnp.argsort(e_flat)
    e_sorted = e_flat[order]
    tok_sorted = tok_flat[order]
    g_sorted = g_flat[order]
    sizes = jnp.bincount(e_flat, length=N_EXPERTS).astype(jnp.int32)
    padded = ((sizes + MOE_BLOCK - 1) // MOE_BLOCK) * MOE_BLOCK
    start = jnp.cumsum(sizes) - sizes
    pend = jnp.cumsum(padded)
    pstart = pend - padded
    dest = pstart[e_sorted] + (jnp.arange(N * TOP_K, dtype=jnp.int32) - start[e_sorted])
    P = N * TOP_K + N_EXPERTS * MOE_BLOCK
    nblk = P // MOE_BLOCK
    buf = jnp.zeros((P, D), t.dtype).at[dest].set(t[tok_sorted])
    blk_e = jnp.minimum(jnp.searchsorted(pend, jnp.arange(nblk, dtype=jnp.int32) * MOE_BLOCK, side='right'),
                        N_EXPERTS - 1).astype(jnp.int32)

    def expert_block(args):
        xb, e = args
        return (jax.nn.silu(xb @ w_gate[e]) * (xb @ w_up[e])) @ w_down[e]

    yb = lax.map(expert_block, (buf.reshape(nblk, MOE_BLOCK, D), blk_e))
    y = yb.reshape(P, D)[dest] * g_sorted[:, None].astype(t.dtype)
    out = jax.ops.segment_sum(y, tok_sorted, num_segments=N)
    return out.reshape(bsz, S, D)


def setup_inputs(seed: int = 0) -> dict:
    key = jax.random.key(seed)
    ks = jax.random.split(key, 24)
    nrm = lambda k, shape, fan_in: jax.random.normal(k, shape, jnp.float32) * (fan_in ** -0.5)
    x = jax.random.normal(ks[0], (BATCH, SEQ, D_MODEL), jnp.float32)
    offsets = jax.random.randint(ks[1], (BATCH, 1), 0, 1024, dtype=jnp.int32)
    positions = offsets + jnp.arange(SEQ, dtype=jnp.int32)[None, :]
    g_mix = 1.0 + 0.02 * jax.random.normal(ks[2], (DEPTH, D_MODEL), jnp.float32)
    g_ffn = 1.0 + 0.02 * jax.random.normal(ks[3], (DEPTH, D_MODEL), jnp.float32)
    g_final = 1.0 + 0.02 * jax.random.normal(ks[4], (D_MODEL,), jnp.float32)
    w_in_attn = nrm(ks[5], (N_EVEN, D_MODEL, IN_ATTN), D_MODEL)
    w_out_attn = nrm(ks[6], (N_EVEN, MIX_ATTN, D_MODEL), MIX_ATTN)
    attn_sinks = 0.5 * jax.random.normal(ks[7], (N_EVEN, B_Q_HEADS), jnp.float32)
    w_in_rec = nrm(ks[8], (N_ODD, D_MODEL, IN_REC), D_MODEL)
    rec_lower_bounds = 0.5 * jax.random.normal(ks[9], (DEPTH, C_K_WIDTH), jnp.float32)
    rec_norm_g = 1.0 + 0.02 * jax.random.normal(ks[10], (N_ODD, C_V_WIDTH), jnp.float32)
    w_out_rec = nrm(ks[11], (N_ODD, C_V_WIDTH, D_MODEL), C_V_WIDTH)
    w_gate_dense = nrm(ks[12], (N_EVEN, D_MODEL, D_FF), D_MODEL)
    w_up_dense = nrm(ks[13], (N_EVEN, D_MODEL, D_FF), D_MODEL)
    w_down_dense = nrm(ks[14], (N_EVEN, D_FF, D_MODEL), D_FF)
    w_router = nrm(ks[15], (N_ODD, D_MODEL, N_EXPERTS), D_MODEL)
    w_gate_moe = nrm(ks[16], (N_ODD, N_EXPERTS, D_MODEL, D_FF), D_MODEL)
    w_up_moe = nrm(ks[17], (N_ODD, N_EXPERTS, D_MODEL, D_FF), D_MODEL)
    w_down_moe = nrm(ks[18], (N_ODD, N_EXPERTS, D_FF, D_MODEL), D_FF)
    return {"x": x, "positions": positions, "g_mix": g_mix, "g_ffn": g_ffn, "g_final": g_final,
            "w_in_attn": w_in_attn, "w_out_attn": w_out_attn, "attn_sinks": attn_sinks,
            "w_in_rec": w_in_rec, "rec_lower_bounds": rec_lower_bounds, "rec_norm_g": rec_norm_g,
            "w_out_rec": w_out_rec, "w_gate_dense": w_gate_dense, "w_up_dense": w_up_dense,
            "w_down_dense": w_down_dense, "w_router": w_router, "w_gate_moe": w_gate_moe,
            "w_up_moe": w_up_moe, "w_down_moe": w_down_moe}


def reference(x, positions, g_mix, g_ffn, g_final, w_in_attn, w_out_attn, attn_sinks, w_in_rec,
              rec_lower_bounds, rec_norm_g, w_out_rec, w_gate_dense, w_up_dense, w_down_dense,
              w_router, w_gate_moe, w_up_moe, w_down_moe):
    inv_freq = jnp.power(ROPE_THETA, -jnp.arange(0, ROT_DIM, 2, dtype=jnp.float32) / ROT_DIM)
    ang = positions.astype(jnp.float32)[..., None] * inv_freq
    cos = jnp.cos(ang)[:, :, None, :]
    sin = jnp.sin(ang)[:, :, None, :]
    lb = jax.nn.softmax(rec_lower_bounds.astype(jnp.float32), axis=0)
    lb_table = jnp.cumsum(lb, axis=0) - lb[0]
    for layer in range(DEPTH):
        j = layer // 2
        h = rms_norm(x, g_mix[layer])
        if layer % 2 == 0:
            x = x + attn_mixer(h, w_in_attn[j], attn_sinks[j], w_out_attn[j], cos, sin)
            x = x + swiglu(rms_norm(x, g_ffn[layer]), w_gate_dense[j], w_up_dense[j], w_down_dense[j])
        else:
            x = x + hgrn2_mixer(h, w_in_rec[j], lb_table[layer], rec_norm_g[j], w_out_rec[j])
            x = x + moe_swiglu(rms_norm(x, g_ffn[layer]), w_router[j], w_gate_moe[j], w_up_moe[j], w_down_moe[j])
    return rms_norm(x, g_final)
```

```python
import functools

import jax
import jax.numpy as jnp
from jax import lax
from jax.experimental import pallas as pl
from jax.experimental.pallas import tpu as pltpu

F32 = jnp.float32
BF16 = jnp.bfloat16
I32 = jnp.int32

EPS = 1e-6
HEAD_DIM = 64
ROT_DIM = HEAD_DIM // 4
ROPE_THETA = 500000.0
LANES = 128
ATTN_BLOCK = 128
A_HEADS = 8
A_PATTERNS = ((128, 1), (512, 4), (2048, 16))
B_Q_HEADS = 8
B_KV_HEADS = 2
B_WINDOW = 128
C_HEADS = 8
N_EXPERTS = 8
REC_CHUNK = 128
REC_SUB = 16
VMEM_LIMIT = 56 * 1024 * 1024


def _params(*sem):
    return pltpu.CompilerParams(dimension_semantics=sem, vmem_limit_bytes=VMEM_LIMIT)


def _rms(x, g):
    return x * lax.rsqrt(jnp.mean(x * x, axis=-1, keepdims=True) + EPS) * g


def _silu(x):
    return x / (1.0 + jnp.exp(-x))


def _dot(a, b):
    return jnp.dot(a, b, preferred_element_type=F32)


def _dot_nt(a, b):
    return lax.dot_general(a, b, (((1,), (1,)), ((), ())), preferred_element_type=F32)


def _dot_tn(a, b):
    return lax.dot_general(a, b, (((0,), (0,)), ((), ())), preferred_element_type=F32)


def _rope_table_kernel(pos_ref, invf_ref, cos_ref, sin_ref):
    ang = pos_ref[...].astype(F32) * invf_ref[...]
    cos_ref[...] = jnp.cos(ang)
    sin_ref[...] = jnp.sin(ang)


def _rope_tables(positions):
    n = positions.size
    half = ROT_DIM // 2
    per_row = LANES // half
    inv_freq = jnp.power(ROPE_THETA, -jnp.arange(0, ROT_DIM, 2, dtype=F32) / ROT_DIM)
    pos_rep = jnp.repeat(positions.reshape(n // per_row, per_row), half, axis=1)
    invf_row = jnp.tile(inv_freq, per_row).reshape(1, LANES)
    rows = n // per_row
    tr = min(rows, 512)
    cos8, sin8 = pl.pallas_call(
        _rope_table_kernel,
        grid=(rows // tr,),
        in_specs=[pl.BlockSpec((tr, LANES), lambda i: (i, 0)),
                  pl.BlockSpec((1, LANES), lambda i: (0, 0))],
        out_specs=[pl.BlockSpec((tr, LANES), lambda i: (i, 0))] * 2,
        out_shape=[jax.ShapeDtypeStruct((rows, LANES), F32)] * 2,
        compiler_params=_params("parallel"),
        name="rope_tables",
    )(pos_rep, invf_row)
    cos8 = cos8.reshape(n, half)
    sin8 = sin8.reshape(n, half)
    rest = HEAD_DIM - ROT_DIM
    c64 = jnp.concatenate([cos8, cos8, jnp.ones((n, rest), F32)], axis=1)
    s64 = jnp.concatenate([-sin8, sin8, jnp.zeros((n, rest), F32)], axis=1)
    return jnp.tile(c64, (1, 2)), jnp.tile(s64, (1, 2))


def _attn_inproj_kernel(x_ref, g_ref, w_ref, c_ref, s_ref, o_ref, *, rope_blocks, q_blocks):
    h = _rms(x_ref[...], g_ref[...]).astype(BF16)
    acc = _dot(h, w_ref[...])
    c = c_ref[...]
    s = s_ref[...]
    lane = lax.broadcasted_iota(I32, c.shape, 1)
    first = (lane % HEAD_DIM) < (ROT_DIM // 2)
    for cb in range(acc.shape[1] // LANES):
        blk = acc[:, cb * LANES:(cb + 1) * LANES]
        if cb in rope_blocks:
            up = pltpu.roll(blk, LANES - ROT_DIM // 2, 1)
            dn = pltpu.roll(blk, ROT_DIM // 2, 1)
            blk = blk * c + jnp.where(first, up, dn) * s
        if cb in q_blocks:
            blk = blk * (HEAD_DIM ** -0.5)
        o_ref[:, cb * LANES:(cb + 1) * LANES] = blk.astype(BF16)


def _attn_inproj(x, g, w, cos_t, sin_t, tm=512):
    n, d = x.shape
    width = w.shape[1]
    a_w = A_HEADS * HEAD_DIM // LANES
    bq_w = B_Q_HEADS * HEAD_DIM // LANES
    bkv_w = B_KV_HEADS * HEAD_DIM // LANES
    qa = tuple(range(0, a_w))
    ka = tuple(range(a_w, 2 * a_w))
    qb = tuple(range(3 * a_w, 3 * a_w + bq_w))
    kb = tuple(range(3 * a_w + bq_w, 3 * a_w + bq_w + bkv_w))
    kern = functools.partial(_attn_inproj_kernel, rope_blocks=qa + ka + qb + kb, q_blocks=qa + qb)
    return pl.pallas_call(
        kern,
        grid=(n // tm,),
        in_specs=[pl.BlockSpec((tm, d), lambda i: (i, 0)),
                  pl.BlockSpec((1, d), lambda i: (0, 0)),
                  pl.BlockSpec((d, width), lambda i: (0, 0)),
                  pl.BlockSpec((tm, LANES), lambda i: (i, 0)),
                  pl.BlockSpec((tm, LANES), lambda i: (i, 0))],
        out_specs=pl.BlockSpec((tm, width), lambda i: (i, 0)),
        out_shape=jax.ShapeDtypeStruct((n, width), BF16),
        compiler_params=_params("parallel"),
        name="attn_inproj",
    )(x, g.reshape(1, d), w, cos_t, sin_t)


def _band_mask(n_back):
    qi = lax.broadcasted_iota(I32, (ATTN_BLOCK, 2 * ATTN_BLOCK), 0)
    kj = lax.broadcasted_iota(I32, (ATTN_BLOCK, 2 * ATTN_BLOCK), 1)
    dist = ATTN_BLOCK + qi - kj
    return (dist >= 0) & (dist <= n_back), kj >= ATTN_BLOCK


def _pair_attention(q, k_pos, v_pos, valid, sinks):
    lane = lax.broadcasted_iota(I32, (ATTN_BLOCK, LANES), 1)
    o = jnp.zeros((ATTN_BLOCK, LANES), F32)
    lses = []
    for p in range(2):
        s = _dot_nt(q, k_pos[p])
        s = jnp.where(valid, s, -jnp.inf)
        m = jnp.max(s, axis=-1, keepdims=True)
        if sinks is not None:
            m = jnp.maximum(m, sinks[p])
        e = jnp.exp(s - m)
        l = jnp.sum(e, axis=-1, keepdims=True)
        if sinks is not None:
            l = l + jnp.exp(sinks[p] - m)
        o = o + _dot(e.astype(BF16), v_pos[p]) / l
        lses.append(m + jnp.log(l))
    return o, jnp.where(lane < HEAD_DIM, lses[0], lses[1])


def _dilated_kernel(q_ref, kp_ref, kc_ref, vp_ref, vc_ref, *rest, n_back, nq, merge):
    if merge:
        o1_ref, l1_ref, o2_ref, l2_ref, o_ref = rest
    else:
        o_ref, lse_ref = rest
    first_tile = pl.program_id(3) == 0
    band, in_cur = _band_mask(n_back)
    kall = jnp.concatenate([kp_ref[0], kc_ref[0]], axis=0)
    vall = jnp.concatenate([vp_ref[0], vc_ref[0]], axis=0)
    lane = lax.broadcasted_iota(I32, kall.shape, 1)
    zero = jnp.zeros_like(kall)
    head = (lane < HEAD_DIM, lane >= HEAD_DIM)
    k_pos = [jnp.where(hm, kall, zero) for hm in head]
    v_pos = [jnp.where(hm, vall, zero) for hm in head]
    for jb in range(nq):
        lo, hi = jb * ATTN_BLOCK, (jb + 2) * ATTN_BLOCK
        rows = slice(jb * ATTN_BLOCK, (jb + 1) * ATTN_BLOCK)
        valid = band & (in_cur | jnp.logical_not(first_tile)) if jb == 0 else band
        o, lse = _pair_attention(q_ref[0, rows, :], [k[lo:hi] for k in k_pos],
                                 [v[lo:hi] for v in v_pos], valid, None)
        if merge:
            l1 = l1_ref[0, rows, :]
            l2 = l2_ref[0, rows, :]
            mx = jnp.maximum(jnp.maximum(lse, l1), l2)
            w0 = jnp.exp(lse - mx)
            w1 = jnp.exp(l1 - mx)
            w2 = jnp.exp(l2 - mx)
            num = w0 * o + w1 * o1_ref[0, rows, :].astype(F32) + w2 * o2_ref[0, rows, :].astype(F32)
            o_ref[0, rows, :] = (num / (w0 + w1 + w2)).astype(o_ref.dtype)
        else:
            o_ref[0, rows, :] = o.astype(o_ref.dtype)
            lse_ref[0, rows, :] = lse


def _dilated_branch(proj, bsz, seq, window, dil, others=None):
    width = proj.shape[-1]
    wb = width // LANES
    a_w = A_HEADS * HEAD_DIM
    ab = a_w // LANES
    length = seq // dil
    tq = min(length, 4 * ATTN_BLOCK)
    nq = tq // ATTN_BLOCK
    view = proj.reshape(bsz, length, dil * width)

    def q_map(b, r, hp, i):
        return (b, i, r * wb + hp)

    def kc_map(b, r, hp, i):
        return (b, i, r * wb + ab + hp)

    def kp_map(b, r, hp, i):
        return (b, jnp.maximum(i * nq - 1, 0), r * wb + ab + hp)

    def vc_map(b, r, hp, i):
        return (b, i, r * wb + 2 * ab + hp)

    def vp_map(b, r, hp, i):
        return (b, jnp.maximum(i * nq - 1, 0), r * wb + 2 * ab + hp)

    def o_map(b, r, hp, i):
        return (b, i, r * ab + hp)

    in_specs = [pl.BlockSpec((1, tq, LANES), q_map),
                pl.BlockSpec((1, ATTN_BLOCK, LANES), kp_map),
                pl.BlockSpec((1, tq, LANES), kc_map),
                pl.BlockSpec((1, ATTN_BLOCK, LANES), vp_map),
                pl.BlockSpec((1, tq, LANES), vc_map)]
    o_spec = pl.BlockSpec((1, tq, LANES), o_map)
    args = [view, view, view, view, view]
    merge = others is not None
    if merge:
        assert dil == 1
        in_specs += [o_spec] * 4
        args += list(others)
        out_specs = o_spec
        out_shape = jax.ShapeDtypeStruct((bsz, length, dil * a_w), BF16)
    else:
        out_specs = [o_spec, o_spec]
        out_shape = [jax.ShapeDtypeStruct((bsz, length, dil * a_w), BF16),
                     jax.ShapeDtypeStruct((bsz, length, dil * a_w), F32)]
    kern = functools.partial(_dilated_kernel, n_back=window // dil, nq=nq, merge=merge)
    out = pl.pallas_call(
        kern,
        grid=(bsz, dil, ab, length // tq),
        in_specs=in_specs,
        out_specs=out_specs,
        out_shape=out_shape,
        compiler_params=_params("parallel", "parallel", "parallel", "parallel"),
        name=f"dilated_attn_d{dil}",
    )(*args)
    if merge:
        return out.reshape(bsz, seq, a_w)
    return out[0].reshape(bsz, seq, a_w), out[1].reshape(bsz, seq, a_w)


def _swa_kernel(sink_ref, q_ref, kp_ref, kc_ref, vp_ref, vc_ref, o_ref, *, n_back, nq):
    first_tile = pl.program_id(1) == 0
    band, in_cur = _band_mask(n_back)
    kall = jnp.concatenate([kp_ref[0], kc_ref[0]], axis=0)
    vall = jnp.concatenate([vp_ref[0], vc_ref[0]], axis=0)
    lane = lax.broadcasted_iota(I32, kall.shape, 1)
    zero = jnp.zeros_like(kall)
    lo_half = lane < HEAD_DIM
    kswap = pltpu.roll(kall.astype(F32), HEAD_DIM, 1).astype(BF16)
    vswap = pltpu.roll(vall.astype(F32), HEAD_DIM, 1).astype(BF16)
    group = B_Q_HEADS // B_KV_HEADS
    for g in range(B_KV_HEADS):
        src_k = [kall if g == p else kswap for p in range(2)]
        src_v = [vall if g == p else vswap for p in range(2)]
        k_pos = [jnp.where(lo_half if p == 0 else ~lo_half, src_k[p], zero) for p in range(2)]
        v_pos = [jnp.where(lo_half if p == 0 else ~lo_half, src_v[p], zero) for p in range(2)]
        for pp in range(group // 2):
            cb = g * (group // 2) + pp
            sinks = [sink_ref[2 * cb], sink_ref[2 * cb + 1]]
            for jb in range(nq):
                lo, hi = jb * ATTN_BLOCK, (jb + 2) * ATTN_BLOCK
                rows = slice(jb * ATTN_BLOCK, (jb + 1) * ATTN_BLOCK)
                valid = band & (in_cur | jnp.logical_not(first_tile)) if jb == 0 else band
                o, _ = _pair_attention(q_ref[0, rows, cb * LANES:(cb + 1) * LANES],
                                       [k[lo:hi] for k in k_pos], [v[lo:hi] for v in v_pos],
                                       valid, sinks)
                o_ref[0, rows, cb * LANES:(cb + 1) * LANES] = o.astype(o_ref.dtype)


def _swa_gqa(proj, sinks, bsz, seq):
    width = proj.shape[-1]
    a_w = A_HEADS * HEAD_DIM
    bq_w = B_Q_HEADS * HEAD_DIM
    assert B_KV_HEADS * HEAD_DIM == LANES
    q_col = 3 * a_w // bq_w
    k_col = (3 * a_w + bq_w) // LANES
    tq = min(seq, 2 * ATTN_BLOCK)
    nq = tq // ATTN_BLOCK
    view = proj.reshape(bsz, seq, width)
    in_specs = [pl.BlockSpec(memory_space=pltpu.SMEM),
                pl.BlockSpec((1, tq, bq_w), lambda b, i: (b, i, q_col)),
                pl.BlockSpec((1, ATTN_BLOCK, LANES), lambda b, i: (b, jnp.maximum(i * nq - 1, 0), k_col)),
                pl.BlockSpec((1, tq, LANES), lambda b, i: (b, i, k_col)),
                pl.BlockSpec((1, ATTN_BLOCK, LANES), lambda b, i: (b, jnp.maximum(i * nq - 1, 0), k_col + 1)),
                pl.BlockSpec((1, tq, LANES), lambda b, i: (b, i, k_col + 1))]
    kern = functools.partial(_swa_kernel, n_back=B_WINDOW - 1, nq=nq)
    return pl.pallas_call(
        kern,
        grid=(bsz, seq // tq),
        in_specs=in_specs,
        out_specs=pl.BlockSpec((1, tq, bq_w), lambda b, i: (b, i, 0)),
        out_shape=jax.ShapeDtypeStruct((bsz, seq, bq_w), BF16),
        compiler_params=_params("parallel", "parallel"),
        name="swa_gqa",
    )(sinks.astype(F32), view, view, view, view, view)


def _outproj_ffn_kernel(x_ref, a_ref, b_ref, wa_ref, wb_ref, g_ref, wg_ref, wu_ref, wd_ref,
                        o_ref, h_ref):
    f = pl.program_id(1)

    @pl.when(f == 0)
    def _():
        x1 = x_ref[...] + _dot(a_ref[...], wa_ref[...]) + _dot(b_ref[...], wb_ref[...])
        o_ref[...] = x1
        h_ref[...] = _rms(x1, g_ref[...]).astype(BF16)

    h = h_ref[...]
    act = _silu(_dot(h, wg_ref[...])) * _dot(h, wu_ref[...])
    o_ref[...] += _dot(act.astype(BF16), wd_ref[...])


def _outproj_ffn(x, mix_a, mix_b, w_a, w_b, g, w_gate, w_up, w_down, tm=1024, tf=512):
    n, d = x.shape
    ff = w_gate.shape[1]
    ka, kb = mix_a.shape[1], mix_b.shape[1]
    return pl.pallas_call(
        _outproj_ffn_kernel,
        grid=(n // tm, ff // tf),
        in_specs=[pl.BlockSpec((tm, d), lambda i, f: (i, 0)),
                  pl.BlockSpec((tm, ka), lambda i, f: (i, 0)),
                  pl.BlockSpec((tm, kb), lambda i, f: (i, 0)),
                  pl.BlockSpec((ka, d), lambda i, f: (0, 0)),
                  pl.BlockSpec((kb, d), lambda i, f: (0, 0)),
                  pl.BlockSpec((1, d), lambda i, f: (0, 0)),
                  pl.BlockSpec((d, tf), lambda i, f: (0, f)),
                  pl.BlockSpec((d, tf), lambda i, f: (0, f)),
                  pl.BlockSpec((tf, d), lambda i, f: (f, 0))],
        out_specs=pl.BlockSpec((tm, d), lambda i, f: (i, 0)),
        out_shape=jax.ShapeDtypeStruct((n, d), F32),
        scratch_shapes=[pltpu.VMEM((tm, d), BF16)],
        compiler_params=_params("parallel", "arbitrary"),
        name="outproj_ffn",
    )(x, mix_a, mix_b, w_a, w_b, g.reshape(1, d), w_gate, w_up, w_down)


def _rec_inproj_kernel(x_ref, g_ref, w_ref, lb_ref, q_ref, k_ref, lf_ref, v_ref, gate_ref):
    h = _rms(x_ref[...], g_ref[...]).astype(BF16)
    acc = _dot(h, w_ref[...])
    kw = q_ref.shape[1]
    vw = v_ref.shape[1]
    lb = lb_ref[...]
    q_ref[...] = _silu(acc[:, :kw]).astype(BF16)
    fg = lb + (1.0 - lb) / (1.0 + jnp.exp(-acc[:, kw:2 * kw]))
    k_ref[...] = (1.0 - fg).astype(BF16)
    lf_ref[...] = jnp.log(fg)
    v_ref[...] = acc[:, 2 * kw:2 * kw + vw].astype(BF16)
    gate_ref[...] = _silu(acc[:, 2 * kw + vw:]).astype(BF16)


def _rec_inproj(x, g, w, lb, kw, vw, tm=512):
    n, d = x.shape
    width = w.shape[1]
    row = lambda i: (i, 0)
    fix = lambda i: (0, 0)
    return pl.pallas_call(
        _rec_inproj_kernel,
        grid=(n // tm,),
        in_specs=[pl.BlockSpec((tm, d), row),
                  pl.BlockSpec((1, d), fix),
                  pl.BlockSpec((d, width), fix),
                  pl.BlockSpec((1, kw), fix)],
        out_specs=[pl.BlockSpec((tm, kw), row), pl.BlockSpec((tm, kw), row),
                   pl.BlockSpec((tm, kw), row), pl.BlockSpec((tm, vw), row),
                   pl.BlockSpec((tm, vw), row)],
        out_shape=[jax.ShapeDtypeStruct((n, kw), BF16), jax.ShapeDtypeStruct((n, kw), BF16),
                   jax.ShapeDtypeStruct((n, kw), F32), jax.ShapeDtypeStruct((n, vw), BF16),
                   jax.ShapeDtypeStruct((n, vw), BF16)],
        compiler_params=_params("parallel"),
        name="rec_inproj",
    )(x, g.reshape(1, d), w, lb.reshape(1, kw))


def _hgrn_kernel(q_ref, k_ref, lf_ref, v_ref, gate_ref, ng_ref, o_ref, st_ref, *, n_chunks):
    @pl.when(pl.program_id(2) == 0)
    def _():
        st_ref[...] = jnp.zeros_like(st_ref)

    c_len, sub = REC_CHUNK, REC_SUB
    r_i = lax.broadcasted_iota(I32, (c_len, c_len), 0)
    c_i = lax.broadcasted_iota(I32, (c_len, c_len), 1)
    causal = c_i <= r_i
    tril = causal.astype(BF16)
    ng = ng_ref[...]
    st = st_ref[...]
    for c in range(n_chunks):
        rows = slice(c * c_len, (c + 1) * c_len)
        lf = lf_ref[0, rows, :]
        lf1 = lf.astype(BF16)
        rem = lf - lf1.astype(F32)
        lf2 = rem.astype(BF16)
        lf3 = (rem - lf2.astype(F32)).astype(BF16)
        b = _dot(tril, lf1) + _dot(tril, lf2) + _dot(tril, lf3)
        q = q_ref[0, rows, :].astype(F32)
        k = k_ref[0, rows, :].astype(F32)
        v = v_ref[0, rows, :]
        b_last = b[c_len - 1:c_len, :]
        att_rows = []
        qe_rows = []
        for i in range(c_len // sub):
            lo, hi = i * sub, (i + 1) * sub
            ref = b[lo - 1:lo, :] if i > 0 else jnp.zeros_like(b_last)
            qt = q[lo:hi] * jnp.exp(b[lo:hi] - ref)
            kt = (k[:hi] * jnp.exp(ref - b[:hi])).astype(BF16)
            if hi < c_len:
                kt = jnp.concatenate([kt, jnp.zeros((c_len - hi, kt.shape[1]), BF16)], axis=0)
            att_rows.append(_dot_nt(qt.astype(BF16), kt))
            qe_rows.append(qt * jnp.exp(ref))
        att = jnp.where(causal, jnp.concatenate(att_rows, axis=0), 0.0)
        qe = jnp.concatenate(qe_rows, axis=0)
        o = _dot(att.astype(BF16), v) + _dot_nt(qe.astype(BF16), st.astype(BF16))
        kd = (k * jnp.exp(b_last - b)).astype(BF16)
        st = st * jnp.exp(b_last) + _dot_tn(v, kd)
        y = o * lax.rsqrt(jnp.mean(o * o, axis=-1, keepdims=True) + EPS)
        o_ref[0, rows, :] = (y * ng * gate_ref[0, rows, :].astype(F32)).astype(o_ref.dtype)
    st_ref[...] = st


def _hgrn(q, k, lf, v, gate, norm_g, bsz, seq, ts=512):
    kw = q.shape[-1] // C_HEADS
    vw = v.shape[-1] // C_HEADS
    assert kw == LANES and vw == LANES
    ts = min(ts, seq)
    v3 = lambda t: t.reshape(bsz, seq, t.shape[-1])
    blk = pl.BlockSpec((1, ts, LANES), lambda b, h, c: (b, c, h))
    kern = functools.partial(_hgrn_kernel, n_chunks=ts // REC_CHUNK)
    out = pl.pallas_call(
        kern,
        grid=(bsz, C_HEADS, seq // ts),
        in_specs=[blk, blk, blk, blk, blk, pl.BlockSpec((1, LANES), lambda b, h, c: (0, h))],
        out_specs=blk,
        out_shape=jax.ShapeDtypeStruct((bsz, seq, C_HEADS * vw), BF16),
        scratch_shapes=[pltpu.VMEM((vw, kw), F32)],
        compiler_params=_params("parallel", "parallel", "arbitrary"),
        name="hgrn2",
    )(v3(q), v3(k), v3(lf), v3(v), v3(gate), norm_g.reshape(1, -1))
    return out.reshape(bsz * seq, C_HEADS * vw)


ROUTE_E, ROUTE_RANK, ROUTE_GATE = 0, 2, 4


def _router_kernel(x_ref, a_ref, wo_ref, g_ref, wr_hi_ref, wr_lo_ref, x3_ref, h_ref, route_ref,
                   cnt_ref, base_ref):
    @pl.when(pl.program_id(0) == 0)
    def _():
        base_ref[...] = jnp.zeros_like(base_ref)

    x3 = x_ref[...] + _dot(a_ref[...], wo_ref[...])
    x3_ref[...] = x3
    h = _rms(x3, g_ref[...])
    h_ref[...] = h
    h_hi = h.astype(BF16)
    h_lo = (h - h_hi.astype(F32)).astype(BF16)
    logits = _dot(h_hi, wr_hi_ref[...]) + _dot(h_hi, wr_lo_ref[...]) + _dot(h_lo, wr_hi_ref[...])
    tm = logits.shape[0]
    lane = lax.broadcasted_iota(I32, logits.shape, 1)
    logits = jnp.where(lane < N_EXPERTS, logits, -jnp.inf)
    lane_f = lane.astype(F32)
    v1 = jnp.max(logits, axis=-1, keepdims=True)
    e1 = jnp.min(jnp.where(logits == v1, lane_f, float(LANES)), axis=-1, keepdims=True)
    hot1 = lane_f == e1
    rest = jnp.where(hot1, -jnp.inf, logits)
    v2 = jnp.max(rest, axis=-1, keepdims=True)
    e2 = jnp.min(jnp.where(rest == v2, lane_f, float(LANES)), axis=-1, keepdims=True)
    hot2 = lane_f == e2
    t = jnp.exp(v2 - v1)
    g1 = 1.0 / (1.0 + t)
    g2 = t / (1.0 + t)
    e1 = e1.astype(I32)
    e2 = e2.astype(I32)
    member = (hot1 | hot2).astype(BF16)
    r_i = lax.broadcasted_iota(I32, (tm, tm), 0)
    c_i = lax.broadcasted_iota(I32, (tm, tm), 1)
    before = _dot((c_i < r_i).astype(BF16), member) + base_ref[...]
    rank1 = jnp.sum(jnp.where(hot1, before, 0.0), axis=-1, keepdims=True).astype(I32)
    rank2 = jnp.sum(jnp.where(hot2, before, 0.0), axis=-1, keepdims=True).astype(I32)
    base = base_ref[...] + jnp.sum(member.astype(F32), axis=0, keepdims=True)
    base_ref[...] = base
    cnt_ref[...] = base.astype(I32)
    route = jnp.where(lane == ROUTE_E, e1, 0)
    route = jnp.where(lane == ROUTE_E + 1, e2, route)
    route = jnp.where(lane == ROUTE_RANK, rank1, route)
    route = jnp.where(lane == ROUTE_RANK + 1, rank2, route)
    gate_bits = lax.bitcast_convert_type(jnp.where(lane == ROUTE_GATE, g1, g2), I32)
    route = jnp.where((lane == ROUTE_GATE) | (lane == ROUTE_GATE + 1), gate_bits, route)
    route_ref[...] = route


def _router(x, a, wo, g, w_router, tm=512):
    n, d = x.shape
    ka = a.shape[1]
    wr = jnp.zeros((d, LANES), F32).at[:, :N_EXPERTS].set(w_router)
    wr_hi = wr.astype(BF16)
    wr_lo = (wr - wr_hi.astype(F32)).astype(BF16)
    row = lambda i: (i, 0)
    fix = lambda i: (0, 0)
    return pl.pallas_call(
        _router_kernel,
        grid=(n // tm,),
        in_specs=[pl.BlockSpec((tm, d), row), pl.BlockSpec((tm, ka), row),
                  pl.BlockSpec((ka, d), fix), pl.BlockSpec((1, d), fix),
                  pl.BlockSpec((d, LANES), fix), pl.BlockSpec((d, LANES), fix)],
        out_specs=[pl.BlockSpec((tm, d), row), pl.BlockSpec((tm, d), row),
                   pl.BlockSpec((tm, LANES), row), pl.BlockSpec((1, LANES), fix)],
        out_shape=[jax.ShapeDtypeStruct((n, d), F32), jax.ShapeDtypeStruct((n, d), F32),
                   jax.ShapeDtypeStruct((n, LANES), I32), jax.ShapeDtypeStruct((1, LANES), I32)],
        scratch_shapes=[pltpu.VMEM((1, LANES), F32)],
        compiler_params=_params("arbitrary"),
        name="router",
    )(x, a, wo, g.reshape(1, d), wr_hi, wr_lo)


def _dispatch_kernel(nb_ref, src_ref, h_hbm, o_ref, sem):
    tm = o_ref.shape[0]

    @pl.when(pl.program_id(0) < nb_ref[0])
    def _():
        def issue(r, carry):
            pltpu.make_async_copy(h_hbm.at[pl.ds(src_ref[r], 1)], o_ref.at[pl.ds(r, 1)], sem).start()
            return carry

        lax.fori_loop(0, tm, issue, 0)
        pltpu.make_async_copy(h_hbm.at[pl.ds(0, tm)], o_ref, sem).wait()

    @pl.when(pl.program_id(0) >= nb_ref[0])
    def _():
        o_ref[...] = jnp.zeros_like(o_ref)


def _dispatch(h, src, nb_used, tm):
    n, d = h.shape
    rows = src.shape[0]
    return pl.pallas_call(
        _dispatch_kernel,
        grid=(rows // tm,),
        in_specs=[pl.BlockSpec(memory_space=pltpu.SMEM),
                  pl.BlockSpec((tm,), lambda i: (i,), memory_space=pltpu.SMEM),
                  pl.BlockSpec(memory_space=pl.ANY)],
        out_specs=pl.BlockSpec((tm, d), lambda i: (i, 0)),
        out_shape=jax.ShapeDtypeStruct((rows, d), F32),
        scratch_shapes=[pltpu.SemaphoreType.DMA(())],
        compiler_params=_params("arbitrary"),
        name="moe_dispatch",
    )(nb_used, src, h)


def _experts_kernel(be_ref, nb_ref, x_ref, wg_ref, wu_ref, wd_ref, o_ref, h_ref):
    i = pl.program_id(0)
    f = pl.program_id(1)
    used = i < nb_ref[0]

    @pl.when(f == 0)
    def _():
        o_ref[...] = jnp.zeros_like(o_ref)
        h_ref[...] = x_ref[...].astype(BF16)

    @pl.when(used)
    def _():
        h = h_ref[...]
        act = _silu(_dot(h, wg_ref[...])) * _dot(h, wu_ref[...])
        o_ref[...] += _dot(act.astype(BF16), wd_ref[...])


def _experts(xs, blk_e, nb_used, w_gate, w_up, w_down, tm, tf=512):
    rows, d = xs.shape
    ff = w_gate.shape[2]
    nf = ff // tf

    def clamp(i, f, be, nb):
        live = i < nb[0]
        return be[jnp.minimum(i, nb[0] - 1)], jnp.where(live, f, nf - 1)

    def wg_map(i, f, be, nb):
        e, fe = clamp(i, f, be, nb)
        return (e, 0, fe)

    def wd_map(i, f, be, nb):
        e, fe = clamp(i, f, be, nb)
        return (e, fe, 0)

    grid_spec = pltpu.PrefetchScalarGridSpec(
        num_scalar_prefetch=2,
        grid=(rows // tm, nf),
        in_specs=[pl.BlockSpec((tm, d), lambda i, f, be, nb: (i, 0)),
                  pl.BlockSpec((None, d, tf), wg_map),
                  pl.BlockSpec((None, d, tf), wg_map),
                  pl.BlockSpec((None, tf, d), wd_map)],
        out_specs=pl.BlockSpec((tm, d), lambda i, f, be, nb: (i, 0)),
        scratch_shapes=[pltpu.VMEM((tm, d), BF16)],
    )
    return pl.pallas_call(
        _experts_kernel,
        grid_spec=grid_spec,
        out_shape=jax.ShapeDtypeStruct((rows, d), F32),
        compiler_params=_params("parallel", "arbitrary"),
        name="moe_experts",
    )(blk_e, nb_used, xs, w_gate, w_up, w_down)


def _combine_kernel(dest_ref, x_ref, route_ref, g_ref, y_hbm, o_ref, ya_ref, yb_ref, sem):
    tm = o_ref.shape[0]

    def issue(r, carry):
        pltpu.make_async_copy(y_hbm.at[pl.ds(dest_ref[2 * r], 1)], ya_ref.at[pl.ds(r, 1)], sem).start()
        pltpu.make_async_copy(y_hbm.at[pl.ds(dest_ref[2 * r + 1], 1)], yb_ref.at[pl.ds(r, 1)], sem).start()
        return carry

    lax.fori_loop(0, tm, issue, 0)
    pltpu.make_async_copy(y_hbm.at[pl.ds(0, tm)], ya_ref, sem).wait()
    pltpu.make_async_copy(y_hbm.at[pl.ds(0, tm)], yb_ref, sem).wait()
    route = route_ref[...]
    lane = lax.broadcasted_iota(I32, route.shape, 1)
    gates = lax.bitcast_convert_type(route, F32)
    g1 = jnp.sum(jnp.where(lane == ROUTE_GATE, gates, 0.0), axis=-1, keepdims=True)
    g2 = jnp.sum(jnp.where(lane == ROUTE_GATE + 1, gates, 0.0), axis=-1, keepdims=True)
    x4 = x_ref[...] + (ya_ref[...] * g1 + yb_ref[...] * g2)
    o_ref[...] = _rms(x4, g_ref[...])


def _combine(x, route, dest, y, g, tm=512):
    n, d = x.shape
    return pl.pallas_call(
        _combine_kernel,
        grid=(n // tm,),
        in_specs=[pl.BlockSpec((2 * tm,), lambda i: (i,), memory_space=pltpu.SMEM),
                  pl.BlockSpec((tm, d), lambda i: (i, 0)),
                  pl.BlockSpec((tm, LANES), lambda i: (i, 0)),
                  pl.BlockSpec((1, d), lambda i: (0, 0)),
                  pl.BlockSpec(memory_space=pl.ANY)],
        out_specs=pl.BlockSpec((tm, d), lambda i: (i, 0)),
        out_shape=jax.ShapeDtypeStruct((n, d), F32),
        scratch_shapes=[pltpu.VMEM((tm, d), F32), pltpu.VMEM((tm, d), F32),
                        pltpu.SemaphoreType.DMA(())],
        compiler_params=_params("arbitrary"),
        name="moe_combine",
    )(dest, x, route, g.reshape(1, d), y)


def _moe_layout(route, counts, n, tm):
    e = route[:, ROUTE_E:ROUTE_E + 2]
    rank = route[:, ROUTE_RANK:ROUTE_RANK + 2]
    sizes = counts[0, :N_EXPERTS]
    padded = ((sizes + tm - 1) // tm) * tm
    pend = jnp.cumsum(padded)
    pstart = pend - padded
    start_of = jnp.zeros_like(e)
    for j in range(N_EXPERTS):
        start_of = jnp.where(e == j, pstart[j], start_of)
    dest = (start_of + rank).reshape(-1).astype(I32)
    n_blocks = (2 * n) // tm + N_EXPERTS
    blk_e = jnp.minimum(jnp.searchsorted(pend, jnp.arange(n_blocks, dtype=I32) * tm, side='right'),
                        N_EXPERTS - 1).astype(I32)
    nb_used = (pend[-1:] // tm).astype(I32)
    src = jnp.zeros((n_blocks * tm,), I32).at[dest].set(jnp.arange(2 * n, dtype=I32) // 2)
    return dest, src, blk_e, nb_used


def kernel(x, positions, g_mix, g_ffn, g_final, w_in_attn, w_out_attn, attn_sinks, w_in_rec,
           rec_lower_bounds, rec_norm_g, w_out_rec, w_gate_dense, w_up_dense, w_down_dense,
           w_router, w_gate_moe, w_up_moe, w_down_moe):
    bsz, seq, d = x.shape
    n = bsz * seq
    x0 = x.reshape(n, d)
    bf = lambda w: w.astype(BF16)

    cos_t, sin_t = _rope_tables(positions)
    proj = _attn_inproj(x0, g_mix[0], bf(w_in_attn[0]), cos_t, sin_t).reshape(bsz, seq, -1)
    (w1, d1), (w2, d2), (w3, d3) = A_PATTERNS
    assert d1 == 1
    o3, l3 = _dilated_branch(proj, bsz, seq, w3, d3)
    o2, l2 = _dilated_branch(proj, bsz, seq, w2, d2)
    mix_a = _dilated_branch(proj, bsz, seq, w1, d1, others=(o2, l2, o3, l3))
    mix_b = _swa_gqa(proj, attn_sinks[0], bsz, seq)
    a_w = A_HEADS * HEAD_DIM
    w_out = bf(w_out_attn[0])
    x2 = _outproj_ffn(x0, mix_a.reshape(n, -1), mix_b.reshape(n, -1), w_out[:a_w], w_out[a_w:],
                      g_ffn[0], bf(w_gate_dense[0]), bf(w_up_dense[0]), bf(w_down_dense[0]))

    lb = jax.nn.softmax(rec_lower_bounds.astype(F32), axis=0)
    lb1 = (jnp.cumsum(lb, axis=0) - lb[0])[1]
    kw = lb1.shape[0]
    vw = rec_norm_g.shape[1]
    q, k, lf, v, gate = _rec_inproj(x2, g_mix[1], bf(w_in_rec[0]), lb1, kw, vw)
    rec = _hgrn(q, k, lf, v, gate, rec_norm_g[0], bsz, seq)
    x3, h, route, counts = _router(x2, rec, bf(w_out_rec[0]), g_ffn[1], w_router[0])
    tm_moe = min(1024, n)
    dest, src, blk_e, nb_used = _moe_layout(route, counts, n, tm_moe)
    xs = _dispatch(h, src, nb_used, tm_moe)
    ys = _experts(xs, blk_e, nb_used, bf(w_gate_moe[0]), bf(w_up_moe[0]), bf(w_down_moe[0]), tm_moe)
    out = _combine(x3, route, dest, ys, g_final)
    return out.reshape(bsz, seq, d)
```

```python
import functools

import jax
import jax.numpy as jnp
from jax import lax
from jax.experimental import pallas as pl
from jax.experimental.pallas import tpu as pltpu

F32 = jnp.float32
BF16 = jnp.bfloat16
I32 = jnp.int32

EPS = 1e-6
HEAD_DIM = 64
ROT_DIM = HEAD_DIM // 4
ROPE_THETA = 500000.0
LANES = 128
ATTN_BLOCK = 128
A_HEADS = 8
A_PATTERNS = ((128, 1), (512, 4), (2048, 16))
B_Q_HEADS = 8
B_KV_HEADS = 2
B_WINDOW = 128
C_HEADS = 8
N_EXPERTS = 8
REC_CHUNK = 128
REC_SUB = 16
REC_SAFE_DROP = -80.0
DMA_UNROLL = 8
VMEM_LIMIT = 56 * 1024 * 1024


def _params(*sem):
    return pltpu.CompilerParams(dimension_semantics=sem, vmem_limit_bytes=VMEM_LIMIT)


def _rms(x, g):
    return x * lax.rsqrt(jnp.mean(x * x, axis=-1, keepdims=True) + EPS) * g


def _silu(x):
    return x / (1.0 + jnp.exp(-x))


def _dot(a, b):
    return jnp.dot(a, b, preferred_element_type=F32)


def _dot_nt(a, b):
    return lax.dot_general(a, b, (((1,), (1,)), ((), ())), preferred_element_type=F32)


def _dot_tn(a, b):
    return lax.dot_general(a, b, (((0,), (0,)), ((), ())), preferred_element_type=F32)


def _rows(j, size, count=1):
    if isinstance(j, int):
        return slice(j * size, (j + count) * size)
    return pl.ds(pl.multiple_of(j * size, size), count * size)


def _rope_table_kernel(pos_ref, invf_ref, cos_ref, sin_ref):
    ang = pos_ref[...].astype(F32) * invf_ref[...]
    cos_ref[...] = jnp.cos(ang)
    sin_ref[...] = jnp.sin(ang)


def _rope_tables(positions):
    n = positions.size
    half = ROT_DIM // 2
    per_row = LANES // half
    inv_freq = jnp.power(ROPE_THETA, -jnp.arange(0, ROT_DIM, 2, dtype=F32) / ROT_DIM)
    pos_rep = jnp.repeat(positions.reshape(n // per_row, per_row), half, axis=1)
    invf_row = jnp.tile(inv_freq, per_row).reshape(1, LANES)
    rows = n // per_row
    tr = min(rows, 512)
    cos8, sin8 = pl.pallas_call(
        _rope_table_kernel,
        grid=(rows // tr,),
        in_specs=[pl.BlockSpec((tr, LANES), lambda i: (i, 0)),
                  pl.BlockSpec((1, LANES), lambda i: (0, 0))],
        out_specs=[pl.BlockSpec((tr, LANES), lambda i: (i, 0))] * 2,
        out_shape=[jax.ShapeDtypeStruct((rows, LANES), F32)] * 2,
        compiler_params=_params("parallel"),
        name="rope_tables",
    )(pos_rep, invf_row)
    cos8 = cos8.reshape(n, half)
    sin8 = sin8.reshape(n, half)
    rest = HEAD_DIM - ROT_DIM
    c64 = jnp.concatenate([cos8, cos8, jnp.ones((n, rest), F32)], axis=1)
    s64 = jnp.concatenate([-sin8, sin8, jnp.zeros((n, rest), F32)], axis=1)
    return jnp.tile(c64, (1, 2)), jnp.tile(s64, (1, 2))


A_BLOCKS = A_HEADS * HEAD_DIM // LANES
BQ_BLOCKS = B_Q_HEADS * HEAD_DIM // LANES
A_QKV = 3 * A_BLOCKS * LANES
B_QKV = (BQ_BLOCKS + 2 * B_KV_HEADS) * LANES


def _attn_inproj_kernel(x_ref, g_ref, w_ref, c_ref, s_ref, o1_ref, o4_ref, o16_ref, ob_ref, sc_ref):
    h = _rms(x_ref[0], g_ref[...]).astype(BF16)
    acc = _dot(h, w_ref[...])
    tm = acc.shape[0]
    c = c_ref[0]
    s = s_ref[0]
    lane = lax.broadcasted_iota(I32, c.shape, 1)
    first = (lane % HEAD_DIM) < (ROT_DIM // 2)
    lo_half = lane < HEAD_DIM

    def rope(blk):
        up = pltpu.roll(blk, LANES - ROT_DIM // 2, 1)
        dn = pltpu.roll(blk, ROT_DIM // 2, 1)
        return blk * c + jnp.where(first, up, dn) * s

    def col(cb):
        return acc[:, cb * LANES:(cb + 1) * LANES]

    scale = HEAD_DIM ** -0.5
    for cb in range(3 * A_BLOCKS):
        blk = col(cb)
        if cb < A_BLOCKS:
            blk = rope(blk) * scale
        elif cb < 2 * A_BLOCKS:
            blk = rope(blk)
        sc_ref[cb] = blk
        o1_ref[0, :, cb * LANES:(cb + 1) * LANES] = blk.astype(BF16)
    for o_ref in (o4_ref, o16_ref):
        dil = o_ref.shape[1]
        for r in range(dil):
            for cb in range(3 * A_BLOCKS):
                o_ref[0, r, :, cb * LANES:(cb + 1) * LANES] = (
                    sc_ref[cb, pl.ds(r, tm // dil, stride=dil), :].astype(BF16))
    base = 3 * A_BLOCKS
    for j in range(BQ_BLOCKS):
        ob_ref[0, :, j * LANES:(j + 1) * LANES] = (rope(col(base + j)) * scale).astype(BF16)
    for j, blk in enumerate((rope(col(base + BQ_BLOCKS)), col(base + BQ_BLOCKS + 1))):
        swapped = pltpu.roll(blk, HEAD_DIM, 1)
        for g, dup in enumerate((jnp.where(lo_half, blk, swapped), jnp.where(lo_half, swapped, blk))):
            cb = BQ_BLOCKS + j * B_KV_HEADS + g
            ob_ref[0, :, cb * LANES:(cb + 1) * LANES] = dup.astype(BF16)


def _attn_inproj(x, g, w, cos_t, sin_t, tm=512):
    bsz, seq, d = x.shape
    width = w.shape[1]
    assert B_KV_HEADS * HEAD_DIM == LANES and width == A_QKV + (BQ_BLOCKS + 2) * LANES
    tile = lambda b, i: (b, i, 0)
    fix = lambda b, i: (0, 0)
    d4, d16 = A_PATTERNS[1][1], A_PATTERNS[2][1]
    perm = lambda dil: pl.BlockSpec((1, dil, tm // dil, A_QKV), lambda b, i: (b, 0, i, 0))
    return pl.pallas_call(
        _attn_inproj_kernel,
        grid=(bsz, seq // tm),
        in_specs=[pl.BlockSpec((1, tm, d), tile),
                  pl.BlockSpec((1, d), fix),
                  pl.BlockSpec((d, width), fix),
                  pl.BlockSpec((1, tm, LANES), tile),
                  pl.BlockSpec((1, tm, LANES), tile)],
        out_specs=[pl.BlockSpec((1, tm, A_QKV), tile), perm(d4), perm(d16),
                   pl.BlockSpec((1, tm, B_QKV), tile)],
        out_shape=[jax.ShapeDtypeStruct((bsz, seq, A_QKV), BF16),
                   jax.ShapeDtypeStruct((bsz, d4, seq // d4, A_QKV), BF16),
                   jax.ShapeDtypeStruct((bsz, d16, seq // d16, A_QKV), BF16),
                   jax.ShapeDtypeStruct((bsz, seq, B_QKV), BF16)],
        scratch_shapes=[pltpu.VMEM((3 * A_BLOCKS, tm, LANES), F32)],
        compiler_params=_params("parallel", "parallel"),
        name="attn_inproj",
    )(x, g.reshape(1, d), w, cos_t.reshape(bsz, seq, LANES), sin_t.reshape(bsz, seq, LANES))


def _band_mask(n_back):
    qi = lax.broadcasted_iota(I32, (ATTN_BLOCK, 2 * ATTN_BLOCK), 0)
    kj = lax.broadcasted_iota(I32, (ATTN_BLOCK, 2 * ATTN_BLOCK), 1)
    dist = ATTN_BLOCK + qi - kj
    return (dist >= 0) & (dist <= n_back), kj >= ATTN_BLOCK


def _band_block(q_pairs, kk, vv, valid, sinks=None, want_lse=True):
    lane = lax.broadcasted_iota(I32, (ATTN_BLOCK, LANES), 1)
    lo_half = lane < HEAD_DIM
    zero = jnp.zeros((ATTN_BLOCK, LANES), BF16)
    lhs = []
    for q in q_pairs:
        lhs += [jnp.where(lo_half, q, zero), jnp.where(lo_half, zero, q)]
    s_all = _dot_nt(jnp.concatenate(lhs, axis=0), kk)
    ps, inv_l, lses = [], [], []
    for u in range(len(lhs)):
        s = jnp.where(valid, s_all[u * ATTN_BLOCK:(u + 1) * ATTN_BLOCK], -jnp.inf)
        m = jnp.max(s, axis=-1, keepdims=True)
        if sinks is not None:
            m = jnp.maximum(m, sinks[u])
        e = jnp.exp(s - m)
        l = jnp.sum(e, axis=-1, keepdims=True)
        if sinks is not None:
            l = l + jnp.exp(sinks[u] - m)
        ps.append(e.astype(BF16))
        inv_l.append(1.0 / l)
        lses.append(m + jnp.log(l) if want_lse else None)
    o_all = _dot(jnp.concatenate(ps, axis=0), vv)
    outs = []
    for j in range(len(q_pairs)):
        o0 = o_all[(2 * j) * ATTN_BLOCK:(2 * j + 1) * ATTN_BLOCK] * inv_l[2 * j]
        o1 = o_all[(2 * j + 1) * ATTN_BLOCK:(2 * j + 2) * ATTN_BLOCK] * inv_l[2 * j + 1]
        lse = jnp.where(lo_half, lses[2 * j], lses[2 * j + 1]) if want_lse else None
        outs.append((jnp.where(lo_half, o0, o1), lse))
    return outs


def _for_each_block(nq, first_fn, rest_fn):
    first_fn()
    if nq > 1:
        def body(jb, carry):
            rest_fn(jb)
            return carry
        lax.fori_loop(1, nq, body, 0)


def _dilated_kernel(q1, kp1, kc1, vp1, vc1, q4, kp4, kc4, vp4, vc4, q16, kp16, kc16, vp16, vc16,
                    o_ref, o4_s, l4_s, o16_s, l16_s, *, n_backs):
    tile = o_ref.shape[1]
    not_first = pl.program_id(2) > 0

    def masks(n_back):
        band, in_cur = _band_mask(n_back)
        return band & (in_cur | not_first), band

    def halo(p_ref, c_ref, idx):
        return jnp.concatenate([p_ref[idx], c_ref[idx + (slice(0, ATTN_BLOCK),)]], axis=0)

    def window(c_ref, idx, jb):
        return c_ref[idx + (_rows(jb - 1, ATTN_BLOCK, 2),)]

    def rows(jb):
        return _rows(jb, ATTN_BLOCK)

    for (q, kp, kc, vp, vc, o_s, l_s), n_back in zip(
            ((q16, kp16, kc16, vp16, vc16, o16_s, l16_s), (q4, kp4, kc4, vp4, vc4, o4_s, l4_s)),
            (n_backs[2], n_backs[1])):
        dil = q.shape[1]
        nq = q.shape[2] // ATTN_BLOCK
        valid0, valid = masks(n_back)
        for r in range(dil):
            idx = (0, r)

            def put(jb, res, r=r, dil=dil, o_s=o_s, l_s=l_s):
                (o, lse), = res
                dst = pl.ds(jb * ATTN_BLOCK * dil + r, ATTN_BLOCK, stride=dil)
                o_s[dst, :] = o
                l_s[dst, :] = lse

            def first(q=q, kp=kp, kc=kc, vp=vp, vc=vc, idx=idx, put=put, valid0=valid0):
                put(0, _band_block([q[idx + (slice(0, ATTN_BLOCK),)]], halo(kp, kc, idx),
                                   halo(vp, vc, idx), valid0))

            def rest(jb, q=q, kc=kc, vc=vc, idx=idx, put=put, valid=valid):
                put(jb, _band_block([q[idx + (rows(jb),)]], window(kc, idx, jb),
                                    window(vc, idx, jb), valid))

            _for_each_block(nq, first, rest)

    valid0, valid = masks(n_backs[0])
    idx = (0,)

    def merge(jb, res):
        (o, lse), = res
        dst = rows(jb)
        l4 = l4_s[dst, :]
        l16 = l16_s[dst, :]
        mx = jnp.maximum(jnp.maximum(lse, l4), l16)
        w1 = jnp.exp(lse - mx)
        w4 = jnp.exp(l4 - mx)
        w16 = jnp.exp(l16 - mx)
        num = w1 * o + w4 * o4_s[dst, :] + w16 * o16_s[dst, :]
        o_ref[0, dst, :] = (num / (w1 + w4 + w16)).astype(o_ref.dtype)

    _for_each_block(
        tile // ATTN_BLOCK,
        lambda: merge(0, _band_block([q1[0, 0:ATTN_BLOCK]], halo(kp1, kc1, idx), halo(vp1, vc1, idx), valid0)),
        lambda jb: merge(jb, _band_block([q1[0, rows(jb)]], window(kc1, idx, jb), window(vc1, idx, jb), valid)))


def _dilated_attention(qkv1, qkv4, qkv16):
    bsz, seq, _ = qkv1.shape
    dils = tuple(p[1] for p in A_PATTERNS)
    assert dils[0] == 1 and qkv4.shape[1] == dils[1] and qkv16.shape[1] == dils[2]
    tile = dils[2] * ATTN_BLOCK
    nt = seq // tile

    def specs(dil):
        rows = tile // dil
        nb = rows // ATTN_BLOCK
        if dil == 1:
            cur = lambda off: pl.BlockSpec((1, rows, LANES), lambda b, hp, i: (b, i, off + hp))
            prev = lambda off: pl.BlockSpec(
                (1, ATTN_BLOCK, LANES), lambda b, hp, i: (b, jnp.maximum(i * nb - 1, 0), off + hp))
        else:
            cur = lambda off: pl.BlockSpec((1, dil, rows, LANES), lambda b, hp, i: (b, 0, i, off + hp))
            prev = lambda off: pl.BlockSpec(
                (1, dil, ATTN_BLOCK, LANES), lambda b, hp, i: (b, 0, jnp.maximum(i * nb - 1, 0), off + hp))
        return [cur(0), prev(A_BLOCKS), cur(A_BLOCKS), prev(2 * A_BLOCKS), cur(2 * A_BLOCKS)]

    kern = functools.partial(_dilated_kernel, n_backs=tuple(w // d for w, d in A_PATTERNS))
    return pl.pallas_call(
        kern,
        grid=(bsz, A_BLOCKS, nt),
        in_specs=specs(1) + specs(dils[1]) + specs(dils[2]),
        out_specs=pl.BlockSpec((1, tile, LANES), lambda b, hp, i: (b, i, hp)),
        out_shape=jax.ShapeDtypeStruct((bsz, seq, A_BLOCKS * LANES), BF16),
        scratch_shapes=[pltpu.VMEM((tile, LANES), F32)] * 4,
        compiler_params=_params("parallel", "parallel", "arbitrary"),
        name="dilated_attn",
    )(*([qkv1] * 5 + [qkv4] * 5 + [qkv16] * 5))


def _swa_kernel(sink_ref, q_ref, *rest, n_back, nq):
    kv = rest[:4 * B_KV_HEADS]
    o_ref = rest[4 * B_KV_HEADS]
    not_first = pl.program_id(1) > 0
    band, in_cur = _band_mask(n_back)
    valid0 = band & (in_cur | not_first)
    pairs = BQ_BLOCKS // B_KV_HEADS

    def run(q_rows, kv_of, valid):
        for g in range(B_KV_HEADS):
            cbs = [g * pairs + j for j in range(pairs)]
            qs = [q_ref[0, q_rows, cb * LANES:(cb + 1) * LANES] for cb in cbs]
            sinks = [sink_ref[2 * cb + p] for cb in cbs for p in range(2)]
            kk, vv = kv_of(g)
            res = _band_block(qs, kk, vv, valid, sinks, want_lse=False)
            for cb, (o, _) in zip(cbs, res):
                o_ref[0, q_rows, cb * LANES:(cb + 1) * LANES] = o.astype(o_ref.dtype)

    def halo_kv(g):
        kp, kc, vp, vc = kv[4 * g:4 * g + 4]
        return (jnp.concatenate([kp[0], kc[0, 0:ATTN_BLOCK]], axis=0),
                jnp.concatenate([vp[0], vc[0, 0:ATTN_BLOCK]], axis=0))

    def window_kv(jb):
        win = _rows(jb - 1, ATTN_BLOCK, 2)
        return lambda g: (kv[4 * g + 1][0, win], kv[4 * g + 3][0, win])

    _for_each_block(
        nq,
        lambda: run(_rows(0, ATTN_BLOCK), halo_kv, valid0),
        lambda jb: run(_rows(jb, ATTN_BLOCK), window_kv(jb), band))


def _swa_gqa(qkvb, sinks, tq=512):
    bsz, seq, _ = qkvb.shape
    tq = min(seq, tq)
    nq = tq // ATTN_BLOCK
    bq_w = BQ_BLOCKS * LANES
    in_specs = [pl.BlockSpec(memory_space=pltpu.SMEM),
                pl.BlockSpec((1, tq, bq_w), lambda b, i: (b, i, 0))]
    for g in range(B_KV_HEADS):
        for section in range(2):
            cb = BQ_BLOCKS + section * B_KV_HEADS + g
            in_specs += [pl.BlockSpec((1, ATTN_BLOCK, LANES),
                                      lambda b, i, cb=cb: (b, jnp.maximum(i * nq - 1, 0), cb)),
                         pl.BlockSpec((1, tq, LANES), lambda b, i, cb=cb: (b, i, cb))]
    args = [sinks.astype(F32)] + [qkvb] * (1 + 4 * B_KV_HEADS)
    kern = functools.partial(_swa_kernel, n_back=B_WINDOW - 1, nq=nq)
    return pl.pallas_call(
        kern,
        grid=(bsz, seq // tq),
        in_specs=in_specs,
        out_specs=pl.BlockSpec((1, tq, bq_w), lambda b, i: (b, i, 0)),
        out_shape=jax.ShapeDtypeStruct((bsz, seq, bq_w), BF16),
        compiler_params=_params("parallel", "arbitrary"),
        name="swa_gqa",
    )(*args)


def _outproj_ffn_kernel(x_ref, a_ref, b_ref, wa_ref, wb_ref, g_ref, wg_ref, wu_ref, wd_ref,
                        o_ref, h_ref):
    f = pl.program_id(1)

    @pl.when(f == 0)
    def _():
        x1 = x_ref[...] + _dot(a_ref[...], wa_ref[...]) + _dot(b_ref[...], wb_ref[...])
        o_ref[...] = x1
        h_ref[...] = _rms(x1, g_ref[...]).astype(BF16)

    h = h_ref[...]
    act = _silu(_dot(h, wg_ref[...])) * _dot(h, wu_ref[...])
    o_ref[...] += _dot(act.astype(BF16), wd_ref[...])


def _outproj_ffn(x, mix_a, mix_b, w_a, w_b, g, w_gate, w_up, w_down, tm=1024, tf=512):
    n, d = x.shape
    ff = w_gate.shape[1]
    ka, kb = mix_a.shape[1], mix_b.shape[1]
    return pl.pallas_call(
        _outproj_ffn_kernel,
        grid=(n // tm, ff // tf),
        in_specs=[pl.BlockSpec((tm, d), lambda i, f: (i, 0)),
                  pl.BlockSpec((tm, ka), lambda i, f: (i, 0)),
                  pl.BlockSpec((tm, kb), lambda i, f: (i, 0)),
                  pl.BlockSpec((ka, d), lambda i, f: (0, 0)),
                  pl.BlockSpec((kb, d), lambda i, f: (0, 0)),
                  pl.BlockSpec((1, d), lambda i, f: (0, 0)),
                  pl.BlockSpec((d, tf), lambda i, f: (0, f)),
                  pl.BlockSpec((d, tf), lambda i, f: (0, f)),
                  pl.BlockSpec((tf, d), lambda i, f: (f, 0))],
        out_specs=pl.BlockSpec((tm, d), lambda i, f: (i, 0)),
        out_shape=jax.ShapeDtypeStruct((n, d), F32),
        scratch_shapes=[pltpu.VMEM((tm, d), BF16)],
        compiler_params=_params("parallel", "arbitrary"),
        name="outproj_ffn",
    )(x, mix_a, mix_b, w_a, w_b, g.reshape(1, d), w_gate, w_up, w_down)


def _rec_inproj_kernel(x_ref, g_ref, w_ref, lb_ref, q_ref, k_ref, lf_ref, v_ref, gate_ref):
    h = _rms(x_ref[...], g_ref[...]).astype(BF16)
    acc = _dot(h, w_ref[...])
    kw = q_ref.shape[1]
    vw = v_ref.shape[1]
    lb = lb_ref[...]
    q_ref[...] = _silu(acc[:, :kw]).astype(BF16)
    fg = lb + (1.0 - lb) / (1.0 + jnp.exp(-acc[:, kw:2 * kw]))
    k_ref[...] = (1.0 - fg).astype(BF16)
    lf_ref[...] = jnp.log(fg)
    v_ref[...] = acc[:, 2 * kw:2 * kw + vw].astype(BF16)
    gate_ref[...] = _silu(acc[:, 2 * kw + vw:]).astype(BF16)


def _rec_inproj(x, g, w, lb, kw, vw, tm=512):
    n, d = x.shape
    width = w.shape[1]
    row = lambda i: (i, 0)
    fix = lambda i: (0, 0)
    return pl.pallas_call(
        _rec_inproj_kernel,
        grid=(n // tm,),
        in_specs=[pl.BlockSpec((tm, d), row),
                  pl.BlockSpec((1, d), fix),
                  pl.BlockSpec((d, width), fix),
                  pl.BlockSpec((1, kw), fix)],
        out_specs=[pl.BlockSpec((tm, kw), row), pl.BlockSpec((tm, kw), row),
                   pl.BlockSpec((tm, kw), row), pl.BlockSpec((tm, vw), row),
                   pl.BlockSpec((tm, vw), row)],
        out_shape=[jax.ShapeDtypeStruct((n, kw), BF16), jax.ShapeDtypeStruct((n, kw), BF16),
                   jax.ShapeDtypeStruct((n, kw), F32), jax.ShapeDtypeStruct((n, vw), BF16),
                   jax.ShapeDtypeStruct((n, vw), BF16)],
        compiler_params=_params("parallel"),
        name="rec_inproj",
    )(x, g.reshape(1, d), w, lb.reshape(1, kw))


def _hgrn_kernel(q_ref, k_ref, lf_ref, v_ref, gate_ref, ng_ref, o_ref, st_ref, b_ref, kf_ref, *,
                 n_chunks):
    @pl.when(pl.program_id(2) == 0)
    def _():
        st_ref[...] = jnp.zeros_like(st_ref)

    c_len, sub = REC_CHUNK, REC_SUB
    r_i = lax.broadcasted_iota(I32, (c_len, c_len), 0)
    c_i = lax.broadcasted_iota(I32, (c_len, c_len), 1)
    causal = c_i <= r_i
    tril = causal.astype(BF16)
    ng = ng_ref[...]

    drop = jnp.zeros((1, LANES), F32)
    for c in range(n_chunks):
        rows = slice(c * c_len, (c + 1) * c_len)
        lf = lf_ref[0, rows, :]
        lf1 = lf.astype(BF16)
        rem = lf - lf1.astype(F32)
        lf2 = rem.astype(BF16)
        lf3 = (rem - lf2.astype(F32)).astype(BF16)
        b = _dot(tril, lf1) + _dot(tril, lf2) + _dot(tril, lf3)
        b_ref[rows, :] = b
        for i in range(c_len // sub):
            end = b[(i + 1) * sub - 1:(i + 1) * sub, :]
            drop = jnp.minimum(drop, end - b[i * sub - 1:i * sub, :] if i > 0 else end)
    safe = jnp.min(drop) > REC_SAFE_DROP

    def finish(c, att, st):
        rows = _rows(c, c_len)
        b = b_ref[rows, :]
        q = q_ref[0, rows, :].astype(F32)
        k = k_ref[0, rows, :].astype(F32)
        v = v_ref[0, rows, :]
        b_last = b[c_len - 1:c_len, :]
        qe = (q * jnp.exp(b)).astype(BF16)
        o = _dot(att.astype(BF16), v) + _dot_nt(qe, st.astype(BF16))
        kd = (k * jnp.exp(b_last - b)).astype(BF16)
        st = st * jnp.exp(b_last) + _dot_tn(v, kd)
        y = o * lax.rsqrt(jnp.mean(o * o, axis=-1, keepdims=True) + EPS)
        o_ref[0, rows, :] = (y * ng * gate_ref[0, rows, :].astype(F32)).astype(o_ref.dtype)
        return st

    @pl.when(safe)
    def _():
        st = st_ref[...]
        for c in range(n_chunks):
            rows = slice(c * c_len, (c + 1) * c_len)
            b = b_ref[rows, :]
            q = q_ref[0, rows, :].astype(F32)
            k = k_ref[0, rows, :].astype(F32)
            att_rows = []
            for i in range(c_len // sub):
                lo, hi = i * sub, (i + 1) * sub
                ref = b[lo - 1:lo, :] if i > 0 else jnp.zeros((1, LANES), F32)
                qt = (q[lo:hi] * jnp.exp(b[lo:hi] - ref)).astype(BF16)
                kt = (k[:hi] * jnp.exp(ref - b[:hi])).astype(BF16)
                if hi < c_len:
                    kt = jnp.concatenate([kt, jnp.zeros((c_len - hi, LANES), BF16)], axis=0)
                att_rows.append(_dot_nt(qt, kt))
            att = jnp.where(causal, jnp.concatenate(att_rows, axis=0), 0.0)
            st = finish(c, att, st)
        st_ref[...] = st

    @pl.when(jnp.logical_not(safe))
    def _():
        def chunk(c, st):
            rows = _rows(c, c_len)
            b = b_ref[rows, :]
            q = q_ref[0, rows, :].astype(F32)
            kf_ref[...] = k_ref[0, rows, :].astype(F32)

            def key(s, att):
                b_s = b_ref[pl.ds(c * c_len + s, 1), :]
                dec = jnp.exp(jnp.minimum(b - b_s, 0.0))
                col = jnp.sum(q * kf_ref[pl.ds(s, 1), :] * dec, axis=-1, keepdims=True)
                return jnp.where(c_i == s, col, att)

            att = lax.fori_loop(0, c_len, key, jnp.zeros((c_len, c_len), F32))
            return finish(c, jnp.where(causal, att, 0.0), st)

        st_ref[...] = lax.fori_loop(0, n_chunks, chunk, st_ref[...])


def _hgrn(q, k, lf, v, gate, norm_g, bsz, seq, ts=512):
    kw = q.shape[-1] // C_HEADS
    vw = v.shape[-1] // C_HEADS
    assert kw == LANES and vw == LANES
    ts = min(ts, seq)
    v3 = lambda t: t.reshape(bsz, seq, t.shape[-1])
    blk = pl.BlockSpec((1, ts, LANES), lambda b, h, c: (b, c, h))
    kern = functools.partial(_hgrn_kernel, n_chunks=ts // REC_CHUNK)
    out = pl.pallas_call(
        kern,
        grid=(bsz, C_HEADS, seq // ts),
        in_specs=[blk, blk, blk, blk, blk, pl.BlockSpec((1, LANES), lambda b, h, c: (0, h))],
        out_specs=blk,
        out_shape=jax.ShapeDtypeStruct((bsz, seq, C_HEADS * vw), BF16),
        scratch_shapes=[pltpu.VMEM((vw, kw), F32), pltpu.VMEM((ts, LANES), F32),
                        pltpu.VMEM((REC_CHUNK, LANES), F32)],
        compiler_params=_params("parallel", "parallel", "arbitrary"),
        name="hgrn2",
    )(v3(q), v3(k), v3(lf), v3(v), v3(gate), norm_g.reshape(1, -1))
    return out.reshape(bsz * seq, C_HEADS * vw)


ROUTE_E, ROUTE_RANK, ROUTE_GATE = 0, 2, 4


def _router_kernel(x_ref, a_ref, wo_ref, g_ref, wr_hi_ref, wr_lo_ref, x3_ref, h_ref, route_ref,
                   cnt_ref, base_ref):
    @pl.when(pl.program_id(0) == 0)
    def _():
        base_ref[...] = jnp.zeros_like(base_ref)

    x3 = x_ref[...] + _dot(a_ref[...], wo_ref[...])
    x3_ref[...] = x3
    h = _rms(x3, g_ref[...])
    h_ref[...] = h
    h_hi = h.astype(BF16)
    h_lo = (h - h_hi.astype(F32)).astype(BF16)
    logits = _dot(h_hi, wr_hi_ref[...]) + _dot(h_hi, wr_lo_ref[...]) + _dot(h_lo, wr_hi_ref[...])
    tm = logits.shape[0]
    lane = lax.broadcasted_iota(I32, logits.shape, 1)
    logits = jnp.where(lane < N_EXPERTS, logits, -jnp.inf)
    lane_f = lane.astype(F32)
    v1 = jnp.max(logits, axis=-1, keepdims=True)
    e1 = jnp.min(jnp.where(logits == v1, lane_f, float(LANES)), axis=-1, keepdims=True)
    hot1 = lane_f == e1
    rest = jnp.where(hot1, -jnp.inf, logits)
    v2 = jnp.max(rest, axis=-1, keepdims=True)
    e2 = jnp.min(jnp.where(rest == v2, lane_f, float(LANES)), axis=-1, keepdims=True)
    hot2 = lane_f == e2
    t = jnp.exp(v2 - v1)
    g1 = 1.0 / (1.0 + t)
    g2 = t / (1.0 + t)
    e1 = e1.astype(I32)
    e2 = e2.astype(I32)
    member = (hot1 | hot2).astype(BF16)
    r_i = lax.broadcasted_iota(I32, (tm, tm), 0)
    c_i = lax.broadcasted_iota(I32, (tm, tm), 1)
    before = _dot((c_i < r_i).astype(BF16), member) + base_ref[...]
    rank1 = jnp.sum(jnp.where(hot1, before, 0.0), axis=-1, keepdims=True).astype(I32)
    rank2 = jnp.sum(jnp.where(hot2, before, 0.0), axis=-1, keepdims=True).astype(I32)
    base = base_ref[...] + jnp.sum(member.astype(F32), axis=0, keepdims=True)
    base_ref[...] = base
    cnt_ref[...] = base.astype(I32)
    route = jnp.where(lane == ROUTE_E, e1, 0)
    route = jnp.where(lane == ROUTE_E + 1, e2, route)
    route = jnp.where(lane == ROUTE_RANK, rank1, route)
    route = jnp.where(lane == ROUTE_RANK + 1, rank2, route)
    gate_bits = lax.bitcast_convert_type(jnp.where(lane == ROUTE_GATE, g1, g2), I32)
    route = jnp.where((lane == ROUTE_GATE) | (lane == ROUTE_GATE + 1), gate_bits, route)
    route_ref[...] = route


def _router(x, a, wo, g, w_router, tm=512):
    n, d = x.shape
    ka = a.shape[1]
    wr = jnp.zeros((d, LANES), F32).at[:, :N_EXPERTS].set(w_router)
    wr_hi = wr.astype(BF16)
    wr_lo = (wr - wr_hi.astype(F32)).astype(BF16)
    row = lambda i: (i, 0)
    fix = lambda i: (0, 0)
    return pl.pallas_call(
        _router_kernel,
        grid=(n // tm,),
        in_specs=[pl.BlockSpec((tm, d), row), pl.BlockSpec((tm, ka), row),
                  pl.BlockSpec((ka, d), fix), pl.BlockSpec((1, d), fix),
                  pl.BlockSpec((d, LANES), fix), pl.BlockSpec((d, LANES), fix)],
        out_specs=[pl.BlockSpec((tm, d), row), pl.BlockSpec((tm, d), row),
                   pl.BlockSpec((tm, LANES), row), pl.BlockSpec((1, LANES), fix)],
        out_shape=[jax.ShapeDtypeStruct((n, d), F32), jax.ShapeDtypeStruct((n, d), F32),
                   jax.ShapeDtypeStruct((n, LANES), I32), jax.ShapeDtypeStruct((1, LANES), I32)],
        scratch_shapes=[pltpu.VMEM((1, LANES), F32)],
        compiler_params=_params("arbitrary"),
        name="router",
    )(x, a, wo, g.reshape(1, d), wr_hi, wr_lo)


def _issue_rows(n_rows, copy_of):
    def body(it, carry):
        for u in range(DMA_UNROLL):
            for cp in copy_of(it * DMA_UNROLL + u):
                cp.start()
        return carry

    lax.fori_loop(0, n_rows // DMA_UNROLL, body, 0)


def _dispatch_kernel(nb_ref, src_ref, h_hbm, o_ref, sem):
    tm = o_ref.shape[0]

    @pl.when(pl.program_id(0) < nb_ref[0])
    def _():
        _issue_rows(tm, lambda r: [pltpu.make_async_copy(
            h_hbm.at[pl.ds(src_ref[r], 1)], o_ref.at[pl.ds(r, 1)], sem)])
        pltpu.make_async_copy(h_hbm.at[pl.ds(0, tm)], o_ref, sem).wait()

    @pl.when(pl.program_id(0) >= nb_ref[0])
    def _():
        o_ref[...] = jnp.zeros_like(o_ref)


def _dispatch(h, src, nb_used, tm):
    n, d = h.shape
    rows = src.shape[0]
    return pl.pallas_call(
        _dispatch_kernel,
        grid=(rows // tm,),
        in_specs=[pl.BlockSpec(memory_space=pltpu.SMEM),
                  pl.BlockSpec((tm,), lambda i: (i,), memory_space=pltpu.SMEM),
                  pl.BlockSpec(memory_space=pl.ANY)],
        out_specs=pl.BlockSpec((tm, d), lambda i: (i, 0)),
        out_shape=jax.ShapeDtypeStruct((rows, d), F32),
        scratch_shapes=[pltpu.SemaphoreType.DMA(())],
        compiler_params=_params("arbitrary"),
        name="moe_dispatch",
    )(nb_used, src, h)


def _experts_kernel(be_ref, nb_ref, x_ref, wg_ref, wu_ref, wd_ref, o_ref, h_ref):
    i = pl.program_id(0)
    f = pl.program_id(1)
    used = i < nb_ref[0]

    @pl.when(f == 0)
    def _():
        o_ref[...] = jnp.zeros_like(o_ref)
        h_ref[...] = x_ref[...].astype(BF16)

    @pl.when(used)
    def _():
        h = h_ref[...]
        act = _silu(_dot(h, wg_ref[...])) * _dot(h, wu_ref[...])
        o_ref[...] += _dot(act.astype(BF16), wd_ref[...])


def _experts(xs, blk_e, nb_used, w_gate, w_up, w_down, tm, tf=512):
    rows, d = xs.shape
    ff = w_gate.shape[2]
    nf = ff // tf

    def clamp(i, f, be, nb):
        live = i < nb[0]
        return be[jnp.minimum(i, nb[0] - 1)], jnp.where(live, f, nf - 1)

    def wg_map(i, f, be, nb):
        e, fe = clamp(i, f, be, nb)
        return (e, 0, fe)

    def wd_map(i, f, be, nb):
        e, fe = clamp(i, f, be, nb)
        return (e, fe, 0)

    grid_spec = pltpu.PrefetchScalarGridSpec(
        num_scalar_prefetch=2,
        grid=(rows // tm, nf),
        in_specs=[pl.BlockSpec((tm, d), lambda i, f, be, nb: (i, 0)),
                  pl.BlockSpec((None, d, tf), wg_map),
                  pl.BlockSpec((None, d, tf), wg_map),
                  pl.BlockSpec((None, tf, d), wd_map)],
        out_specs=pl.BlockSpec((tm, d), lambda i, f, be, nb: (i, 0)),
        scratch_shapes=[pltpu.VMEM((tm, d), BF16)],
    )
    return pl.pallas_call(
        _experts_kernel,
        grid_spec=grid_spec,
        out_shape=jax.ShapeDtypeStruct((rows, d), F32),
        compiler_params=_params("parallel", "arbitrary"),
        name="moe_experts",
    )(blk_e, nb_used, xs, w_gate, w_up, w_down)


def _combine_kernel(dest_ref, x_ref, route_ref, g_ref, y_hbm, o_ref, ya_ref, yb_ref, sem):
    tm = o_ref.shape[0]
    _issue_rows(tm, lambda r: [
        pltpu.make_async_copy(y_hbm.at[pl.ds(dest_ref[2 * r], 1)], ya_ref.at[pl.ds(r, 1)], sem),
        pltpu.make_async_copy(y_hbm.at[pl.ds(dest_ref[2 * r + 1], 1)], yb_ref.at[pl.ds(r, 1)], sem)])
    pltpu.make_async_copy(y_hbm.at[pl.ds(0, tm)], ya_ref, sem).wait()
    pltpu.make_async_copy(y_hbm.at[pl.ds(0, tm)], yb_ref, sem).wait()
    route = route_ref[...]
    lane = lax.broadcasted_iota(I32, route.shape, 1)
    gates = lax.bitcast_convert_type(route, F32)
    g1 = jnp.sum(jnp.where(lane == ROUTE_GATE, gates, 0.0), axis=-1, keepdims=True)
    g2 = jnp.sum(jnp.where(lane == ROUTE_GATE + 1, gates, 0.0), axis=-1, keepdims=True)
    x4 = x_ref[...] + (ya_ref[...] * g1 + yb_ref[...] * g2)
    o_ref[...] = _rms(x4, g_ref[...])


def _combine(x, route, dest, y, g, tm=512):
    n, d = x.shape
    return pl.pallas_call(
        _combine_kernel,
        grid=(n // tm,),
        in_specs=[pl.BlockSpec((2 * tm,), lambda i: (i,), memory_space=pltpu.SMEM),
                  pl.BlockSpec((tm, d), lambda i: (i, 0)),
                  pl.BlockSpec((tm, LANES), lambda i: (i, 0)),
                  pl.BlockSpec((1, d), lambda i: (0, 0)),
                  pl.BlockSpec(memory_space=pl.ANY)],
        out_specs=pl.BlockSpec((tm, d), lambda i: (i, 0)),
        out_shape=jax.ShapeDtypeStruct((n, d), F32),
        scratch_shapes=[pltpu.VMEM((tm, d), F32), pltpu.VMEM((tm, d), F32),
                        pltpu.SemaphoreType.DMA(())],
        compiler_params=_params("arbitrary"),
        name="moe_combine",
    )(dest, x, route, g.reshape(1, d), y)


def _moe_layout(route, counts, n, tm):
    e = route[:, ROUTE_E:ROUTE_E + 2]
    rank = route[:, ROUTE_RANK:ROUTE_RANK + 2]
    sizes = counts[0, :N_EXPERTS]
    padded = ((sizes + tm - 1) // tm) * tm
    pend = jnp.cumsum(padded)
    pstart = pend - padded
    start_of = jnp.zeros_like(e)
    for j in range(N_EXPERTS):
        start_of = jnp.where(e == j, pstart[j], start_of)
    dest = (start_of + rank).reshape(-1).astype(I32)
    n_blocks = (2 * n) // tm + N_EXPERTS
    blk_e = jnp.minimum(jnp.searchsorted(pend, jnp.arange(n_blocks, dtype=I32) * tm, side='right'),
                        N_EXPERTS - 1).astype(I32)
    nb_used = (pend[-1:] // tm).astype(I32)
    src = jnp.zeros((n_blocks * tm,), I32).at[dest].set(jnp.arange(2 * n, dtype=I32) // 2)
    return dest, src, blk_e, nb_used


def kernel(x, positions, g_mix, g_ffn, g_final, w_in_attn, w_out_attn, attn_sinks, w_in_rec,
           rec_lower_bounds, rec_norm_g, w_out_rec, w_gate_dense, w_up_dense, w_down_dense,
           w_router, w_gate_moe, w_up_moe, w_down_moe):
    bsz, seq, d = x.shape
    n = bsz * seq
    x0 = x.reshape(n, d)
    bf = lambda w: w.astype(BF16)

    cos_t, sin_t = _rope_tables(positions)
    qkv1, qkv4, qkv16, qkvb = _attn_inproj(x, g_mix[0], bf(w_in_attn[0]), cos_t, sin_t)
    mix_a = _dilated_attention(qkv1, qkv4, qkv16)
    mix_b = _swa_gqa(qkvb, attn_sinks[0])
    a_w = A_HEADS * HEAD_DIM
    w_out = bf(w_out_attn[0])
    x2 = _outproj_ffn(x0, mix_a.reshape(n, -1), mix_b.reshape(n, -1), w_out[:a_w], w_out[a_w:],
                      g_ffn[0], bf(w_gate_dense[0]), bf(w_up_dense[0]), bf(w_down_dense[0]))

    lb = jax.nn.softmax(rec_lower_bounds.astype(F32), axis=0)
    lb1 = (jnp.cumsum(lb, axis=0) - lb[0])[1]
    kw = lb1.shape[0]
    vw = rec_norm_g.shape[1]
    q, k, lf, v, gate = _rec_inproj(x2, g_mix[1], bf(w_in_rec[0]), lb1, kw, vw)
    rec = _hgrn(q, k, lf, v, gate, rec_norm_g[0], bsz, seq)
    x3, h, route, counts = _router(x2, rec, bf(w_out_rec[0]), g_ffn[1], w_router[0])
    tm_moe = min(1024, n)
    dest, src, blk_e, nb_used = _moe_layout(route, counts, n, tm_moe)
    xs = _dispatch(h, src, nb_used, tm_moe)
    ys = _experts(xs, blk_e, nb_used, bf(w_gate_moe[0]), bf(w_up_moe[0]), bf(w_down_moe[0]), tm_moe)
    out = _combine(x3, route, dest, ys, g_final)
    return out.reshape(bsz, seq, d)
```

```python
import functools

import jax
import jax.numpy as jnp
from jax import lax
from jax.experimental import pallas as pl
from jax.experimental.pallas import tpu as pltpu

F32 = jnp.float32
BF16 = jnp.bfloat16
I32 = jnp.int32

EPS = 1e-6
HEAD_DIM = 64
ROT_DIM = HEAD_DIM // 4
ROPE_THETA = 500000.0
LANES = 128
ATTN_BLOCK = 128
A_HEADS = 8
A_PATTERNS = ((128, 1), (512, 4), (2048, 16))
B_Q_HEADS = 8
B_KV_HEADS = 2
B_WINDOW = 128
C_HEADS = 8
N_EXPERTS = 8
REC_CHUNK = 128
REC_SUB = 16
REC_SAFE_DROP = -80.0
VMEM_LIMIT = 56 * 1024 * 1024


def _params(*sem):
    return pltpu.CompilerParams(dimension_semantics=sem, vmem_limit_bytes=VMEM_LIMIT)


def _rms(x, g):
    return x * lax.rsqrt(jnp.mean(x * x, axis=-1, keepdims=True) + EPS) * g


def _silu(x):
    return x / (1.0 + jnp.exp(-x))


def _dot(a, b):
    return jnp.dot(a, b, preferred_element_type=F32)


def _dot_nt(a, b):
    return lax.dot_general(a, b, (((1,), (1,)), ((), ())), preferred_element_type=F32)


def _dot_tn(a, b):
    return lax.dot_general(a, b, (((0,), (0,)), ((), ())), preferred_element_type=F32)


def _rows(j, size, count=1):
    if isinstance(j, int):
        return slice(j * size, (j + count) * size)
    return pl.ds(pl.multiple_of(j * size, size), count * size)


def _rope_table_kernel(pos_ref, invf_ref, cos_ref, sin_ref):
    ang = pos_ref[...].astype(F32) * invf_ref[...]
    cos_ref[...] = jnp.cos(ang)
    sin_ref[...] = jnp.sin(ang)


def _rope_tables(positions):
    n = positions.size
    half = ROT_DIM // 2
    per_row = LANES // half
    inv_freq = jnp.power(ROPE_THETA, -jnp.arange(0, ROT_DIM, 2, dtype=F32) / ROT_DIM)
    pos_rep = jnp.repeat(positions.reshape(n // per_row, per_row), half, axis=1)
    invf_row = jnp.tile(inv_freq, per_row).reshape(1, LANES)
    rows = n // per_row
    tr = min(rows, 512)
    cos8, sin8 = pl.pallas_call(
        _rope_table_kernel,
        grid=(rows // tr,),
        in_specs=[pl.BlockSpec((tr, LANES), lambda i: (i, 0)),
                  pl.BlockSpec((1, LANES), lambda i: (0, 0))],
        out_specs=[pl.BlockSpec((tr, LANES), lambda i: (i, 0))] * 2,
        out_shape=[jax.ShapeDtypeStruct((rows, LANES), F32)] * 2,
        compiler_params=_params("parallel"),
        name="rope_tables",
    )(pos_rep, invf_row)
    cos8 = cos8.reshape(n, half)
    sin8 = sin8.reshape(n, half)
    rest = HEAD_DIM - ROT_DIM
    c64 = jnp.concatenate([cos8, cos8, jnp.ones((n, rest), F32)], axis=1)
    s64 = jnp.concatenate([-sin8, sin8, jnp.zeros((n, rest), F32)], axis=1)
    return jnp.tile(c64, (1, 2)), jnp.tile(s64, (1, 2))


A_BLOCKS = A_HEADS * HEAD_DIM // LANES
BQ_BLOCKS = B_Q_HEADS * HEAD_DIM // LANES
A_QKV = 3 * A_BLOCKS * LANES
B_QKV = (BQ_BLOCKS + 2 * B_KV_HEADS) * LANES


def _attn_inproj_kernel(x_ref, g_ref, w_ref, c_ref, s_ref, o1_ref, o4_ref, o16_ref, ob_ref, sc_ref):
    h = _rms(x_ref[0], g_ref[...]).astype(BF16)
    acc = _dot(h, w_ref[...])
    tm = acc.shape[0]
    c = c_ref[0]
    s = s_ref[0]
    lane = lax.broadcasted_iota(I32, c.shape, 1)
    first = (lane % HEAD_DIM) < (ROT_DIM // 2)
    lo_half = lane < HEAD_DIM

    def rope(blk):
        up = pltpu.roll(blk, LANES - ROT_DIM // 2, 1)
        dn = pltpu.roll(blk, ROT_DIM // 2, 1)
        return blk * c + jnp.where(first, up, dn) * s

    def col(cb):
        return acc[:, cb * LANES:(cb + 1) * LANES]

    scale = HEAD_DIM ** -0.5
    for cb in range(3 * A_BLOCKS):
        blk = col(cb)
        if cb < A_BLOCKS:
            blk = rope(blk) * scale
        elif cb < 2 * A_BLOCKS:
            blk = rope(blk)
        sc_ref[cb] = blk
        o1_ref[0, :, cb * LANES:(cb + 1) * LANES] = blk.astype(BF16)
    for o_ref in (o4_ref, o16_ref):
        dil = o_ref.shape[1]
        for r in range(dil):
            for cb in range(3 * A_BLOCKS):
                o_ref[0, r, :, cb * LANES:(cb + 1) * LANES] = (
                    sc_ref[cb, pl.ds(r, tm // dil, stride=dil), :].astype(BF16))
    base = 3 * A_BLOCKS
    for j in range(BQ_BLOCKS):
        ob_ref[0, :, j * LANES:(j + 1) * LANES] = (rope(col(base + j)) * scale).astype(BF16)
    for j, blk in enumerate((rope(col(base + BQ_BLOCKS)), col(base + BQ_BLOCKS + 1))):
        swapped = pltpu.roll(blk, HEAD_DIM, 1)
        for g, dup in enumerate((jnp.where(lo_half, blk, swapped), jnp.where(lo_half, swapped, blk))):
            cb = BQ_BLOCKS + j * B_KV_HEADS + g
            ob_ref[0, :, cb * LANES:(cb + 1) * LANES] = dup.astype(BF16)


def _attn_inproj(x, g, w, cos_t, sin_t, tm=512):
    bsz, seq, d = x.shape
    width = w.shape[1]
    assert B_KV_HEADS * HEAD_DIM == LANES and width == A_QKV + (BQ_BLOCKS + 2) * LANES
    tile = lambda b, i: (b, i, 0)
    fix = lambda b, i: (0, 0)
    d4, d16 = A_PATTERNS[1][1], A_PATTERNS[2][1]
    perm = lambda dil: pl.BlockSpec((1, dil, tm // dil, A_QKV), lambda b, i: (b, 0, i, 0))
    return pl.pallas_call(
        _attn_inproj_kernel,
        grid=(bsz, seq // tm),
        in_specs=[pl.BlockSpec((1, tm, d), tile),
                  pl.BlockSpec((1, d), fix),
                  pl.BlockSpec((d, width), fix),
                  pl.BlockSpec((1, tm, LANES), tile),
                  pl.BlockSpec((1, tm, LANES), tile)],
        out_specs=[pl.BlockSpec((1, tm, A_QKV), tile), perm(d4), perm(d16),
                   pl.BlockSpec((1, tm, B_QKV), tile)],
        out_shape=[jax.ShapeDtypeStruct((bsz, seq, A_QKV), BF16),
                   jax.ShapeDtypeStruct((bsz, d4, seq // d4, A_QKV), BF16),
                   jax.ShapeDtypeStruct((bsz, d16, seq // d16, A_QKV), BF16),
                   jax.ShapeDtypeStruct((bsz, seq, B_QKV), BF16)],
        scratch_shapes=[pltpu.VMEM((3 * A_BLOCKS, tm, LANES), F32)],
        compiler_params=_params("parallel", "parallel"),
        name="attn_inproj",
    )(x, g.reshape(1, d), w, cos_t.reshape(bsz, seq, LANES), sin_t.reshape(bsz, seq, LANES))


def _band_mask(n_back):
    qi = lax.broadcasted_iota(I32, (ATTN_BLOCK, 2 * ATTN_BLOCK), 0)
    kj = lax.broadcasted_iota(I32, (ATTN_BLOCK, 2 * ATTN_BLOCK), 1)
    dist = ATTN_BLOCK + qi - kj
    return (dist >= 0) & (dist <= n_back), kj >= ATTN_BLOCK


def _band_block(q_pairs, kk, vv, valid, sinks=None, want_lse=True):
    lane = lax.broadcasted_iota(I32, (ATTN_BLOCK, LANES), 1)
    lo_half = lane < HEAD_DIM
    zero = jnp.zeros((ATTN_BLOCK, LANES), BF16)
    lhs = []
    for q in q_pairs:
        lhs += [jnp.where(lo_half, q, zero), jnp.where(lo_half, zero, q)]
    s_all = _dot_nt(jnp.concatenate(lhs, axis=0), kk)
    ps, inv_l, lses = [], [], []
    for u in range(len(lhs)):
        s = jnp.where(valid, s_all[u * ATTN_BLOCK:(u + 1) * ATTN_BLOCK], -jnp.inf)
        m = jnp.max(s, axis=-1, keepdims=True)
        if sinks is not None:
            m = jnp.maximum(m, sinks[u])
        e = jnp.exp(s - m)
        l = jnp.sum(e, axis=-1, keepdims=True)
        if sinks is not None:
            l = l + jnp.exp(sinks[u] - m)
        ps.append(e.astype(BF16))
        inv_l.append(1.0 / l)
        lses.append(m + jnp.log(l) if want_lse else None)
    o_all = _dot(jnp.concatenate(ps, axis=0), vv)
    outs = []
    for j in range(len(q_pairs)):
        o0 = o_all[(2 * j) * ATTN_BLOCK:(2 * j + 1) * ATTN_BLOCK] * inv_l[2 * j]
        o1 = o_all[(2 * j + 1) * ATTN_BLOCK:(2 * j + 2) * ATTN_BLOCK] * inv_l[2 * j + 1]
        lse = jnp.where(lo_half, lses[2 * j], lses[2 * j + 1]) if want_lse else None
        outs.append((jnp.where(lo_half, o0, o1), lse))
    return outs


def _for_each_block(nq, first_fn, rest_fn):
    first_fn()
    if nq > 1:
        def body(jb, carry):
            rest_fn(jb)
            return carry
        lax.fori_loop(1, nq, body, 0)


def _dilated_kernel(q1, kp1, kc1, vp1, vc1, q4, kp4, kc4, vp4, vc4, q16, kp16, kc16, vp16, vc16,
                    o_ref, o4_s, l4_s, o16_s, l16_s, *, n_backs):
    tile = o_ref.shape[1]
    not_first = pl.program_id(2) > 0

    def masks(n_back):
        band, in_cur = _band_mask(n_back)
        return band & (in_cur | not_first), band

    def halo(p_ref, c_ref, idx):
        return jnp.concatenate([p_ref[idx], c_ref[idx + (slice(0, ATTN_BLOCK),)]], axis=0)

    def window(c_ref, idx, jb):
        return c_ref[idx + (_rows(jb - 1, ATTN_BLOCK, 2),)]

    def rows(jb):
        return _rows(jb, ATTN_BLOCK)

    for (q, kp, kc, vp, vc, o_s, l_s), n_back in zip(
            ((q16, kp16, kc16, vp16, vc16, o16_s, l16_s), (q4, kp4, kc4, vp4, vc4, o4_s, l4_s)),
            (n_backs[2], n_backs[1])):
        dil = q.shape[1]
        nq = q.shape[2] // ATTN_BLOCK
        valid0, valid = masks(n_back)
        for r in range(dil):
            idx = (0, r)

            def put(jb, res, r=r, dil=dil, o_s=o_s, l_s=l_s):
                (o, lse), = res
                dst = pl.ds(jb * ATTN_BLOCK * dil + r, ATTN_BLOCK, stride=dil)
                o_s[dst, :] = o
                l_s[dst, :] = lse

            def first(q=q, kp=kp, kc=kc, vp=vp, vc=vc, idx=idx, put=put, valid0=valid0):
                put(0, _band_block([q[idx + (slice(0, ATTN_BLOCK),)]], halo(kp, kc, idx),
                                   halo(vp, vc, idx), valid0))

            def rest(jb, q=q, kc=kc, vc=vc, idx=idx, put=put, valid=valid):
                put(jb, _band_block([q[idx + (rows(jb),)]], window(kc, idx, jb),
                                    window(vc, idx, jb), valid))

            _for_each_block(nq, first, rest)

    valid0, valid = masks(n_backs[0])
    idx = (0,)

    def merge(jb, res):
        (o, lse), = res
        dst = rows(jb)
        l4 = l4_s[dst, :]
        l16 = l16_s[dst, :]
        mx = jnp.maximum(jnp.maximum(lse, l4), l16)
        w1 = jnp.exp(lse - mx)
        w4 = jnp.exp(l4 - mx)
        w16 = jnp.exp(l16 - mx)
        num = w1 * o + w4 * o4_s[dst, :] + w16 * o16_s[dst, :]
        o_ref[0, dst, :] = (num / (w1 + w4 + w16)).astype(o_ref.dtype)

    _for_each_block(
        tile // ATTN_BLOCK,
        lambda: merge(0, _band_block([q1[0, 0:ATTN_BLOCK]], halo(kp1, kc1, idx), halo(vp1, vc1, idx), valid0)),
        lambda jb: merge(jb, _band_block([q1[0, rows(jb)]], window(kc1, idx, jb), window(vc1, idx, jb), valid)))


def _dilated_attention(qkv1, qkv4, qkv16):
    bsz, seq, _ = qkv1.shape
    dils = tuple(p[1] for p in A_PATTERNS)
    assert dils[0] == 1 and qkv4.shape[1] == dils[1] and qkv16.shape[1] == dils[2]
    tile = dils[2] * ATTN_BLOCK
    nt = seq // tile

    def specs(dil):
        rows = tile // dil
        nb = rows // ATTN_BLOCK
        if dil == 1:
            cur = lambda off: pl.BlockSpec((1, rows, LANES), lambda b, hp, i: (b, i, off + hp))
            prev = lambda off: pl.BlockSpec(
                (1, ATTN_BLOCK, LANES), lambda b, hp, i: (b, jnp.maximum(i * nb - 1, 0), off + hp))
        else:
            cur = lambda off: pl.BlockSpec((1, dil, rows, LANES), lambda b, hp, i: (b, 0, i, off + hp))
            prev = lambda off: pl.BlockSpec(
                (1, dil, ATTN_BLOCK, LANES), lambda b, hp, i: (b, 0, jnp.maximum(i * nb - 1, 0), off + hp))
        return [cur(0), prev(A_BLOCKS), cur(A_BLOCKS), prev(2 * A_BLOCKS), cur(2 * A_BLOCKS)]

    kern = functools.partial(_dilated_kernel, n_backs=tuple(w // d for w, d in A_PATTERNS))
    return pl.pallas_call(
        kern,
        grid=(bsz, A_BLOCKS, nt),
        in_specs=specs(1) + specs(dils[1]) + specs(dils[2]),
        out_specs=pl.BlockSpec((1, tile, LANES), lambda b, hp, i: (b, i, hp)),
        out_shape=jax.ShapeDtypeStruct((bsz, seq, A_BLOCKS * LANES), BF16),
        scratch_shapes=[pltpu.VMEM((tile, LANES), F32)] * 4,
        compiler_params=_params("parallel", "parallel", "arbitrary"),
        name="dilated_attn",
    )(*([qkv1] * 5 + [qkv4] * 5 + [qkv16] * 5))


def _swa_kernel(sink_ref, q_ref, *rest, n_back, nq):
    kv = rest[:4 * B_KV_HEADS]
    o_ref = rest[4 * B_KV_HEADS]
    not_first = pl.program_id(1) > 0
    band, in_cur = _band_mask(n_back)
    valid0 = band & (in_cur | not_first)
    pairs = BQ_BLOCKS // B_KV_HEADS

    def run(q_rows, kv_of, valid):
        for g in range(B_KV_HEADS):
            cbs = [g * pairs + j for j in range(pairs)]
            qs = [q_ref[0, q_rows, cb * LANES:(cb + 1) * LANES] for cb in cbs]
            sinks = [sink_ref[2 * cb + p] for cb in cbs for p in range(2)]
            kk, vv = kv_of(g)
            res = _band_block(qs, kk, vv, valid, sinks, want_lse=False)
            for cb, (o, _) in zip(cbs, res):
                o_ref[0, q_rows, cb * LANES:(cb + 1) * LANES] = o.astype(o_ref.dtype)

    def halo_kv(g):
        kp, kc, vp, vc = kv[4 * g:4 * g + 4]
        return (jnp.concatenate([kp[0], kc[0, 0:ATTN_BLOCK]], axis=0),
                jnp.concatenate([vp[0], vc[0, 0:ATTN_BLOCK]], axis=0))

    def window_kv(jb):
        win = _rows(jb - 1, ATTN_BLOCK, 2)
        return lambda g: (kv[4 * g + 1][0, win], kv[4 * g + 3][0, win])

    _for_each_block(
        nq,
        lambda: run(_rows(0, ATTN_BLOCK), halo_kv, valid0),
        lambda jb: run(_rows(jb, ATTN_BLOCK), window_kv(jb), band))


def _swa_gqa(qkvb, sinks, tq=512):
    bsz, seq, _ = qkvb.shape
    tq = min(seq, tq)
    nq = tq // ATTN_BLOCK
    bq_w = BQ_BLOCKS * LANES
    in_specs = [pl.BlockSpec(memory_space=pltpu.SMEM),
                pl.BlockSpec((1, tq, bq_w), lambda b, i: (b, i, 0))]
    for g in range(B_KV_HEADS):
        for section in range(2):
            cb = BQ_BLOCKS + section * B_KV_HEADS + g
            in_specs += [pl.BlockSpec((1, ATTN_BLOCK, LANES),
                                      lambda b, i, cb=cb: (b, jnp.maximum(i * nq - 1, 0), cb)),
                         pl.BlockSpec((1, tq, LANES), lambda b, i, cb=cb: (b, i, cb))]
    args = [sinks.astype(F32)] + [qkvb] * (1 + 4 * B_KV_HEADS)
    kern = functools.partial(_swa_kernel, n_back=B_WINDOW - 1, nq=nq)
    return pl.pallas_call(
        kern,
        grid=(bsz, seq // tq),
        in_specs=in_specs,
        out_specs=pl.BlockSpec((1, tq, bq_w), lambda b, i: (b, i, 0)),
        out_shape=jax.ShapeDtypeStruct((bsz, seq, bq_w), BF16),
        compiler_params=_params("parallel", "arbitrary"),
        name="swa_gqa",
    )(*args)


def _outproj_ffn_kernel(x_ref, a_ref, b_ref, wa_ref, wb_ref, g_ref, wg_ref, wu_ref, wd_ref,
                        o_ref, h_ref):
    f = pl.program_id(1)

    @pl.when(f == 0)
    def _():
        x1 = x_ref[...] + _dot(a_ref[...], wa_ref[...]) + _dot(b_ref[...], wb_ref[...])
        o_ref[...] = x1
        h_ref[...] = _rms(x1, g_ref[...]).astype(BF16)

    h = h_ref[...]
    act = _silu(_dot(h, wg_ref[...])) * _dot(h, wu_ref[...])
    o_ref[...] += _dot(act.astype(BF16), wd_ref[...])


def _outproj_ffn(x, mix_a, mix_b, w_a, w_b, g, w_gate, w_up, w_down, tm=1024, tf=512):
    n, d = x.shape
    ff = w_gate.shape[1]
    ka, kb = mix_a.shape[1], mix_b.shape[1]
    return pl.pallas_call(
        _outproj_ffn_kernel,
        grid=(n // tm, ff // tf),
        in_specs=[pl.BlockSpec((tm, d), lambda i, f: (i, 0)),
                  pl.BlockSpec((tm, ka), lambda i, f: (i, 0)),
                  pl.BlockSpec((tm, kb), lambda i, f: (i, 0)),
                  pl.BlockSpec((ka, d), lambda i, f: (0, 0)),
                  pl.BlockSpec((kb, d), lambda i, f: (0, 0)),
                  pl.BlockSpec((1, d), lambda i, f: (0, 0)),
                  pl.BlockSpec((d, tf), lambda i, f: (0, f)),
                  pl.BlockSpec((d, tf), lambda i, f: (0, f)),
                  pl.BlockSpec((tf, d), lambda i, f: (f, 0))],
        out_specs=pl.BlockSpec((tm, d), lambda i, f: (i, 0)),
        out_shape=jax.ShapeDtypeStruct((n, d), F32),
        scratch_shapes=[pltpu.VMEM((tm, d), BF16)],
        compiler_params=_params("parallel", "arbitrary"),
        name="outproj_ffn",
    )(x, mix_a, mix_b, w_a, w_b, g.reshape(1, d), w_gate, w_up, w_down)


def _rec_inproj_kernel(x_ref, g_ref, w_ref, lb_ref, q_ref, k_ref, lf_ref, v_ref, gate_ref):
    h = _rms(x_ref[...], g_ref[...]).astype(BF16)
    acc = _dot(h, w_ref[...])
    kw = q_ref.shape[1]
    vw = v_ref.shape[1]
    lb = lb_ref[...]
    q_ref[...] = _silu(acc[:, :kw]).astype(BF16)
    fg = lb + (1.0 - lb) / (1.0 + jnp.exp(-acc[:, kw:2 * kw]))
    k_ref[...] = (1.0 - fg).astype(BF16)
    lf_ref[...] = jnp.log(fg)
    v_ref[...] = acc[:, 2 * kw:2 * kw + vw].astype(BF16)
    gate_ref[...] = _silu(acc[:, 2 * kw + vw:]).astype(BF16)


def _rec_inproj(x, g, w, lb, kw, vw, tm=512):
    n, d = x.shape
    width = w.shape[1]
    row = lambda i: (i, 0)
    fix = lambda i: (0, 0)
    return pl.pallas_call(
        _rec_inproj_kernel,
        grid=(n // tm,),
        in_specs=[pl.BlockSpec((tm, d), row),
                  pl.BlockSpec((1, d), fix),
                  pl.BlockSpec((d, width), fix),
                  pl.BlockSpec((1, kw), fix)],
        out_specs=[pl.BlockSpec((tm, kw), row), pl.BlockSpec((tm, kw), row),
                   pl.BlockSpec((tm, kw), row), pl.BlockSpec((tm, vw), row),
                   pl.BlockSpec((tm, vw), row)],
        out_shape=[jax.ShapeDtypeStruct((n, kw), BF16), jax.ShapeDtypeStruct((n, kw), BF16),
                   jax.ShapeDtypeStruct((n, kw), F32), jax.ShapeDtypeStruct((n, vw), BF16),
                   jax.ShapeDtypeStruct((n, vw), BF16)],
        compiler_params=_params("parallel"),
        name="rec_inproj",
    )(x, g.reshape(1, d), w, lb.reshape(1, kw))


def _hgrn_kernel(q_ref, k_ref, lf_ref, v_ref, gate_ref, ng_ref, o_ref, st_ref, b_ref, kf_ref, *,
                 n_chunks):
    @pl.when(pl.program_id(2) == 0)
    def _():
        st_ref[...] = jnp.zeros_like(st_ref)

    c_len, sub = REC_CHUNK, REC_SUB
    r_i = lax.broadcasted_iota(I32, (c_len, c_len), 0)
    c_i = lax.broadcasted_iota(I32, (c_len, c_len), 1)
    causal = c_i <= r_i
    tril = causal.astype(BF16)
    ng = ng_ref[...]

    drop = jnp.zeros((1, LANES), F32)
    for c in range(n_chunks):
        rows = slice(c * c_len, (c + 1) * c_len)
        lf = lf_ref[0, rows, :]
        lf1 = lf.astype(BF16)
        rem = lf - lf1.astype(F32)
        lf2 = rem.astype(BF16)
        lf3 = (rem - lf2.astype(F32)).astype(BF16)
        b = _dot(tril, lf1) + _dot(tril, lf2) + _dot(tril, lf3)
        b_ref[rows, :] = b
        for i in range(c_len // sub):
            end = b[(i + 1) * sub - 1:(i + 1) * sub, :]
            drop = jnp.minimum(drop, end - b[i * sub - 1:i * sub, :] if i > 0 else end)
    safe = jnp.min(drop) > REC_SAFE_DROP

    def finish(c, att, st):
        rows = _rows(c, c_len)
        b = b_ref[rows, :]
        q = q_ref[0, rows, :].astype(F32)
        k = k_ref[0, rows, :].astype(F32)
        v = v_ref[0, rows, :]
        b_last = b[c_len - 1:c_len, :]
        qe = (q * jnp.exp(b)).astype(BF16)
        o = _dot(att.astype(BF16), v) + _dot_nt(qe, st.astype(BF16))
        kd = (k * jnp.exp(b_last - b)).astype(BF16)
        st = st * jnp.exp(b_last) + _dot_tn(v, kd)
        y = o * lax.rsqrt(jnp.mean(o * o, axis=-1, keepdims=True) + EPS)
        o_ref[0, rows, :] = (y * ng * gate_ref[0, rows, :].astype(F32)).astype(o_ref.dtype)
        return st

    @pl.when(safe)
    def _():
        st = st_ref[...]
        for c in range(n_chunks):
            rows = slice(c * c_len, (c + 1) * c_len)
            b = b_ref[rows, :]
            q = q_ref[0, rows, :].astype(F32)
            k = k_ref[0, rows, :].astype(F32)
            att_rows = []
            for i in range(c_len // sub):
                lo, hi = i * sub, (i + 1) * sub
                ref = b[lo - 1:lo, :] if i > 0 else jnp.zeros((1, LANES), F32)
                qt = (q[lo:hi] * jnp.exp(b[lo:hi] - ref)).astype(BF16)
                kt = (k[:hi] * jnp.exp(ref - b[:hi])).astype(BF16)
                if hi < c_len:
                    kt = jnp.concatenate([kt, jnp.zeros((c_len - hi, LANES), BF16)], axis=0)
                att_rows.append(_dot_nt(qt, kt))
            att = jnp.where(causal, jnp.concatenate(att_rows, axis=0), 0.0)
            st = finish(c, att, st)
        st_ref[...] = st

    @pl.when(jnp.logical_not(safe))
    def _():
        def chunk(c, st):
            rows = _rows(c, c_len)
            b = b_ref[rows, :]
            q = q_ref[0, rows, :].astype(F32)
            kf_ref[...] = k_ref[0, rows, :].astype(F32)

            def key(s, att):
                b_s = b_ref[pl.ds(c * c_len + s, 1), :]
                dec = jnp.exp(jnp.minimum(b - b_s, 0.0))
                col = jnp.sum(q * kf_ref[pl.ds(s, 1), :] * dec, axis=-1, keepdims=True)
                return jnp.where(c_i == s, col, att)

            att = lax.fori_loop(0, c_len, key, jnp.zeros((c_len, c_len), F32))
            return finish(c, jnp.where(causal, att, 0.0), st)

        st_ref[...] = lax.fori_loop(0, n_chunks, chunk, st_ref[...])


def _hgrn(q, k, lf, v, gate, norm_g, bsz, seq, ts=512):
    kw = q.shape[-1] // C_HEADS
    vw = v.shape[-1] // C_HEADS
    assert kw == LANES and vw == LANES
    ts = min(ts, seq)
    v3 = lambda t: t.reshape(bsz, seq, t.shape[-1])
    blk = pl.BlockSpec((1, ts, LANES), lambda b, h, c: (b, c, h))
    kern = functools.partial(_hgrn_kernel, n_chunks=ts // REC_CHUNK)
    out = pl.pallas_call(
        kern,
        grid=(bsz, C_HEADS, seq // ts),
        in_specs=[blk, blk, blk, blk, blk, pl.BlockSpec((1, LANES), lambda b, h, c: (0, h))],
        out_specs=blk,
        out_shape=jax.ShapeDtypeStruct((bsz, seq, C_HEADS * vw), BF16),
        scratch_shapes=[pltpu.VMEM((vw, kw), F32), pltpu.VMEM((ts, LANES), F32),
                        pltpu.VMEM((REC_CHUNK, LANES), F32)],
        compiler_params=_params("parallel", "parallel", "arbitrary"),
        name="hgrn2",
    )(v3(q), v3(k), v3(lf), v3(v), v3(gate), norm_g.reshape(1, -1))
    return out.reshape(bsz * seq, C_HEADS * vw)


ROUTE_E, ROUTE_RANK, ROUTE_GATE = 0, 2, 4


def _router_kernel(x_ref, a_ref, wo_ref, g_ref, wr_hi_ref, wr_lo_ref, x3_ref, h_ref, route_ref,
                   cnt_ref, base_ref):
    @pl.when(pl.program_id(0) == 0)
    def _():
        base_ref[...] = jnp.zeros_like(base_ref)

    x3 = x_ref[...] + _dot(a_ref[...], wo_ref[...])
    x3_ref[...] = x3
    h = _rms(x3, g_ref[...])
    h_ref[...] = h
    h_hi = h.astype(BF16)
    h_lo = (h - h_hi.astype(F32)).astype(BF16)
    logits = _dot(h_hi, wr_hi_ref[...]) + _dot(h_hi, wr_lo_ref[...]) + _dot(h_lo, wr_hi_ref[...])
    tm = logits.shape[0]
    lane = lax.broadcasted_iota(I32, logits.shape, 1)
    logits = jnp.where(lane < N_EXPERTS, logits, -jnp.inf)
    lane_f = lane.astype(F32)
    v1 = jnp.max(logits, axis=-1, keepdims=True)
    e1 = jnp.min(jnp.where(logits == v1, lane_f, float(LANES)), axis=-1, keepdims=True)
    hot1 = lane_f == e1
    rest = jnp.where(hot1, -jnp.inf, logits)
    v2 = jnp.max(rest, axis=-1, keepdims=True)
    e2 = jnp.min(jnp.where(rest == v2, lane_f, float(LANES)), axis=-1, keepdims=True)
    hot2 = lane_f == e2
    t = jnp.exp(v2 - v1)
    g1 = 1.0 / (1.0 + t)
    g2 = t / (1.0 + t)
    e1 = e1.astype(I32)
    e2 = e2.astype(I32)
    member = (hot1 | hot2).astype(BF16)
    r_i = lax.broadcasted_iota(I32, (tm, tm), 0)
    c_i = lax.broadcasted_iota(I32, (tm, tm), 1)
    before = _dot((c_i < r_i).astype(BF16), member) + base_ref[...]
    rank1 = jnp.sum(jnp.where(hot1, before, 0.0), axis=-1, keepdims=True).astype(I32)
    rank2 = jnp.sum(jnp.where(hot2, before, 0.0), axis=-1, keepdims=True).astype(I32)
    base = base_ref[...] + jnp.sum(member.astype(F32), axis=0, keepdims=True)
    base_ref[...] = base
    cnt_ref[...] = base.astype(I32)
    route = jnp.where(lane == ROUTE_E, e1, 0)
    route = jnp.where(lane == ROUTE_E + 1, e2, route)
    route = jnp.where(lane == ROUTE_RANK, rank1, route)
    route = jnp.where(lane == ROUTE_RANK + 1, rank2, route)
    gate_bits = lax.bitcast_convert_type(jnp.where(lane == ROUTE_GATE, g1, g2), I32)
    route = jnp.where((lane == ROUTE_GATE) | (lane == ROUTE_GATE + 1), gate_bits, route)
    route_ref[...] = route


def _router(x, a, wo, g, w_router, tm=512):
    n, d = x.shape
    ka = a.shape[1]
    wr = jnp.zeros((d, LANES), F32).at[:, :N_EXPERTS].set(w_router)
    wr_hi = wr.astype(BF16)
    wr_lo = (wr - wr_hi.astype(F32)).astype(BF16)
    row = lambda i: (i, 0)
    fix = lambda i: (0, 0)
    return pl.pallas_call(
        _router_kernel,
        grid=(n // tm,),
        in_specs=[pl.BlockSpec((tm, d), row), pl.BlockSpec((tm, ka), row),
                  pl.BlockSpec((ka, d), fix), pl.BlockSpec((1, d), fix),
                  pl.BlockSpec((d, LANES), fix), pl.BlockSpec((d, LANES), fix)],
        out_specs=[pl.BlockSpec((tm, d), row), pl.BlockSpec((tm, d), row),
                   pl.BlockSpec((tm, LANES), row), pl.BlockSpec((1, LANES), fix)],
        out_shape=[jax.ShapeDtypeStruct((n, d), F32), jax.ShapeDtypeStruct((n, d), F32),
                   jax.ShapeDtypeStruct((n, LANES), I32), jax.ShapeDtypeStruct((1, LANES), I32)],
        scratch_shapes=[pltpu.VMEM((1, LANES), F32)],
        compiler_params=_params("arbitrary"),
        name="router",
    )(x, a, wo, g.reshape(1, d), wr_hi, wr_lo)


def _experts_kernel(be_ref, nb_ref, src0_ref, srcn_ref, slot_ref, h_hbm, wg_ref, wu_ref, wd_ref,
                    y_hbm, in_ref, out_ref, acc_ref, hb_ref, gsem, ssem, *, rps, nf, n_tok, second):
    i = pl.program_id(0)
    f = pl.program_id(1)
    nb = nb_ref[0]
    tm = hb_ref.shape[0]
    assert rps * nf == tm

    def gather_row(src_ref, r):
        return pltpu.make_async_copy(h_hbm.at[pl.ds(src_ref[r], 1)], in_ref.at[pl.ds(r, 1)], gsem)

    def scatter_row(r):
        return pltpu.make_async_copy(out_ref.at[pl.ds(r, 1)], y_hbm.at[pl.ds(slot_ref[r], 1)], ssem)

    def spare_fill(region, k):
        rows = pl.ds(region * second + n_tok + k * tm, tm)
        return pltpu.make_async_copy(out_ref, y_hbm.at[rows], ssem)

    def wait_gather():
        pltpu.make_async_copy(h_hbm.at[pl.ds(0, tm)], in_ref, gsem).wait()

    def wait_scatter():
        pltpu.make_async_copy(out_ref, y_hbm.at[pl.ds(0, tm)], ssem).wait()

    @pl.when((i == 0) & (f == 0))
    def _():
        out_ref[...] = jnp.zeros_like(out_ref)
        fills = [spare_fill(region, k) for region in range(2) for k in range((second - n_tok) // tm)]
        for cp in fills:
            cp.start()
        for cp in fills:
            cp.wait()

        def body(it, carry):
            for u in range(8):
                gather_row(src0_ref, it * 8 + u).start()
            return carry

        lax.fori_loop(0, tm // 8, body, 0)

    def start_rows(with_gather):
        for u in range(rps):
            r = f * rps + u
            if with_gather:
                gather_row(srcn_ref, r).start()
            scatter_row(r).start()

    @pl.when(i < nb)
    def _():
        @pl.when(f == 0)
        def _():
            wait_gather()
            hb_ref[...] = in_ref[...].astype(BF16)
            acc_ref[...] = jnp.zeros_like(acc_ref)

        start_rows(True)
        h = hb_ref[...]
        act = _silu(_dot(h, wg_ref[...])) * _dot(h, wu_ref[...])
        acc_ref[...] += _dot(act.astype(BF16), wd_ref[...])

        @pl.when(f == nf - 1)
        def _():
            wait_scatter()
            out_ref[...] = acc_ref[...]

    @pl.when(i == nb)
    def _():
        @pl.when(f == 0)
        def _():
            wait_gather()

        start_rows(False)

        @pl.when(f == nf - 1)
        def _():
            wait_scatter()


def _experts(h, src, slot, blk_e, nb_used, w_gate, w_up, w_down, tm, second, tf=896):
    n, d = h.shape
    ff = w_gate.shape[2]
    nf = ff // tf
    n_blocks = src.shape[0] // tm
    rps = tm // nf

    def clamp(i, f, be, nb):
        live = i < nb[0]
        return be[jnp.minimum(i, nb[0] - 1)], jnp.where(live, f, nf - 1)

    def wg_map(i, f, be, nb):
        e, fe = clamp(i, f, be, nb)
        return (e, 0, fe)

    def wd_map(i, f, be, nb):
        e, fe = clamp(i, f, be, nb)
        return (e, fe, 0)

    smem = lambda index_map: pl.BlockSpec((tm,), index_map, memory_space=pltpu.SMEM)
    grid_spec = pltpu.PrefetchScalarGridSpec(
        num_scalar_prefetch=2,
        grid=(n_blocks + 1, nf),
        in_specs=[smem(lambda i, f, be, nb: (0,)),
                  smem(lambda i, f, be, nb: (jnp.minimum(i + 1, nb[0] - 1),)),
                  smem(lambda i, f, be, nb: (jnp.clip(i - 1, 0, nb[0] - 1),)),
                  pl.BlockSpec(memory_space=pl.ANY),
                  pl.BlockSpec((None, d, tf), wg_map),
                  pl.BlockSpec((None, d, tf), wg_map),
                  pl.BlockSpec((None, tf, d), wd_map)],
        out_specs=pl.BlockSpec(memory_space=pl.ANY),
        scratch_shapes=[pltpu.VMEM((tm, d), F32), pltpu.VMEM((tm, d), F32), pltpu.VMEM((tm, d), F32),
                        pltpu.VMEM((tm, d), BF16), pltpu.SemaphoreType.DMA(()),
                        pltpu.SemaphoreType.DMA(())],
    )
    kern = functools.partial(_experts_kernel, rps=rps, nf=nf, n_tok=n, second=second)
    return pl.pallas_call(
        kern,
        grid_spec=grid_spec,
        out_shape=jax.ShapeDtypeStruct((2 * second, d), F32),
        compiler_params=_params("arbitrary", "arbitrary"),
        name="moe_experts",
    )(blk_e, nb_used, src, src, slot, h, w_gate, w_up, w_down)


def _combine_kernel(x_ref, route_ref, g_ref, ya_ref, yb_ref, o_ref):
    route = route_ref[...]
    lane = lax.broadcasted_iota(I32, route.shape, 1)
    gates = lax.bitcast_convert_type(route, F32)
    g1 = jnp.sum(jnp.where(lane == ROUTE_GATE, gates, 0.0), axis=-1, keepdims=True)
    g2 = jnp.sum(jnp.where(lane == ROUTE_GATE + 1, gates, 0.0), axis=-1, keepdims=True)
    x4 = x_ref[...] + (ya_ref[...] * g1 + yb_ref[...] * g2)
    o_ref[...] = _rms(x4, g_ref[...])


def _combine(x, route, y, g, second, tm=512):
    n, d = x.shape
    off = second // tm
    return pl.pallas_call(
        _combine_kernel,
        grid=(n // tm,),
        in_specs=[pl.BlockSpec((tm, d), lambda i: (i, 0)),
                  pl.BlockSpec((tm, LANES), lambda i: (i, 0)),
                  pl.BlockSpec((1, d), lambda i: (0, 0)),
                  pl.BlockSpec((tm, d), lambda i: (i, 0)),
                  pl.BlockSpec((tm, d), lambda i: (i + off, 0))],
        out_specs=pl.BlockSpec((tm, d), lambda i: (i, 0)),
        out_shape=jax.ShapeDtypeStruct((n, d), F32),
        compiler_params=_params("parallel"),
        name="moe_combine",
    )(x, route, g.reshape(1, d), y, y)


def _moe_layout(route, counts, n, tm):
    e = route[:, ROUTE_E:ROUTE_E + 2]
    rank = route[:, ROUTE_RANK:ROUTE_RANK + 2]
    sizes = counts[0, :N_EXPERTS]
    padded = ((sizes + tm - 1) // tm) * tm
    pend = jnp.cumsum(padded)
    pstart = pend - padded
    start_of = jnp.zeros_like(e)
    for j in range(N_EXPERTS):
        start_of = jnp.where(e == j, pstart[j], start_of)
    dest = (start_of + rank).astype(I32)
    n_blocks = (2 * n) // tm + N_EXPERTS
    rows = n_blocks * tm
    blk_e = jnp.minimum(jnp.searchsorted(pend, jnp.arange(n_blocks, dtype=I32) * tm, side='right'),
                        N_EXPERTS - 1).astype(I32)
    nb_used = (pend[-1:] // tm).astype(I32)
    second = n + N_EXPERTS * tm
    pair_slot = jnp.arange(n, dtype=I32)[:, None] + jnp.array([0, second], I32)[None, :]
    row = jnp.arange(rows, dtype=I32)
    row_e = jnp.repeat(blk_e, tm)
    pad_j = jnp.clip(row - (pstart + sizes)[row_e], 0, tm - 1)
    spare = n + row_e * tm + pad_j
    slot = spare.at[dest.reshape(-1)].set(pair_slot.reshape(-1))
    token = slot % second
    src = jnp.where(token < n, token, 0)
    return src, slot, blk_e, nb_used, second


def kernel(x, positions, g_mix, g_ffn, g_final, w_in_attn, w_out_attn, attn_sinks, w_in_rec,
           rec_lower_bounds, rec_norm_g, w_out_rec, w_gate_dense, w_up_dense, w_down_dense,
           w_router, w_gate_moe, w_up_moe, w_down_moe):
    bsz, seq, d = x.shape
    n = bsz * seq
    x0 = x.reshape(n, d)
    bf = lambda w: w.astype(BF16)

    cos_t, sin_t = _rope_tables(positions)
    qkv1, qkv4, qkv16, qkvb = _attn_inproj(x, g_mix[0], bf(w_in_attn[0]), cos_t, sin_t)
    mix_a = _dilated_attention(qkv1, qkv4, qkv16)
    mix_b = _swa_gqa(qkvb, attn_sinks[0])
    a_w = A_HEADS * HEAD_DIM
    w_out = bf(w_out_attn[0])
    x2 = _outproj_ffn(x0, mix_a.reshape(n, -1), mix_b.reshape(n, -1), w_out[:a_w], w_out[a_w:],
                      g_ffn[0], bf(w_gate_dense[0]), bf(w_up_dense[0]), bf(w_down_dense[0]))

    lb = jax.nn.softmax(rec_lower_bounds.astype(F32), axis=0)
    lb1 = (jnp.cumsum(lb, axis=0) - lb[0])[1]
    kw = lb1.shape[0]
    vw = rec_norm_g.shape[1]
    q, k, lf, v, gate = _rec_inproj(x2, g_mix[1], bf(w_in_rec[0]), lb1, kw, vw)
    rec = _hgrn(q, k, lf, v, gate, rec_norm_g[0], bsz, seq)
    x3, h, route, counts = _router(x2, rec, bf(w_out_rec[0]), g_ffn[1], w_router[0])
    tm_moe = min(1024, n)
    src, slot, blk_e, nb_used, second = _moe_layout(route, counts, n, tm_moe)
    ys = _experts(h, src, slot, blk_e, nb_used, bf(w_gate_moe[0]), bf(w_up_moe[0]),
                  bf(w_down_moe[0]), tm_moe, second)
    out = _combine(x3, route, ys, g_final, second)
    return out.reshape(bsz, seq, d)
```

```python
import functools

import jax
import jax.numpy as jnp
from jax import lax
from jax.experimental import pallas as pl
from jax.experimental.pallas import tpu as pltpu

F32 = jnp.float32
BF16 = jnp.bfloat16
I32 = jnp.int32

EPS = 1e-6
HEAD_DIM = 64
ROT_DIM = HEAD_DIM // 4
ROPE_THETA = 500000.0
LANES = 128
ATTN_BLOCK = 128
A_HEADS = 8
A_PATTERNS = ((128, 1), (512, 4), (2048, 16))
B_Q_HEADS = 8
B_KV_HEADS = 2
B_WINDOW = 128
C_HEADS = 8
N_EXPERTS = 8
REC_CHUNK = 128
REC_SUB = 16
REC_SAFE_DROP = -80.0
VMEM_LIMIT = 56 * 1024 * 1024


def _params(*sem):
    return pltpu.CompilerParams(dimension_semantics=sem, vmem_limit_bytes=VMEM_LIMIT)


def _rms(x, g):
    return x * lax.rsqrt(jnp.mean(x * x, axis=-1, keepdims=True) + EPS) * g


def _silu(x):
    return x / (1.0 + jnp.exp(-x))


def _dot(a, b):
    return jnp.dot(a, b, preferred_element_type=F32)


def _dot_nt(a, b):
    return lax.dot_general(a, b, (((1,), (1,)), ((), ())), preferred_element_type=F32)


def _dot_tn(a, b):
    return lax.dot_general(a, b, (((0,), (0,)), ((), ())), preferred_element_type=F32)


def _rows(j, size, count=1):
    if isinstance(j, int):
        return slice(j * size, (j + count) * size)
    return pl.ds(pl.multiple_of(j * size, size), count * size)


def _to_row_tiles(ref, x):
    pieces = x.shape[1] // LANES
    for j in range(pieces):
        ref[pl.ds(j, x.shape[0], stride=pieces), :] = x[:, j * LANES:(j + 1) * LANES]


def _row_tile_piece(ref, j, m, pieces):
    return ref[pl.ds(j, m, stride=pieces), :]


def _rope_table_kernel(pos_ref, invf_ref, cos_ref, sin_ref):
    ang = pos_ref[...].astype(F32) * invf_ref[...]
    cos_ref[...] = jnp.cos(ang)
    sin_ref[...] = jnp.sin(ang)


def _rope_tables(positions):
    n = positions.size
    half = ROT_DIM // 2
    per_row = LANES // half
    inv_freq = jnp.power(ROPE_THETA, -jnp.arange(0, ROT_DIM, 2, dtype=F32) / ROT_DIM)
    pos_rep = jnp.repeat(positions.reshape(n // per_row, per_row), half, axis=1)
    invf_row = jnp.tile(inv_freq, per_row).reshape(1, LANES)
    rows = n // per_row
    tr = min(rows, 512)
    cos8, sin8 = pl.pallas_call(
        _rope_table_kernel,
        grid=(rows // tr,),
        in_specs=[pl.BlockSpec((tr, LANES), lambda i: (i, 0)),
                  pl.BlockSpec((1, LANES), lambda i: (0, 0))],
        out_specs=[pl.BlockSpec((tr, LANES), lambda i: (i, 0))] * 2,
        out_shape=[jax.ShapeDtypeStruct((rows, LANES), F32)] * 2,
        compiler_params=_params("parallel"),
        name="rope_tables",
    )(pos_rep, invf_row)
    cos8 = cos8.reshape(n, half)
    sin8 = sin8.reshape(n, half)
    rest = HEAD_DIM - ROT_DIM
    c64 = jnp.concatenate([cos8, cos8, jnp.ones((n, rest), F32)], axis=1)
    s64 = jnp.concatenate([-sin8, sin8, jnp.zeros((n, rest), F32)], axis=1)
    return jnp.tile(c64, (1, 2)), jnp.tile(s64, (1, 2))


A_BLOCKS = A_HEADS * HEAD_DIM // LANES
BQ_BLOCKS = B_Q_HEADS * HEAD_DIM // LANES
A_QKV = 3 * A_BLOCKS * LANES
B_QKV = (BQ_BLOCKS + 2 * B_KV_HEADS) * LANES


def _attn_inproj_kernel(x_ref, g_ref, w_ref, c_ref, s_ref, o1_ref, o4_ref, o16_ref, ob_ref, sc_ref):
    h = _rms(x_ref[0], g_ref[...]).astype(BF16)
    acc = _dot(h, w_ref[...])
    tm = acc.shape[0]
    c = c_ref[0]
    s = s_ref[0]
    lane = lax.broadcasted_iota(I32, c.shape, 1)
    first = (lane % HEAD_DIM) < (ROT_DIM // 2)
    lo_half = lane < HEAD_DIM

    def rope(blk):
        up = pltpu.roll(blk, LANES - ROT_DIM // 2, 1)
        dn = pltpu.roll(blk, ROT_DIM // 2, 1)
        return blk * c + jnp.where(first, up, dn) * s

    def col(cb):
        return acc[:, cb * LANES:(cb + 1) * LANES]

    scale = HEAD_DIM ** -0.5
    for cb in range(3 * A_BLOCKS):
        blk = col(cb)
        if cb < A_BLOCKS:
            blk = rope(blk) * scale
        elif cb < 2 * A_BLOCKS:
            blk = rope(blk)
        sc_ref[cb] = blk
        o1_ref[0, :, cb * LANES:(cb + 1) * LANES] = blk.astype(BF16)
    for o_ref in (o4_ref, o16_ref):
        dil = o_ref.shape[1]
        for r in range(dil):
            for cb in range(3 * A_BLOCKS):
                o_ref[0, r, :, cb * LANES:(cb + 1) * LANES] = (
                    sc_ref[cb, pl.ds(r, tm // dil, stride=dil), :].astype(BF16))
    base = 3 * A_BLOCKS
    for j in range(BQ_BLOCKS):
        ob_ref[0, :, j * LANES:(j + 1) * LANES] = (rope(col(base + j)) * scale).astype(BF16)
    for j, blk in enumerate((rope(col(base + BQ_BLOCKS)), col(base + BQ_BLOCKS + 1))):
        swapped = pltpu.roll(blk, HEAD_DIM, 1)
        for g, dup in enumerate((jnp.where(lo_half, blk, swapped), jnp.where(lo_half, swapped, blk))):
            cb = BQ_BLOCKS + j * B_KV_HEADS + g
            ob_ref[0, :, cb * LANES:(cb + 1) * LANES] = dup.astype(BF16)


def _attn_inproj(x, g, w, cos_t, sin_t, tm=512):
    bsz, seq, d = x.shape
    width = w.shape[1]
    assert B_KV_HEADS * HEAD_DIM == LANES and width == A_QKV + (BQ_BLOCKS + 2) * LANES
    tile = lambda b, i: (b, i, 0)
    fix = lambda b, i: (0, 0)
    d4, d16 = A_PATTERNS[1][1], A_PATTERNS[2][1]
    perm = lambda dil: pl.BlockSpec((1, dil, tm // dil, A_QKV), lambda b, i: (b, 0, i, 0))
    return pl.pallas_call(
        _attn_inproj_kernel,
        grid=(bsz, seq // tm),
        in_specs=[pl.BlockSpec((1, tm, d), tile),
                  pl.BlockSpec((1, d), fix),
                  pl.BlockSpec((d, width), fix),
                  pl.BlockSpec((1, tm, LANES), tile),
                  pl.BlockSpec((1, tm, LANES), tile)],
        out_specs=[pl.BlockSpec((1, tm, A_QKV), tile), perm(d4), perm(d16),
                   pl.BlockSpec((1, tm, B_QKV), tile)],
        out_shape=[jax.ShapeDtypeStruct((bsz, seq, A_QKV), BF16),
                   jax.ShapeDtypeStruct((bsz, d4, seq // d4, A_QKV), BF16),
                   jax.ShapeDtypeStruct((bsz, d16, seq // d16, A_QKV), BF16),
                   jax.ShapeDtypeStruct((bsz, seq, B_QKV), BF16)],
        scratch_shapes=[pltpu.VMEM((3 * A_BLOCKS, tm, LANES), F32)],
        compiler_params=_params("parallel", "parallel"),
        name="attn_inproj",
    )(x, g.reshape(1, d), w, cos_t.reshape(bsz, seq, LANES), sin_t.reshape(bsz, seq, LANES))


def _band_mask(n_back):
    qi = lax.broadcasted_iota(I32, (ATTN_BLOCK, 2 * ATTN_BLOCK), 0)
    kj = lax.broadcasted_iota(I32, (ATTN_BLOCK, 2 * ATTN_BLOCK), 1)
    dist = ATTN_BLOCK + qi - kj
    return (dist >= 0) & (dist <= n_back), kj >= ATTN_BLOCK


def _band_block(q_pairs, kk, vv, valid, sinks=None, want_lse=True):
    lane = lax.broadcasted_iota(I32, (ATTN_BLOCK, LANES), 1)
    lo_half = lane < HEAD_DIM
    zero = jnp.zeros((ATTN_BLOCK, LANES), BF16)
    lhs = []
    for q in q_pairs:
        lhs += [jnp.where(lo_half, q, zero), jnp.where(lo_half, zero, q)]
    s_all = _dot_nt(jnp.concatenate(lhs, axis=0), kk)
    ps, inv_l, lses = [], [], []
    for u in range(len(lhs)):
        s = jnp.where(valid, s_all[u * ATTN_BLOCK:(u + 1) * ATTN_BLOCK], -jnp.inf)
        m = jnp.max(s, axis=-1, keepdims=True)
        if sinks is not None:
            m = jnp.maximum(m, sinks[u])
        e = jnp.exp(s - m)
        l = jnp.sum(e, axis=-1, keepdims=True)
        if sinks is not None:
            l = l + jnp.exp(sinks[u] - m)
        ps.append(e.astype(BF16))
        inv_l.append(1.0 / l)
        lses.append(m + jnp.log(l) if want_lse else None)
    o_all = _dot(jnp.concatenate(ps, axis=0), vv)
    outs = []
    for j in range(len(q_pairs)):
        o0 = o_all[(2 * j) * ATTN_BLOCK:(2 * j + 1) * ATTN_BLOCK] * inv_l[2 * j]
        o1 = o_all[(2 * j + 1) * ATTN_BLOCK:(2 * j + 2) * ATTN_BLOCK] * inv_l[2 * j + 1]
        lse = jnp.where(lo_half, lses[2 * j], lses[2 * j + 1]) if want_lse else None
        outs.append((jnp.where(lo_half, o0, o1), lse))
    return outs


def _for_each_block(nq, first_fn, rest_fn):
    first_fn()
    if nq > 1:
        def body(jb, carry):
            rest_fn(jb)
            return carry
        lax.fori_loop(1, nq, body, 0)


def _dilated_kernel(q1, kp1, kc1, vp1, vc1, q4, kp4, kc4, vp4, vc4, q16, kp16, kc16, vp16, vc16,
                    o_ref, o4_s, l4_s, o16_s, l16_s, *, n_backs):
    tile = o_ref.shape[1]
    not_first = pl.program_id(2) > 0

    def masks(n_back):
        band, in_cur = _band_mask(n_back)
        return band & (in_cur | not_first), band

    def halo(p_ref, c_ref, idx):
        return jnp.concatenate([p_ref[idx], c_ref[idx + (slice(0, ATTN_BLOCK),)]], axis=0)

    def window(c_ref, idx, jb):
        return c_ref[idx + (_rows(jb - 1, ATTN_BLOCK, 2),)]

    def rows(jb):
        return _rows(jb, ATTN_BLOCK)

    for (q, kp, kc, vp, vc, o_s, l_s), n_back in zip(
            ((q16, kp16, kc16, vp16, vc16, o16_s, l16_s), (q4, kp4, kc4, vp4, vc4, o4_s, l4_s)),
            (n_backs[2], n_backs[1])):
        dil = q.shape[1]
        nq = q.shape[2] // ATTN_BLOCK
        valid0, valid = masks(n_back)
        for r in range(dil):
            idx = (0, r)

            def put(jb, res, r=r, dil=dil, o_s=o_s, l_s=l_s):
                (o, lse), = res
                dst = pl.ds(jb * ATTN_BLOCK * dil + r, ATTN_BLOCK, stride=dil)
                o_s[dst, :] = o
                l_s[dst, :] = lse

            def first(q=q, kp=kp, kc=kc, vp=vp, vc=vc, idx=idx, put=put, valid0=valid0):
                put(0, _band_block([q[idx + (slice(0, ATTN_BLOCK),)]], halo(kp, kc, idx),
                                   halo(vp, vc, idx), valid0))

            def rest(jb, q=q, kc=kc, vc=vc, idx=idx, put=put, valid=valid):
                put(jb, _band_block([q[idx + (rows(jb),)]], window(kc, idx, jb),
                                    window(vc, idx, jb), valid))

            _for_each_block(nq, first, rest)

    valid0, valid = masks(n_backs[0])
    idx = (0,)

    def merge(jb, res):
        (o, lse), = res
        dst = rows(jb)
        l4 = l4_s[dst, :]
        l16 = l16_s[dst, :]
        mx = jnp.maximum(jnp.maximum(lse, l4), l16)
        w1 = jnp.exp(lse - mx)
        w4 = jnp.exp(l4 - mx)
        w16 = jnp.exp(l16 - mx)
        num = w1 * o + w4 * o4_s[dst, :] + w16 * o16_s[dst, :]
        o_ref[0, dst, :] = (num / (w1 + w4 + w16)).astype(o_ref.dtype)

    _for_each_block(
        tile // ATTN_BLOCK,
        lambda: merge(0, _band_block([q1[0, 0:ATTN_BLOCK]], halo(kp1, kc1, idx), halo(vp1, vc1, idx), valid0)),
        lambda jb: merge(jb, _band_block([q1[0, rows(jb)]], window(kc1, idx, jb), window(vc1, idx, jb), valid)))


def _dilated_attention(qkv1, qkv4, qkv16):
    bsz, seq, _ = qkv1.shape
    dils = tuple(p[1] for p in A_PATTERNS)
    assert dils[0] == 1 and qkv4.shape[1] == dils[1] and qkv16.shape[1] == dils[2]
    tile = dils[2] * ATTN_BLOCK
    nt = seq // tile

    def specs(dil):
        rows = tile // dil
        nb = rows // ATTN_BLOCK
        if dil == 1:
            cur = lambda off: pl.BlockSpec((1, rows, LANES), lambda b, hp, i: (b, i, off + hp))
            prev = lambda off: pl.BlockSpec(
                (1, ATTN_BLOCK, LANES), lambda b, hp, i: (b, jnp.maximum(i * nb - 1, 0), off + hp))
        else:
            cur = lambda off: pl.BlockSpec((1, dil, rows, LANES), lambda b, hp, i: (b, 0, i, off + hp))
            prev = lambda off: pl.BlockSpec(
                (1, dil, ATTN_BLOCK, LANES), lambda b, hp, i: (b, 0, jnp.maximum(i * nb - 1, 0), off + hp))
        return [cur(0), prev(A_BLOCKS), cur(A_BLOCKS), prev(2 * A_BLOCKS), cur(2 * A_BLOCKS)]

    kern = functools.partial(_dilated_kernel, n_backs=tuple(w // d for w, d in A_PATTERNS))
    return pl.pallas_call(
        kern,
        grid=(bsz, A_BLOCKS, nt),
        in_specs=specs(1) + specs(dils[1]) + specs(dils[2]),
        out_specs=pl.BlockSpec((1, tile, LANES), lambda b, hp, i: (b, i, hp)),
        out_shape=jax.ShapeDtypeStruct((bsz, seq, A_BLOCKS * LANES), BF16),
        scratch_shapes=[pltpu.VMEM((tile, LANES), F32)] * 4,
        compiler_params=_params("parallel", "parallel", "arbitrary"),
        name="dilated_attn",
    )(*([qkv1] * 5 + [qkv4] * 5 + [qkv16] * 5))


def _swa_kernel(sink_ref, q_ref, *rest, n_back, nq):
    kv = rest[:4 * B_KV_HEADS]
    o_ref = rest[4 * B_KV_HEADS]
    not_first = pl.program_id(1) > 0
    band, in_cur = _band_mask(n_back)
    valid0 = band & (in_cur | not_first)
    pairs = BQ_BLOCKS // B_KV_HEADS

    def run(q_rows, kv_of, valid):
        for g in range(B_KV_HEADS):
            cbs = [g * pairs + j for j in range(pairs)]
            qs = [q_ref[0, q_rows, cb * LANES:(cb + 1) * LANES] for cb in cbs]
            sinks = [sink_ref[2 * cb + p] for cb in cbs for p in range(2)]
            kk, vv = kv_of(g)
            res = _band_block(qs, kk, vv, valid, sinks, want_lse=False)
            for cb, (o, _) in zip(cbs, res):
                o_ref[0, q_rows, cb * LANES:(cb + 1) * LANES] = o.astype(o_ref.dtype)

    def halo_kv(g):
        kp, kc, vp, vc = kv[4 * g:4 * g + 4]
        return (jnp.concatenate([kp[0], kc[0, 0:ATTN_BLOCK]], axis=0),
                jnp.concatenate([vp[0], vc[0, 0:ATTN_BLOCK]], axis=0))

    def window_kv(jb):
        win = _rows(jb - 1, ATTN_BLOCK, 2)
        return lambda g: (kv[4 * g + 1][0, win], kv[4 * g + 3][0, win])

    _for_each_block(
        nq,
        lambda: run(_rows(0, ATTN_BLOCK), halo_kv, valid0),
        lambda jb: run(_rows(jb, ATTN_BLOCK), window_kv(jb), band))


def _swa_gqa(qkvb, sinks, tq=512):
    bsz, seq, _ = qkvb.shape
    tq = min(seq, tq)
    nq = tq // ATTN_BLOCK
    bq_w = BQ_BLOCKS * LANES
    in_specs = [pl.BlockSpec(memory_space=pltpu.SMEM),
                pl.BlockSpec((1, tq, bq_w), lambda b, i: (b, i, 0))]
    for g in range(B_KV_HEADS):
        for section in range(2):
            cb = BQ_BLOCKS + section * B_KV_HEADS + g
            in_specs += [pl.BlockSpec((1, ATTN_BLOCK, LANES),
                                      lambda b, i, cb=cb: (b, jnp.maximum(i * nq - 1, 0), cb)),
                         pl.BlockSpec((1, tq, LANES), lambda b, i, cb=cb: (b, i, cb))]
    args = [sinks.astype(F32)] + [qkvb] * (1 + 4 * B_KV_HEADS)
    kern = functools.partial(_swa_kernel, n_back=B_WINDOW - 1, nq=nq)
    return pl.pallas_call(
        kern,
        grid=(bsz, seq // tq),
        in_specs=in_specs,
        out_specs=pl.BlockSpec((1, tq, bq_w), lambda b, i: (b, i, 0)),
        out_shape=jax.ShapeDtypeStruct((bsz, seq, bq_w), BF16),
        compiler_params=_params("parallel", "arbitrary"),
        name="swa_gqa",
    )(*args)


def _outproj_ffn_kernel(x_ref, a_ref, b_ref, wa_ref, wb_ref, g_ref, wg_ref, wu_ref, wd_ref,
                        o_ref, h_ref):
    f = pl.program_id(1)

    @pl.when(f == 0)
    def _():
        x1 = x_ref[...] + _dot(a_ref[...], wa_ref[...]) + _dot(b_ref[...], wb_ref[...])
        o_ref[...] = x1
        h_ref[...] = _rms(x1, g_ref[...]).astype(BF16)

    h = h_ref[...]
    act = _silu(_dot(h, wg_ref[...])) * _dot(h, wu_ref[...])
    o_ref[...] += _dot(act.astype(BF16), wd_ref[...])


def _outproj_ffn(x, mix_a, mix_b, w_a, w_b, g, w_gate, w_up, w_down, tm=1024, tf=512):
    n, d = x.shape
    ff = w_gate.shape[1]
    ka, kb = mix_a.shape[1], mix_b.shape[1]
    return pl.pallas_call(
        _outproj_ffn_kernel,
        grid=(n // tm, ff // tf),
        in_specs=[pl.BlockSpec((tm, d), lambda i, f: (i, 0)),
                  pl.BlockSpec((tm, ka), lambda i, f: (i, 0)),
                  pl.BlockSpec((tm, kb), lambda i, f: (i, 0)),
                  pl.BlockSpec((ka, d), lambda i, f: (0, 0)),
                  pl.BlockSpec((kb, d), lambda i, f: (0, 0)),
                  pl.BlockSpec((1, d), lambda i, f: (0, 0)),
                  pl.BlockSpec((d, tf), lambda i, f: (0, f)),
                  pl.BlockSpec((d, tf), lambda i, f: (0, f)),
                  pl.BlockSpec((tf, d), lambda i, f: (f, 0))],
        out_specs=pl.BlockSpec((tm, d), lambda i, f: (i, 0)),
        out_shape=jax.ShapeDtypeStruct((n, d), F32),
        scratch_shapes=[pltpu.VMEM((tm, d), BF16)],
        compiler_params=_params("parallel", "arbitrary"),
        name="outproj_ffn",
    )(x, mix_a, mix_b, w_a, w_b, g.reshape(1, d), w_gate, w_up, w_down)


def _rec_inproj_kernel(x_ref, g_ref, w_ref, lb_ref, q_ref, k_ref, lf_ref, v_ref, gate_ref):
    h = _rms(x_ref[...], g_ref[...]).astype(BF16)
    acc = _dot(h, w_ref[...])
    kw = q_ref.shape[1]
    vw = v_ref.shape[1]
    lb = lb_ref[...]
    q_ref[...] = _silu(acc[:, :kw]).astype(BF16)
    fg = lb + (1.0 - lb) / (1.0 + jnp.exp(-acc[:, kw:2 * kw]))
    k_ref[...] = (1.0 - fg).astype(BF16)
    lf_ref[...] = jnp.log(fg)
    v_ref[...] = acc[:, 2 * kw:2 * kw + vw].astype(BF16)
    gate_ref[...] = _silu(acc[:, 2 * kw + vw:]).astype(BF16)


def _rec_inproj(x, g, w, lb, kw, vw, tm=512):
    n, d = x.shape
    width = w.shape[1]
    row = lambda i: (i, 0)
    fix = lambda i: (0, 0)
    return pl.pallas_call(
        _rec_inproj_kernel,
        grid=(n // tm,),
        in_specs=[pl.BlockSpec((tm, d), row),
                  pl.BlockSpec((1, d), fix),
                  pl.BlockSpec((d, width), fix),
                  pl.BlockSpec((1, kw), fix)],
        out_specs=[pl.BlockSpec((tm, kw), row), pl.BlockSpec((tm, kw), row),
                   pl.BlockSpec((tm, kw), row), pl.BlockSpec((tm, vw), row),
                   pl.BlockSpec((tm, vw), row)],
        out_shape=[jax.ShapeDtypeStruct((n, kw), BF16), jax.ShapeDtypeStruct((n, kw), BF16),
                   jax.ShapeDtypeStruct((n, kw), F32), jax.ShapeDtypeStruct((n, vw), BF16),
                   jax.ShapeDtypeStruct((n, vw), BF16)],
        compiler_params=_params("parallel"),
        name="rec_inproj",
    )(x, g.reshape(1, d), w, lb.reshape(1, kw))


def _hgrn_kernel(q_ref, k_ref, lf_ref, v_ref, gate_ref, ng_ref, o_ref, st_ref, b_ref, kf_ref, *,
                 n_chunks):
    @pl.when(pl.program_id(2) == 0)
    def _():
        st_ref[...] = jnp.zeros_like(st_ref)

    c_len, sub = REC_CHUNK, REC_SUB
    r_i = lax.broadcasted_iota(I32, (c_len, c_len), 0)
    c_i = lax.broadcasted_iota(I32, (c_len, c_len), 1)
    causal = c_i <= r_i
    tril = causal.astype(BF16)
    ng = ng_ref[...]

    drop = jnp.zeros((1, LANES), F32)
    for c in range(n_chunks):
        rows = slice(c * c_len, (c + 1) * c_len)
        lf = lf_ref[0, rows, :]
        lf1 = lf.astype(BF16)
        rem = lf - lf1.astype(F32)
        lf2 = rem.astype(BF16)
        lf3 = (rem - lf2.astype(F32)).astype(BF16)
        b = _dot(tril, lf1) + _dot(tril, lf2) + _dot(tril, lf3)
        b_ref[rows, :] = b
        for i in range(c_len // sub):
            end = b[(i + 1) * sub - 1:(i + 1) * sub, :]
            drop = jnp.minimum(drop, end - b[i * sub - 1:i * sub, :] if i > 0 else end)
    safe = jnp.min(drop) > REC_SAFE_DROP

    def finish(c, att, st):
        rows = _rows(c, c_len)
        b = b_ref[rows, :]
        q = q_ref[0, rows, :].astype(F32)
        k = k_ref[0, rows, :].astype(F32)
        v = v_ref[0, rows, :]
        b_last = b[c_len - 1:c_len, :]
        qe = (q * jnp.exp(b)).astype(BF16)
        o = _dot(att.astype(BF16), v) + _dot_nt(qe, st.astype(BF16))
        kd = (k * jnp.exp(b_last - b)).astype(BF16)
        st = st * jnp.exp(b_last) + _dot_tn(v, kd)
        y = o * lax.rsqrt(jnp.mean(o * o, axis=-1, keepdims=True) + EPS)
        o_ref[0, rows, :] = (y * ng * gate_ref[0, rows, :].astype(F32)).astype(o_ref.dtype)
        return st

    @pl.when(safe)
    def _():
        st = st_ref[...]
        for c in range(n_chunks):
            rows = slice(c * c_len, (c + 1) * c_len)
            b = b_ref[rows, :]
            q = q_ref[0, rows, :].astype(F32)
            k = k_ref[0, rows, :].astype(F32)
            att_rows = []
            for i in range(c_len // sub):
                lo, hi = i * sub, (i + 1) * sub
                ref = b[lo - 1:lo, :] if i > 0 else jnp.zeros((1, LANES), F32)
                qt = (q[lo:hi] * jnp.exp(b[lo:hi] - ref)).astype(BF16)
                kt = (k[:hi] * jnp.exp(ref - b[:hi])).astype(BF16)
                if hi < c_len:
                    kt = jnp.concatenate([kt, jnp.zeros((c_len - hi, LANES), BF16)], axis=0)
                att_rows.append(_dot_nt(qt, kt))
            att = jnp.where(causal, jnp.concatenate(att_rows, axis=0), 0.0)
            st = finish(c, att, st)
        st_ref[...] = st

    @pl.when(jnp.logical_not(safe))
    def _():
        def chunk(c, st):
            rows = _rows(c, c_len)
            b = b_ref[rows, :]
            q = q_ref[0, rows, :].astype(F32)
            kf_ref[...] = k_ref[0, rows, :].astype(F32)

            def key(s, att):
                b_s = b_ref[pl.ds(c * c_len + s, 1), :]
                dec = jnp.exp(jnp.minimum(b - b_s, 0.0))
                col = jnp.sum(q * kf_ref[pl.ds(s, 1), :] * dec, axis=-1, keepdims=True)
                return jnp.where(c_i == s, col, att)

            att = lax.fori_loop(0, c_len, key, jnp.zeros((c_len, c_len), F32))
            return finish(c, jnp.where(causal, att, 0.0), st)

        st_ref[...] = lax.fori_loop(0, n_chunks, chunk, st_ref[...])


def _hgrn(q, k, lf, v, gate, norm_g, bsz, seq, ts=512):
    kw = q.shape[-1] // C_HEADS
    vw = v.shape[-1] // C_HEADS
    assert kw == LANES and vw == LANES
    ts = min(ts, seq)
    v3 = lambda t: t.reshape(bsz, seq, t.shape[-1])
    blk = pl.BlockSpec((1, ts, LANES), lambda b, h, c: (b, c, h))
    kern = functools.partial(_hgrn_kernel, n_chunks=ts // REC_CHUNK)
    out = pl.pallas_call(
        kern,
        grid=(bsz, C_HEADS, seq // ts),
        in_specs=[blk, blk, blk, blk, blk, pl.BlockSpec((1, LANES), lambda b, h, c: (0, h))],
        out_specs=blk,
        out_shape=jax.ShapeDtypeStruct((bsz, seq, C_HEADS * vw), BF16),
        scratch_shapes=[pltpu.VMEM((vw, kw), F32), pltpu.VMEM((ts, LANES), F32),
                        pltpu.VMEM((REC_CHUNK, LANES), F32)],
        compiler_params=_params("parallel", "parallel", "arbitrary"),
        name="hgrn2",
    )(v3(q), v3(k), v3(lf), v3(v), v3(gate), norm_g.reshape(1, -1))
    return out.reshape(bsz * seq, C_HEADS * vw)


ROUTE_E, ROUTE_RANK, ROUTE_GATE = 0, 2, 4


def _router_kernel(x_ref, a_ref, wo_ref, g_ref, wr_hi_ref, wr_lo_ref, x3_ref, h_ref, route_ref,
                   cnt_ref, base_ref):
    @pl.when(pl.program_id(0) == 0)
    def _():
        base_ref[...] = jnp.zeros_like(base_ref)

    x3 = x_ref[...] + _dot(a_ref[...], wo_ref[...])
    x3_ref[...] = x3
    h = _rms(x3, g_ref[...])
    _to_row_tiles(h_ref, h)
    h_hi = h.astype(BF16)
    h_lo = (h - h_hi.astype(F32)).astype(BF16)
    logits = _dot(h_hi, wr_hi_ref[...]) + _dot(h_hi, wr_lo_ref[...]) + _dot(h_lo, wr_hi_ref[...])
    tm = logits.shape[0]
    lane = lax.broadcasted_iota(I32, logits.shape, 1)
    logits = jnp.where(lane < N_EXPERTS, logits, -jnp.inf)
    lane_f = lane.astype(F32)
    v1 = jnp.max(logits, axis=-1, keepdims=True)
    e1 = jnp.min(jnp.where(logits == v1, lane_f, float(LANES)), axis=-1, keepdims=True)
    hot1 = lane_f == e1
    rest = jnp.where(hot1, -jnp.inf, logits)
    v2 = jnp.max(rest, axis=-1, keepdims=True)
    e2 = jnp.min(jnp.where(rest == v2, lane_f, float(LANES)), axis=-1, keepdims=True)
    hot2 = lane_f == e2
    t = jnp.exp(v2 - v1)
    g1 = 1.0 / (1.0 + t)
    g2 = t / (1.0 + t)
    e1 = e1.astype(I32)
    e2 = e2.astype(I32)
    member = (hot1 | hot2).astype(BF16)
    r_i = lax.broadcasted_iota(I32, (tm, tm), 0)
    c_i = lax.broadcasted_iota(I32, (tm, tm), 1)
    before = _dot((c_i < r_i).astype(BF16), member) + base_ref[...]
    rank1 = jnp.sum(jnp.where(hot1, before, 0.0), axis=-1, keepdims=True).astype(I32)
    rank2 = jnp.sum(jnp.where(hot2, before, 0.0), axis=-1, keepdims=True).astype(I32)
    base = base_ref[...] + jnp.sum(member.astype(F32), axis=0, keepdims=True)
    base_ref[...] = base
    cnt_ref[...] = base.astype(I32)
    route = jnp.where(lane == ROUTE_E, e1, 0)
    route = jnp.where(lane == ROUTE_E + 1, e2, route)
    route = jnp.where(lane == ROUTE_RANK, rank1, route)
    route = jnp.where(lane == ROUTE_RANK + 1, rank2, route)
    gate_bits = lax.bitcast_convert_type(jnp.where(lane == ROUTE_GATE, g1, g2), I32)
    route = jnp.where((lane == ROUTE_GATE) | (lane == ROUTE_GATE + 1), gate_bits, route)
    route_ref[...] = route


def _router(x, a, wo, g, w_router, tm=512):
    n, d = x.shape
    ka = a.shape[1]
    wr = jnp.zeros((d, LANES), F32).at[:, :N_EXPERTS].set(w_router)
    wr_hi = wr.astype(BF16)
    wr_lo = (wr - wr_hi.astype(F32)).astype(BF16)
    row = lambda i: (i, 0)
    fix = lambda i: (0, 0)
    return pl.pallas_call(
        _router_kernel,
        grid=(n // tm,),
        in_specs=[pl.BlockSpec((tm, d), row), pl.BlockSpec((tm, ka), row),
                  pl.BlockSpec((ka, d), fix), pl.BlockSpec((1, d), fix),
                  pl.BlockSpec((d, LANES), fix), pl.BlockSpec((d, LANES), fix)],
        out_specs=[pl.BlockSpec((tm, d), row), pl.BlockSpec((tm * d // LANES, LANES), row),
                   pl.BlockSpec((tm, LANES), row), pl.BlockSpec((1, LANES), fix)],
        out_shape=[jax.ShapeDtypeStruct((n, d), F32), jax.ShapeDtypeStruct((n * d // LANES, LANES), F32),
                   jax.ShapeDtypeStruct((n, LANES), I32), jax.ShapeDtypeStruct((1, LANES), I32)],
        scratch_shapes=[pltpu.VMEM((1, LANES), F32)],
        compiler_params=_params("arbitrary"),
        name="router",
    )(x, a, wo, g.reshape(1, d), wr_hi, wr_lo)


def _experts_kernel(be_ref, nb_ref, src0_ref, srcn_ref, slot_ref, h_hbm, wg_ref, wu_ref, wd_ref,
                    y_hbm, in_ref, out_ref, acc_ref, hb_ref, gsem, ssem, *, rps, nf, n_tok, second):
    i = pl.program_id(0)
    f = pl.program_id(1)
    nb = nb_ref[0]
    tm, d = hb_ref.shape
    pieces = d // LANES
    assert rps * nf == tm

    def tile(r):
        return pl.ds(pl.multiple_of(r * pieces, pieces), pieces)

    def gather_row(src_ref, r):
        return pltpu.make_async_copy(h_hbm.at[tile(src_ref[r])], in_ref.at[tile(r)], gsem)

    def scatter_row(r):
        return pltpu.make_async_copy(out_ref.at[tile(r)], y_hbm.at[tile(slot_ref[r])], ssem)

    def spare_fill(region, k):
        rows = pl.ds((region * second + n_tok + k * tm) * pieces, tm * pieces)
        return pltpu.make_async_copy(out_ref, y_hbm.at[rows], ssem)

    def wait_gather():
        pltpu.make_async_copy(h_hbm.at[pl.ds(0, tm * pieces)], in_ref, gsem).wait()

    def wait_scatter():
        pltpu.make_async_copy(out_ref, y_hbm.at[pl.ds(0, tm * pieces)], ssem).wait()

    @pl.when((i == 0) & (f == 0))
    def _():
        out_ref[...] = jnp.zeros_like(out_ref)
        fills = [spare_fill(region, k) for region in range(2) for k in range((second - n_tok) // tm)]
        for cp in fills:
            cp.start()
        for cp in fills:
            cp.wait()

        def body(it, carry):
            for u in range(8):
                gather_row(src0_ref, it * 8 + u).start()
            return carry

        lax.fori_loop(0, tm // 8, body, 0)

    def start_rows(with_gather):
        for u in range(rps):
            r = f * rps + u
            if with_gather:
                gather_row(srcn_ref, r).start()
            scatter_row(r).start()

    @pl.when(i < nb)
    def _():
        @pl.when(f == 0)
        def _():
            wait_gather()
            for j in range(pieces):
                hb_ref[:, j * LANES:(j + 1) * LANES] = _row_tile_piece(in_ref, j, tm, pieces).astype(BF16)
            acc_ref[...] = jnp.zeros_like(acc_ref)

        start_rows(True)
        h = hb_ref[...]
        act = _silu(_dot(h, wg_ref[...])) * _dot(h, wu_ref[...])
        acc_ref[...] += _dot(act.astype(BF16), wd_ref[...])

        @pl.when(f == nf - 1)
        def _():
            wait_scatter()
            _to_row_tiles(out_ref, acc_ref[...])

    @pl.when(i == nb)
    def _():
        @pl.when(f == 0)
        def _():
            wait_gather()

        start_rows(False)

        @pl.when(f == nf - 1)
        def _():
            wait_scatter()


def _experts(h, src, slot, blk_e, nb_used, w_gate, w_up, w_down, tm, second, tf=896):
    d = w_gate.shape[1]
    pieces = d // LANES
    n = h.shape[0] // pieces
    ff = w_gate.shape[2]
    nf = ff // tf
    n_blocks = src.shape[0] // tm
    rps = tm // nf

    def clamp(i, f, be, nb):
        live = i < nb[0]
        return be[jnp.minimum(i, nb[0] - 1)], jnp.where(live, f, nf - 1)

    def wg_map(i, f, be, nb):
        e, fe = clamp(i, f, be, nb)
        return (e, 0, fe)

    def wd_map(i, f, be, nb):
        e, fe = clamp(i, f, be, nb)
        return (e, fe, 0)

    smem = lambda index_map: pl.BlockSpec((tm,), index_map, memory_space=pltpu.SMEM)
    grid_spec = pltpu.PrefetchScalarGridSpec(
        num_scalar_prefetch=2,
        grid=(n_blocks + 1, nf),
        in_specs=[smem(lambda i, f, be, nb: (0,)),
                  smem(lambda i, f, be, nb: (jnp.minimum(i + 1, nb[0] - 1),)),
                  smem(lambda i, f, be, nb: (jnp.clip(i - 1, 0, nb[0] - 1),)),
                  pl.BlockSpec(memory_space=pl.ANY),
                  pl.BlockSpec((None, d, tf), wg_map),
                  pl.BlockSpec((None, d, tf), wg_map),
                  pl.BlockSpec((None, tf, d), wd_map)],
        out_specs=pl.BlockSpec(memory_space=pl.ANY),
        scratch_shapes=[pltpu.VMEM((tm * pieces, LANES), F32), pltpu.VMEM((tm * pieces, LANES), F32),
                        pltpu.VMEM((tm, d), F32), pltpu.VMEM((tm, d), BF16),
                        pltpu.SemaphoreType.DMA(()), pltpu.SemaphoreType.DMA(())],
    )
    kern = functools.partial(_experts_kernel, rps=rps, nf=nf, n_tok=n, second=second)
    return pl.pallas_call(
        kern,
        grid_spec=grid_spec,
        out_shape=jax.ShapeDtypeStruct((2 * second * pieces, LANES), F32),
        compiler_params=_params("arbitrary", "arbitrary"),
        name="moe_experts",
    )(blk_e, nb_used, src, src, slot, h, w_gate, w_up, w_down)


def _combine_kernel(x_ref, route_ref, g_ref, ya_ref, yb_ref, o_ref):
    route = route_ref[...]
    lane = lax.broadcasted_iota(I32, route.shape, 1)
    gates = lax.bitcast_convert_type(route, F32)
    g1 = jnp.sum(jnp.where(lane == ROUTE_GATE, gates, 0.0), axis=-1, keepdims=True)
    g2 = jnp.sum(jnp.where(lane == ROUTE_GATE + 1, gates, 0.0), axis=-1, keepdims=True)
    tm, d = x_ref.shape
    pieces = d // LANES
    x4 = []
    for j in range(pieces):
        ya = _row_tile_piece(ya_ref, j, tm, pieces)
        yb = _row_tile_piece(yb_ref, j, tm, pieces)
        x4.append(x_ref[:, j * LANES:(j + 1) * LANES] + (ya * g1 + yb * g2))
    ms = sum(jnp.sum(p * p, axis=-1, keepdims=True) for p in x4) * (1.0 / d)
    inv = lax.rsqrt(ms + EPS)
    for j in range(pieces):
        cols = slice(j * LANES, (j + 1) * LANES)
        o_ref[:, cols] = x4[j] * inv * g_ref[:, cols]


def _combine(x, route, y, g, second, tm=512):
    n, d = x.shape
    off = second // tm
    pieces = d // LANES
    return pl.pallas_call(
        _combine_kernel,
        grid=(n // tm,),
        in_specs=[pl.BlockSpec((tm, d), lambda i: (i, 0)),
                  pl.BlockSpec((tm, LANES), lambda i: (i, 0)),
                  pl.BlockSpec((1, d), lambda i: (0, 0)),
                  pl.BlockSpec((tm * pieces, LANES), lambda i: (i, 0)),
                  pl.BlockSpec((tm * pieces, LANES), lambda i: (i + off, 0))],
        out_specs=pl.BlockSpec((tm, d), lambda i: (i, 0)),
        out_shape=jax.ShapeDtypeStruct((n, d), F32),
        compiler_params=_params("parallel"),
        name="moe_combine",
    )(x, route, g.reshape(1, d), y, y)


def _moe_layout(route, counts, n, tm):
    e = route[:, ROUTE_E:ROUTE_E + 2]
    rank = route[:, ROUTE_RANK:ROUTE_RANK + 2]
    sizes = counts[0, :N_EXPERTS]
    padded = ((sizes + tm - 1) // tm) * tm
    pend = jnp.cumsum(padded)
    pstart = pend - padded
    start_of = jnp.zeros_like(e)
    for j in range(N_EXPERTS):
        start_of = jnp.where(e == j, pstart[j], start_of)
    dest = (start_of + rank).astype(I32)
    n_blocks = (2 * n) // tm + N_EXPERTS
    rows = n_blocks * tm
    blk_e = jnp.minimum(jnp.searchsorted(pend, jnp.arange(n_blocks, dtype=I32) * tm, side='right'),
                        N_EXPERTS - 1).astype(I32)
    nb_used = (pend[-1:] // tm).astype(I32)
    second = n + N_EXPERTS * tm
    pair_slot = jnp.arange(n, dtype=I32)[:, None] + jnp.array([0, second], I32)[None, :]
    row = jnp.arange(rows, dtype=I32)
    row_e = jnp.repeat(blk_e, tm)
    pad_j = jnp.clip(row - (pstart + sizes)[row_e], 0, tm - 1)
    spare = n + row_e * tm + pad_j
    slot = spare.at[dest.reshape(-1)].set(pair_slot.reshape(-1))
    token = slot % second
    src = jnp.where(token < n, token, 0)
    return src, slot, blk_e, nb_used, second


def kernel(x, positions, g_mix, g_ffn, g_final, w_in_attn, w_out_attn, attn_sinks, w_in_rec,
           rec_lower_bounds, rec_norm_g, w_out_rec, w_gate_dense, w_up_dense, w_down_dense,
           w_router, w_gate_moe, w_up_moe, w_down_moe):
    bsz, seq, d = x.shape
    n = bsz * seq
    x0 = x.reshape(n, d)
    bf = lambda w: w.astype(BF16)

    cos_t, sin_t = _rope_tables(positions)
    qkv1, qkv4, qkv16, qkvb = _attn_inproj(x, g_mix[0], bf(w_in_attn[0]), cos_t, sin_t)
    mix_a = _dilated_attention(qkv1, qkv4, qkv16)
    mix_b = _swa_gqa(qkvb, attn_sinks[0])
    a_w = A_HEADS * HEAD_DIM
    w_out = bf(w_out_attn[0])
    x2 = _outproj_ffn(x0, mix_a.reshape(n, -1), mix_b.reshape(n, -1), w_out[:a_w], w_out[a_w:],
                      g_ffn[0], bf(w_gate_dense[0]), bf(w_up_dense[0]), bf(w_down_dense[0]))

    lb = jax.nn.softmax(rec_lower_bounds.astype(F32), axis=0)
    lb1 = (jnp.cumsum(lb, axis=0) - lb[0])[1]
    kw = lb1.shape[0]
    vw = rec_norm_g.shape[1]
    q, k, lf, v, gate = _rec_inproj(x2, g_mix[1], bf(w_in_rec[0]), lb1, kw, vw)
    rec = _hgrn(q, k, lf, v, gate, rec_norm_g[0], bsz, seq)
    x3, h, route, counts = _router(x2, rec, bf(w_out_rec[0]), g_ffn[1], w_router[0])
    tm_moe = min(1024, n)
    src, slot, blk_e, nb_used, second = _moe_layout(route, counts, n, tm_moe)
    ys = _experts(h, src, slot, blk_e, nb_used, bf(w_gate_moe[0]), bf(w_up_moe[0]),
                  bf(w_down_moe[0]), tm_moe, second)
    out = _combine(x3, route, ys, g_final, second)
    return out.reshape(bsz, seq, d)
```

```python
import functools

import jax
import jax.numpy as jnp
from jax import lax
from jax.experimental import pallas as pl
from jax.experimental.pallas import tpu as pltpu

F32 = jnp.float32
BF16 = jnp.bfloat16
I32 = jnp.int32

EPS = 1e-6
HEAD_DIM = 64
ROT_DIM = HEAD_DIM // 4
ROPE_THETA = 500000.0
LANES = 128
ATTN_BLOCK = 128
A_HEADS = 8
A_PATTERNS = ((128, 1), (512, 4), (2048, 16))
B_Q_HEADS = 8
B_KV_HEADS = 2
B_WINDOW = 128
C_HEADS = 8
N_EXPERTS = 8
REC_CHUNK = 128
REC_SUB = 16
REC_SAFE_DROP = -80.0
EXPERT_COL_CHUNK = 256
VMEM_LIMIT = 56 * 1024 * 1024


def _params(*sem):
    return pltpu.CompilerParams(dimension_semantics=sem, vmem_limit_bytes=VMEM_LIMIT)


def _rms(x, g):
    return x * lax.rsqrt(jnp.mean(x * x, axis=-1, keepdims=True) + EPS) * g


def _silu(x):
    return x / (1.0 + jnp.exp(-x))


def _dot(a, b):
    return jnp.dot(a, b, preferred_element_type=F32)


def _dot_nt(a, b):
    return lax.dot_general(a, b, (((1,), (1,)), ((), ())), preferred_element_type=F32)


def _dot_tn(a, b):
    return lax.dot_general(a, b, (((0,), (0,)), ((), ())), preferred_element_type=F32)


def _rows(j, size, count=1):
    if isinstance(j, int):
        return slice(j * size, (j + count) * size)
    return pl.ds(pl.multiple_of(j * size, size), count * size)


def _to_row_tiles(ref, x):
    pieces = x.shape[1] // LANES
    for j in range(pieces):
        ref[pl.ds(j, x.shape[0], stride=pieces), :] = x[:, j * LANES:(j + 1) * LANES]


def _row_tile_piece(ref, j, m, pieces):
    return ref[pl.ds(j, m, stride=pieces), :]


def _rope_table_kernel(pos_ref, invf_ref, cos_ref, sin_ref):
    ang = pos_ref[...].astype(F32) * invf_ref[...]
    cos_ref[...] = jnp.cos(ang)
    sin_ref[...] = jnp.sin(ang)


def _rope_tables(positions):
    n = positions.size
    half = ROT_DIM // 2
    per_row = LANES // half
    inv_freq = jnp.power(ROPE_THETA, -jnp.arange(0, ROT_DIM, 2, dtype=F32) / ROT_DIM)
    pos_rep = jnp.repeat(positions.reshape(n // per_row, per_row), half, axis=1)
    invf_row = jnp.tile(inv_freq, per_row).reshape(1, LANES)
    rows = n // per_row
    tr = min(rows, 512)
    cos8, sin8 = pl.pallas_call(
        _rope_table_kernel,
        grid=(rows // tr,),
        in_specs=[pl.BlockSpec((tr, LANES), lambda i: (i, 0)),
                  pl.BlockSpec((1, LANES), lambda i: (0, 0))],
        out_specs=[pl.BlockSpec((tr, LANES), lambda i: (i, 0))] * 2,
        out_shape=[jax.ShapeDtypeStruct((rows, LANES), F32)] * 2,
        compiler_params=_params("parallel"),
        name="rope_tables",
    )(pos_rep, invf_row)
    cos8 = cos8.reshape(n, half)
    sin8 = sin8.reshape(n, half)
    rest = HEAD_DIM - ROT_DIM
    c64 = jnp.concatenate([cos8, cos8, jnp.ones((n, rest), F32)], axis=1)
    s64 = jnp.concatenate([-sin8, sin8, jnp.zeros((n, rest), F32)], axis=1)
    return jnp.tile(c64, (1, 2)), jnp.tile(s64, (1, 2))


A_BLOCKS = A_HEADS * HEAD_DIM // LANES
BQ_BLOCKS = B_Q_HEADS * HEAD_DIM // LANES
A_QKV = 3 * A_BLOCKS * LANES
B_QKV = (BQ_BLOCKS + 2 * B_KV_HEADS) * LANES


def _attn_inproj_kernel(x_ref, g_ref, w_ref, c_ref, s_ref, o1_ref, o4_ref, o16_ref, ob_ref, sc_ref):
    h = _rms(x_ref[0], g_ref[...]).astype(BF16)
    acc = _dot(h, w_ref[...])
    tm = acc.shape[0]
    c = c_ref[0]
    s = s_ref[0]
    lane = lax.broadcasted_iota(I32, c.shape, 1)
    first = (lane % HEAD_DIM) < (ROT_DIM // 2)
    lo_half = lane < HEAD_DIM

    def rope(blk):
        up = pltpu.roll(blk, LANES - ROT_DIM // 2, 1)
        dn = pltpu.roll(blk, ROT_DIM // 2, 1)
        return blk * c + jnp.where(first, up, dn) * s

    def col(cb):
        return acc[:, cb * LANES:(cb + 1) * LANES]

    scale = HEAD_DIM ** -0.5
    for cb in range(3 * A_BLOCKS):
        blk = col(cb)
        if cb < A_BLOCKS:
            blk = rope(blk) * scale
        elif cb < 2 * A_BLOCKS:
            blk = rope(blk)
        sc_ref[cb] = blk
        o1_ref[0, :, cb * LANES:(cb + 1) * LANES] = blk.astype(BF16)
    for o_ref in (o4_ref, o16_ref):
        dil = o_ref.shape[1]
        for r in range(dil):
            for cb in range(3 * A_BLOCKS):
                o_ref[0, r, :, cb * LANES:(cb + 1) * LANES] = (
                    sc_ref[cb, pl.ds(r, tm // dil, stride=dil), :].astype(BF16))
    base = 3 * A_BLOCKS
    for j in range(BQ_BLOCKS):
        ob_ref[0, :, j * LANES:(j + 1) * LANES] = (rope(col(base + j)) * scale).astype(BF16)
    for j, blk in enumerate((rope(col(base + BQ_BLOCKS)), col(base + BQ_BLOCKS + 1))):
        swapped = pltpu.roll(blk, HEAD_DIM, 1)
        for g, dup in enumerate((jnp.where(lo_half, blk, swapped), jnp.where(lo_half, swapped, blk))):
            cb = BQ_BLOCKS + j * B_KV_HEADS + g
            ob_ref[0, :, cb * LANES:(cb + 1) * LANES] = dup.astype(BF16)


def _attn_inproj(x, g, w, cos_t, sin_t, tm=512):
    bsz, seq, d = x.shape
    width = w.shape[1]
    assert B_KV_HEADS * HEAD_DIM == LANES and width == A_QKV + (BQ_BLOCKS + 2) * LANES
    tile = lambda b, i: (b, i, 0)
    fix = lambda b, i: (0, 0)
    d4, d16 = A_PATTERNS[1][1], A_PATTERNS[2][1]
    perm = lambda dil: pl.BlockSpec((1, dil, tm // dil, A_QKV), lambda b, i: (b, 0, i, 0))
    return pl.pallas_call(
        _attn_inproj_kernel,
        grid=(bsz, seq // tm),
        in_specs=[pl.BlockSpec((1, tm, d), tile),
                  pl.BlockSpec((1, d), fix),
                  pl.BlockSpec((d, width), fix),
                  pl.BlockSpec((1, tm, LANES), tile),
                  pl.BlockSpec((1, tm, LANES), tile)],
        out_specs=[pl.BlockSpec((1, tm, A_QKV), tile), perm(d4), perm(d16),
                   pl.BlockSpec((1, tm, B_QKV), tile)],
        out_shape=[jax.ShapeDtypeStruct((bsz, seq, A_QKV), BF16),
                   jax.ShapeDtypeStruct((bsz, d4, seq // d4, A_QKV), BF16),
                   jax.ShapeDtypeStruct((bsz, d16, seq // d16, A_QKV), BF16),
                   jax.ShapeDtypeStruct((bsz, seq, B_QKV), BF16)],
        scratch_shapes=[pltpu.VMEM((3 * A_BLOCKS, tm, LANES), F32)],
        compiler_params=_params("parallel", "parallel"),
        name="attn_inproj",
    )(x, g.reshape(1, d), w, cos_t.reshape(bsz, seq, LANES), sin_t.reshape(bsz, seq, LANES))


def _band_mask(n_back):
    qi = lax.broadcasted_iota(I32, (ATTN_BLOCK, 2 * ATTN_BLOCK), 0)
    kj = lax.broadcasted_iota(I32, (ATTN_BLOCK, 2 * ATTN_BLOCK), 1)
    dist = ATTN_BLOCK + qi - kj
    return (dist >= 0) & (dist <= n_back), kj >= ATTN_BLOCK


def _band_block(q_pairs, kk, vv, valid, sinks=None, want_lse=True):
    lane = lax.broadcasted_iota(I32, (ATTN_BLOCK, LANES), 1)
    lo_half = lane < HEAD_DIM
    zero = jnp.zeros((ATTN_BLOCK, LANES), BF16)
    lhs = []
    for q in q_pairs:
        lhs += [jnp.where(lo_half, q, zero), jnp.where(lo_half, zero, q)]
    s_all = _dot_nt(jnp.concatenate(lhs, axis=0), kk)
    ps, inv_l, lses = [], [], []
    for u in range(len(lhs)):
        s = jnp.where(valid, s_all[u * ATTN_BLOCK:(u + 1) * ATTN_BLOCK], -jnp.inf)
        m = jnp.max(s, axis=-1, keepdims=True)
        if sinks is not None:
            m = jnp.maximum(m, sinks[u])
        e = jnp.exp(s - m)
        l = jnp.sum(e, axis=-1, keepdims=True)
        if sinks is not None:
            l = l + jnp.exp(sinks[u] - m)
        ps.append(e.astype(BF16))
        inv_l.append(1.0 / l)
        lses.append(m + jnp.log(l) if want_lse else None)
    o_all = _dot(jnp.concatenate(ps, axis=0), vv)
    outs = []
    for j in range(len(q_pairs)):
        o0 = o_all[(2 * j) * ATTN_BLOCK:(2 * j + 1) * ATTN_BLOCK] * inv_l[2 * j]
        o1 = o_all[(2 * j + 1) * ATTN_BLOCK:(2 * j + 2) * ATTN_BLOCK] * inv_l[2 * j + 1]
        lse = jnp.where(lo_half, lses[2 * j], lses[2 * j + 1]) if want_lse else None
        outs.append((jnp.where(lo_half, o0, o1), lse))
    return outs


def _for_each_block(nq, first_fn, rest_fn):
    first_fn()
    if nq > 1:
        def body(jb, carry):
            rest_fn(jb)
            return carry
        lax.fori_loop(1, nq, body, 0)


def _dilated_kernel(q1, kp1, kc1, vp1, vc1, q4, kp4, kc4, vp4, vc4, q16, kp16, kc16, vp16, vc16,
                    o_ref, o4_s, l4_s, o16_s, l16_s, *, n_backs):
    tile = o_ref.shape[1]
    not_first = pl.program_id(2) > 0

    def masks(n_back):
        band, in_cur = _band_mask(n_back)
        return band & (in_cur | not_first), band

    def halo(p_ref, c_ref, idx):
        return jnp.concatenate([p_ref[idx], c_ref[idx + (slice(0, ATTN_BLOCK),)]], axis=0)

    def window(c_ref, idx, jb):
        return c_ref[idx + (_rows(jb - 1, ATTN_BLOCK, 2),)]

    def rows(jb):
        return _rows(jb, ATTN_BLOCK)

    for (q, kp, kc, vp, vc, o_s, l_s), n_back in zip(
            ((q16, kp16, kc16, vp16, vc16, o16_s, l16_s), (q4, kp4, kc4, vp4, vc4, o4_s, l4_s)),
            (n_backs[2], n_backs[1])):
        dil = q.shape[1]
        nq = q.shape[2] // ATTN_BLOCK
        valid0, valid = masks(n_back)
        for r in range(dil):
            idx = (0, r)

            def put(jb, res, r=r, dil=dil, o_s=o_s, l_s=l_s):
                (o, lse), = res
                dst = pl.ds(jb * ATTN_BLOCK * dil + r, ATTN_BLOCK, stride=dil)
                o_s[dst, :] = o
                l_s[dst, :] = lse

            def first(q=q, kp=kp, kc=kc, vp=vp, vc=vc, idx=idx, put=put, valid0=valid0):
                put(0, _band_block([q[idx + (slice(0, ATTN_BLOCK),)]], halo(kp, kc, idx),
                                   halo(vp, vc, idx), valid0))

            def rest(jb, q=q, kc=kc, vc=vc, idx=idx, put=put, valid=valid):
                put(jb, _band_block([q[idx + (rows(jb),)]], window(kc, idx, jb),
                                    window(vc, idx, jb), valid))

            _for_each_block(nq, first, rest)

    valid0, valid = masks(n_backs[0])
    idx = (0,)

    def merge(jb, res):
        (o, lse), = res
        dst = rows(jb)
        l4 = l4_s[dst, :]
        l16 = l16_s[dst, :]
        mx = jnp.maximum(jnp.maximum(lse, l4), l16)
        w1 = jnp.exp(lse - mx)
        w4 = jnp.exp(l4 - mx)
        w16 = jnp.exp(l16 - mx)
        num = w1 * o + w4 * o4_s[dst, :] + w16 * o16_s[dst, :]
        o_ref[0, dst, :] = (num / (w1 + w4 + w16)).astype(o_ref.dtype)

    _for_each_block(
        tile // ATTN_BLOCK,
        lambda: merge(0, _band_block([q1[0, 0:ATTN_BLOCK]], halo(kp1, kc1, idx), halo(vp1, vc1, idx), valid0)),
        lambda jb: merge(jb, _band_block([q1[0, rows(jb)]], window(kc1, idx, jb), window(vc1, idx, jb), valid)))


def _dilated_attention(qkv1, qkv4, qkv16):
    bsz, seq, _ = qkv1.shape
    dils = tuple(p[1] for p in A_PATTERNS)
    assert dils[0] == 1 and qkv4.shape[1] == dils[1] and qkv16.shape[1] == dils[2]
    tile = dils[2] * ATTN_BLOCK
    nt = seq // tile

    def specs(dil):
        rows = tile // dil
        nb = rows // ATTN_BLOCK
        if dil == 1:
            cur = lambda off: pl.BlockSpec((1, rows, LANES), lambda b, hp, i: (b, i, off + hp))
            prev = lambda off: pl.BlockSpec(
                (1, ATTN_BLOCK, LANES), lambda b, hp, i: (b, jnp.maximum(i * nb - 1, 0), off + hp))
        else:
            cur = lambda off: pl.BlockSpec((1, dil, rows, LANES), lambda b, hp, i: (b, 0, i, off + hp))
            prev = lambda off: pl.BlockSpec(
                (1, dil, ATTN_BLOCK, LANES), lambda b, hp, i: (b, 0, jnp.maximum(i * nb - 1, 0), off + hp))
        return [cur(0), prev(A_BLOCKS), cur(A_BLOCKS), prev(2 * A_BLOCKS), cur(2 * A_BLOCKS)]

    kern = functools.partial(_dilated_kernel, n_backs=tuple(w // d for w, d in A_PATTERNS))
    return pl.pallas_call(
        kern,
        grid=(bsz, A_BLOCKS, nt),
        in_specs=specs(1) + specs(dils[1]) + specs(dils[2]),
        out_specs=pl.BlockSpec((1, tile, LANES), lambda b, hp, i: (b, i, hp)),
        out_shape=jax.ShapeDtypeStruct((bsz, seq, A_BLOCKS * LANES), BF16),
        scratch_shapes=[pltpu.VMEM((tile, LANES), F32)] * 4,
        compiler_params=_params("parallel", "parallel", "arbitrary"),
        name="dilated_attn",
    )(*([qkv1] * 5 + [qkv4] * 5 + [qkv16] * 5))


def _swa_kernel(sink_ref, q_ref, *rest, n_back, nq):
    kv = rest[:4 * B_KV_HEADS]
    o_ref = rest[4 * B_KV_HEADS]
    not_first = pl.program_id(1) > 0
    band, in_cur = _band_mask(n_back)
    valid0 = band & (in_cur | not_first)
    pairs = BQ_BLOCKS // B_KV_HEADS

    def run(q_rows, kv_of, valid):
        for g in range(B_KV_HEADS):
            cbs = [g * pairs + j for j in range(pairs)]
            qs = [q_ref[0, q_rows, cb * LANES:(cb + 1) * LANES] for cb in cbs]
            sinks = [sink_ref[2 * cb + p] for cb in cbs for p in range(2)]
            kk, vv = kv_of(g)
            res = _band_block(qs, kk, vv, valid, sinks, want_lse=False)
            for cb, (o, _) in zip(cbs, res):
                o_ref[0, q_rows, cb * LANES:(cb + 1) * LANES] = o.astype(o_ref.dtype)

    def halo_kv(g):
        kp, kc, vp, vc = kv[4 * g:4 * g + 4]
        return (jnp.concatenate([kp[0], kc[0, 0:ATTN_BLOCK]], axis=0),
                jnp.concatenate([vp[0], vc[0, 0:ATTN_BLOCK]], axis=0))

    def window_kv(jb):
        win = _rows(jb - 1, ATTN_BLOCK, 2)
        return lambda g: (kv[4 * g + 1][0, win], kv[4 * g + 3][0, win])

    _for_each_block(
        nq,
        lambda: run(_rows(0, ATTN_BLOCK), halo_kv, valid0),
        lambda jb: run(_rows(jb, ATTN_BLOCK), window_kv(jb), band))


def _swa_gqa(qkvb, sinks, tq=512):
    bsz, seq, _ = qkvb.shape
    tq = min(seq, tq)
    nq = tq // ATTN_BLOCK
    bq_w = BQ_BLOCKS * LANES
    in_specs = [pl.BlockSpec(memory_space=pltpu.SMEM),
                pl.BlockSpec((1, tq, bq_w), lambda b, i: (b, i, 0))]
    for g in range(B_KV_HEADS):
        for section in range(2):
            cb = BQ_BLOCKS + section * B_KV_HEADS + g
            in_specs += [pl.BlockSpec((1, ATTN_BLOCK, LANES),
                                      lambda b, i, cb=cb: (b, jnp.maximum(i * nq - 1, 0), cb)),
                         pl.BlockSpec((1, tq, LANES), lambda b, i, cb=cb: (b, i, cb))]
    args = [sinks.astype(F32)] + [qkvb] * (1 + 4 * B_KV_HEADS)
    kern = functools.partial(_swa_kernel, n_back=B_WINDOW - 1, nq=nq)
    return pl.pallas_call(
        kern,
        grid=(bsz, seq // tq),
        in_specs=in_specs,
        out_specs=pl.BlockSpec((1, tq, bq_w), lambda b, i: (b, i, 0)),
        out_shape=jax.ShapeDtypeStruct((bsz, seq, bq_w), BF16),
        compiler_params=_params("parallel", "arbitrary"),
        name="swa_gqa",
    )(*args)


def _outproj_ffn_kernel(x_ref, a_ref, b_ref, wa_ref, wb_ref, g_ref, wg_ref, wu_ref, wd_ref,
                        o_ref, h_ref):
    f = pl.program_id(1)

    @pl.when(f == 0)
    def _():
        x1 = x_ref[...] + _dot(a_ref[...], wa_ref[...]) + _dot(b_ref[...], wb_ref[...])
        o_ref[...] = x1
        h_ref[...] = _rms(x1, g_ref[...]).astype(BF16)

    h = h_ref[...]
    act = _silu(_dot(h, wg_ref[...])) * _dot(h, wu_ref[...])
    o_ref[...] += _dot(act.astype(BF16), wd_ref[...])


def _outproj_ffn(x, mix_a, mix_b, w_a, w_b, g, w_gate, w_up, w_down, tm=1024, tf=512):
    n, d = x.shape
    ff = w_gate.shape[1]
    ka, kb = mix_a.shape[1], mix_b.shape[1]
    return pl.pallas_call(
        _outproj_ffn_kernel,
        grid=(n // tm, ff // tf),
        in_specs=[pl.BlockSpec((tm, d), lambda i, f: (i, 0)),
                  pl.BlockSpec((tm, ka), lambda i, f: (i, 0)),
                  pl.BlockSpec((tm, kb), lambda i, f: (i, 0)),
                  pl.BlockSpec((ka, d), lambda i, f: (0, 0)),
                  pl.BlockSpec((kb, d), lambda i, f: (0, 0)),
                  pl.BlockSpec((1, d), lambda i, f: (0, 0)),
                  pl.BlockSpec((d, tf), lambda i, f: (0, f)),
                  pl.BlockSpec((d, tf), lambda i, f: (0, f)),
                  pl.BlockSpec((tf, d), lambda i, f: (f, 0))],
        out_specs=pl.BlockSpec((tm, d), lambda i, f: (i, 0)),
        out_shape=jax.ShapeDtypeStruct((n, d), F32),
        scratch_shapes=[pltpu.VMEM((tm, d), BF16)],
        compiler_params=_params("parallel", "arbitrary"),
        name="outproj_ffn",
    )(x, mix_a, mix_b, w_a, w_b, g.reshape(1, d), w_gate, w_up, w_down)


def _rec_inproj_kernel(x_ref, g_ref, w_ref, lb_ref, q_ref, k_ref, lf_ref, v_ref, gate_ref):
    h = _rms(x_ref[...], g_ref[...]).astype(BF16)
    acc = _dot(h, w_ref[...])
    kw = q_ref.shape[1]
    vw = v_ref.shape[1]
    lb = lb_ref[...]
    q_ref[...] = _silu(acc[:, :kw]).astype(BF16)
    fg = lb + (1.0 - lb) / (1.0 + jnp.exp(-acc[:, kw:2 * kw]))
    k_ref[...] = (1.0 - fg).astype(BF16)
    lf_ref[...] = jnp.log(fg)
    v_ref[...] = acc[:, 2 * kw:2 * kw + vw].astype(BF16)
    gate_ref[...] = _silu(acc[:, 2 * kw + vw:]).astype(BF16)


def _rec_inproj(x, g, w, lb, kw, vw, tm=512):
    n, d = x.shape
    width = w.shape[1]
    row = lambda i: (i, 0)
    fix = lambda i: (0, 0)
    return pl.pallas_call(
        _rec_inproj_kernel,
        grid=(n // tm,),
        in_specs=[pl.BlockSpec((tm, d), row),
                  pl.BlockSpec((1, d), fix),
                  pl.BlockSpec((d, width), fix),
                  pl.BlockSpec((1, kw), fix)],
        out_specs=[pl.BlockSpec((tm, kw), row), pl.BlockSpec((tm, kw), row),
                   pl.BlockSpec((tm, kw), row), pl.BlockSpec((tm, vw), row),
                   pl.BlockSpec((tm, vw), row)],
        out_shape=[jax.ShapeDtypeStruct((n, kw), BF16), jax.ShapeDtypeStruct((n, kw), BF16),
                   jax.ShapeDtypeStruct((n, kw), F32), jax.ShapeDtypeStruct((n, vw), BF16),
                   jax.ShapeDtypeStruct((n, vw), BF16)],
        compiler_params=_params("parallel"),
        name="rec_inproj",
    )(x, g.reshape(1, d), w, lb.reshape(1, kw))


def _hgrn_kernel(q_ref, k_ref, lf_ref, v_ref, gate_ref, ng_ref, o_ref, st_ref, b_ref, kf_ref, *,
                 n_chunks):
    @pl.when(pl.program_id(2) == 0)
    def _():
        st_ref[...] = jnp.zeros_like(st_ref)

    c_len, sub = REC_CHUNK, REC_SUB
    r_i = lax.broadcasted_iota(I32, (c_len, c_len), 0)
    c_i = lax.broadcasted_iota(I32, (c_len, c_len), 1)
    causal = c_i <= r_i
    tril = causal.astype(BF16)
    ng = ng_ref[...]

    drop = jnp.zeros((1, LANES), F32)
    for c in range(n_chunks):
        rows = slice(c * c_len, (c + 1) * c_len)
        lf = lf_ref[0, rows, :]
        lf1 = lf.astype(BF16)
        rem = lf - lf1.astype(F32)
        lf2 = rem.astype(BF16)
        lf3 = (rem - lf2.astype(F32)).astype(BF16)
        b = _dot(tril, lf1) + _dot(tril, lf2) + _dot(tril, lf3)
        b_ref[rows, :] = b
        for i in range(c_len // sub):
            end = b[(i + 1) * sub - 1:(i + 1) * sub, :]
            drop = jnp.minimum(drop, end - b[i * sub - 1:i * sub, :] if i > 0 else end)
    safe = jnp.min(drop) > REC_SAFE_DROP

    def finish(c, att, st):
        rows = _rows(c, c_len)
        b = b_ref[rows, :]
        q = q_ref[0, rows, :].astype(F32)
        k = k_ref[0, rows, :].astype(F32)
        v = v_ref[0, rows, :]
        b_last = b[c_len - 1:c_len, :]
        qe = (q * jnp.exp(b)).astype(BF16)
        o = _dot(att.astype(BF16), v) + _dot_nt(qe, st.astype(BF16))
        kd = (k * jnp.exp(b_last - b)).astype(BF16)
        st = st * jnp.exp(b_last) + _dot_tn(v, kd)
        y = o * lax.rsqrt(jnp.mean(o * o, axis=-1, keepdims=True) + EPS)
        o_ref[0, rows, :] = (y * ng * gate_ref[0, rows, :].astype(F32)).astype(o_ref.dtype)
        return st

    @pl.when(safe)
    def _():
        st = st_ref[...]
        for c in range(n_chunks):
            rows = slice(c * c_len, (c + 1) * c_len)
            b = b_ref[rows, :]
            q = q_ref[0, rows, :].astype(F32)
            k = k_ref[0, rows, :].astype(F32)
            att_rows = []
            for i in range(c_len // sub):
                lo, hi = i * sub, (i + 1) * sub
                ref = b[lo - 1:lo, :] if i > 0 else jnp.zeros((1, LANES), F32)
                qt = (q[lo:hi] * jnp.exp(b[lo:hi] - ref)).astype(BF16)
                kt = (k[:hi] * jnp.exp(ref - b[:hi])).astype(BF16)
                if hi < c_len:
                    kt = jnp.concatenate([kt, jnp.zeros((c_len - hi, LANES), BF16)], axis=0)
                att_rows.append(_dot_nt(qt, kt))
            att = jnp.where(causal, jnp.concatenate(att_rows, axis=0), 0.0)
            st = finish(c, att, st)
        st_ref[...] = st

    @pl.when(jnp.logical_not(safe))
    def _():
        def chunk(c, st):
            rows = _rows(c, c_len)
            b = b_ref[rows, :]
            q = q_ref[0, rows, :].astype(F32)
            kf_ref[...] = k_ref[0, rows, :].astype(F32)

            def key(s, att):
                b_s = b_ref[pl.ds(c * c_len + s, 1), :]
                dec = jnp.exp(jnp.minimum(b - b_s, 0.0))
                col = jnp.sum(q * kf_ref[pl.ds(s, 1), :] * dec, axis=-1, keepdims=True)
                return jnp.where(c_i == s, col, att)

            att = lax.fori_loop(0, c_len, key, jnp.zeros((c_len, c_len), F32))
            return finish(c, jnp.where(causal, att, 0.0), st)

        st_ref[...] = lax.fori_loop(0, n_chunks, chunk, st_ref[...])


def _hgrn(q, k, lf, v, gate, norm_g, bsz, seq, ts=512):
    kw = q.shape[-1] // C_HEADS
    vw = v.shape[-1] // C_HEADS
    assert kw == LANES and vw == LANES
    ts = min(ts, seq)
    v3 = lambda t: t.reshape(bsz, seq, t.shape[-1])
    blk = pl.BlockSpec((1, ts, LANES), lambda b, h, c: (b, c, h))
    kern = functools.partial(_hgrn_kernel, n_chunks=ts // REC_CHUNK)
    out = pl.pallas_call(
        kern,
        grid=(bsz, C_HEADS, seq // ts),
        in_specs=[blk, blk, blk, blk, blk, pl.BlockSpec((1, LANES), lambda b, h, c: (0, h))],
        out_specs=blk,
        out_shape=jax.ShapeDtypeStruct((bsz, seq, C_HEADS * vw), BF16),
        scratch_shapes=[pltpu.VMEM((vw, kw), F32), pltpu.VMEM((ts, LANES), F32),
                        pltpu.VMEM((REC_CHUNK, LANES), F32)],
        compiler_params=_params("parallel", "parallel", "arbitrary"),
        name="hgrn2",
    )(v3(q), v3(k), v3(lf), v3(v), v3(gate), norm_g.reshape(1, -1))
    return out.reshape(bsz * seq, C_HEADS * vw)


ROUTE_E, ROUTE_RANK, ROUTE_GATE = 0, 2, 4


def _router_kernel(x_ref, a_ref, wo_ref, g_ref, wr_hi_ref, wr_lo_ref, x3_ref, h_ref, route_ref,
                   cnt_ref, base_ref):
    @pl.when(pl.program_id(0) == 0)
    def _():
        base_ref[...] = jnp.zeros_like(base_ref)

    x3 = x_ref[...] + _dot(a_ref[...], wo_ref[...])
    x3_ref[...] = x3
    h = _rms(x3, g_ref[...])
    _to_row_tiles(h_ref, h)
    h_hi = h.astype(BF16)
    h_lo = (h - h_hi.astype(F32)).astype(BF16)
    logits = _dot(h_hi, wr_hi_ref[...]) + _dot(h_hi, wr_lo_ref[...]) + _dot(h_lo, wr_hi_ref[...])
    tm = logits.shape[0]
    lane = lax.broadcasted_iota(I32, logits.shape, 1)
    logits = jnp.where(lane < N_EXPERTS, logits, -jnp.inf)
    lane_f = lane.astype(F32)
    v1 = jnp.max(logits, axis=-1, keepdims=True)
    e1 = jnp.min(jnp.where(logits == v1, lane_f, float(LANES)), axis=-1, keepdims=True)
    hot1 = lane_f == e1
    rest = jnp.where(hot1, -jnp.inf, logits)
    v2 = jnp.max(rest, axis=-1, keepdims=True)
    e2 = jnp.min(jnp.where(rest == v2, lane_f, float(LANES)), axis=-1, keepdims=True)
    hot2 = lane_f == e2
    t = jnp.exp(v2 - v1)
    g1 = 1.0 / (1.0 + t)
    g2 = t / (1.0 + t)
    e1 = e1.astype(I32)
    e2 = e2.astype(I32)
    member = (hot1 | hot2).astype(BF16)
    r_i = lax.broadcasted_iota(I32, (tm, tm), 0)
    c_i = lax.broadcasted_iota(I32, (tm, tm), 1)
    before = _dot((c_i < r_i).astype(BF16), member) + base_ref[...]
    rank1 = jnp.sum(jnp.where(hot1, before, 0.0), axis=-1, keepdims=True).astype(I32)
    rank2 = jnp.sum(jnp.where(hot2, before, 0.0), axis=-1, keepdims=True).astype(I32)
    base = base_ref[...] + jnp.sum(member.astype(F32), axis=0, keepdims=True)
    base_ref[...] = base
    cnt_ref[...] = base.astype(I32)
    route = jnp.where(lane == ROUTE_E, e1, 0)
    route = jnp.where(lane == ROUTE_E + 1, e2, route)
    route = jnp.where(lane == ROUTE_RANK, rank1, route)
    route = jnp.where(lane == ROUTE_RANK + 1, rank2, route)
    gate_bits = lax.bitcast_convert_type(jnp.where(lane == ROUTE_GATE, g1, g2), I32)
    route = jnp.where((lane == ROUTE_GATE) | (lane == ROUTE_GATE + 1), gate_bits, route)
    route_ref[...] = route


def _router(x, a, wo, g, w_router, tm=512):
    n, d = x.shape
    ka = a.shape[1]
    wr = jnp.zeros((d, LANES), F32).at[:, :N_EXPERTS].set(w_router)
    wr_hi = wr.astype(BF16)
    wr_lo = (wr - wr_hi.astype(F32)).astype(BF16)
    row = lambda i: (i, 0)
    fix = lambda i: (0, 0)
    return pl.pallas_call(
        _router_kernel,
        grid=(n // tm,),
        in_specs=[pl.BlockSpec((tm, d), row), pl.BlockSpec((tm, ka), row),
                  pl.BlockSpec((ka, d), fix), pl.BlockSpec((1, d), fix),
                  pl.BlockSpec((d, LANES), fix), pl.BlockSpec((d, LANES), fix)],
        out_specs=[pl.BlockSpec((tm, d), row), pl.BlockSpec((tm * d // LANES, LANES), row),
                   pl.BlockSpec((tm, LANES), row), pl.BlockSpec((1, LANES), fix)],
        out_shape=[jax.ShapeDtypeStruct((n, d), F32), jax.ShapeDtypeStruct((n * d // LANES, LANES), F32),
                   jax.ShapeDtypeStruct((n, LANES), I32), jax.ShapeDtypeStruct((1, LANES), I32)],
        scratch_shapes=[pltpu.VMEM((1, LANES), F32)],
        compiler_params=_params("arbitrary"),
        name="router",
    )(x, a, wo, g.reshape(1, d), wr_hi, wr_lo)


def _experts_kernel(be_ref, nb_ref, src0_ref, srcn_ref, slot_ref, h_hbm, wg_ref, wu_ref, wd_ref,
                    y_hbm, in_ref, out_ref, acc_ref, hb_ref, gsem, ssem, *, rps, nf, n_tok, second):
    i = pl.program_id(0)
    f = pl.program_id(1)
    nb = nb_ref[0]
    tm, d = hb_ref.shape
    pieces = d // LANES
    assert rps * nf == tm
    block = pl.ds(0, tm * pieces)

    def tile(r):
        return pl.ds(pl.multiple_of(r * pieces, pieces), pieces)

    def gather_row(src_ref, r):
        return pltpu.make_async_copy(h_hbm.at[tile(src_ref[r])], in_ref.at[tile(r)], gsem)

    def scatter_row(r):
        return pltpu.make_async_copy(out_ref.at[tile(r)], y_hbm.at[tile(slot_ref[r])], ssem)

    def spare_fill(region, k):
        rows = pl.ds((region * second + n_tok + k * tm) * pieces, tm * pieces)
        return pltpu.make_async_copy(out_ref.at[block], y_hbm.at[rows], ssem)

    def wait_gather():
        pltpu.make_async_copy(h_hbm.at[block], in_ref.at[block], gsem).wait()

    def wait_scatter():
        pltpu.make_async_copy(out_ref.at[block], y_hbm.at[block], ssem).wait()

    @pl.when((i == 0) & (f == 0))
    def _():
        out_ref[...] = jnp.zeros_like(out_ref)
        in_ref[pl.ds(tm * pieces, 8), :] = jnp.zeros((8, LANES), F32)
        fills =[spare_fill(region, k) for region in range(2) for k in range((second - n_tok) // tm)]
        for cp in fills:
            cp.start()
        for cp in fills:
            cp.wait()

        def body(it, carry):
            for u in range(8):
                gather_row(src0_ref, it * 8 + u).start()
            return carry

        lax.fori_loop(0, tm // 8, body, 0)

    def start_rows(with_gather, lo, hi):
        for u in range(lo, hi):
            r = f * rps + u
            if with_gather:
                gather_row(srcn_ref, r).start()
            scatter_row(r).start()

    tf = wg_ref.shape[1]
    bounds = list(range(0, tf, EXPERT_COL_CHUNK)) + [tf]
    col_chunks = list(zip(bounds[:-1], bounds[1:]))
    per_chunk = rps // len(col_chunks)
    assert per_chunk * len(col_chunks) == rps

    @pl.when(i < nb)
    def _():
        @pl.when(f == 0)
        def _():
            wait_gather()
            for j in range(pieces):
                hb_ref[:, j * LANES:(j + 1) * LANES] = _row_tile_piece(in_ref, j, tm, pieces).astype(BF16)
            acc_ref[...] = jnp.zeros_like(acc_ref)

        h = hb_ref[...]
        for c, (lo, hi) in enumerate(col_chunks):
            start_rows(True, c * per_chunk, (c + 1) * per_chunk)
            spare_w = pl.ds(tm * pieces + pl.multiple_of(jnp.minimum(f, 0) * 8, 8), 8)
            spare_r = pl.ds(tm * pieces + pl.multiple_of(jnp.minimum(i, 0) * 8, 8), 8)
            out_ref[spare_w, :] = in_ref[pl.ds(tm * pieces, 8), :]
            zero = out_ref[spare_r, :][0:1, 0:1]
            act = _silu(_dot(h, wg_ref[:, lo:hi])) * (_dot(h, wu_ref[:, lo:hi]) + zero)
            acc_ref[...] += _dot(act.astype(BF16), wd_ref[lo:hi, :])

        @pl.when(f == nf - 1)
        def _():
            wait_scatter()
            _to_row_tiles(out_ref, acc_ref[...])

    @pl.when(i == nb)
    def _():
        @pl.when(f == 0)
        def _():
            wait_gather()

        start_rows(False, 0, rps)

        @pl.when(f == nf - 1)
        def _():
            wait_scatter()


def _experts(h, src, slot, blk_e, nb_used, w_gate, w_up, w_down, tm, second, tf=896):
    d = w_gate.shape[1]
    pieces = d // LANES
    n = h.shape[0] // pieces
    ff = w_gate.shape[2]
    nf = ff // tf
    n_blocks = src.shape[0] // tm
    rps = tm // nf

    def clamp(i, f, be, nb):
        live = i < nb[0]
        return be[jnp.minimum(i, nb[0] - 1)], jnp.where(live, f, nf - 1)

    def wg_map(i, f, be, nb):
        e, fe = clamp(i, f, be, nb)
        return (e, 0, fe)

    def wd_map(i, f, be, nb):
        e, fe = clamp(i, f, be, nb)
        return (e, fe, 0)

    smem = lambda index_map: pl.BlockSpec((tm,), index_map, memory_space=pltpu.SMEM)
    grid_spec = pltpu.PrefetchScalarGridSpec(
        num_scalar_prefetch=2,
        grid=(n_blocks + 1, nf),
        in_specs=[smem(lambda i, f, be, nb: (0,)),
                  smem(lambda i, f, be, nb: (jnp.minimum(i + 1, nb[0] - 1),)),
                  smem(lambda i, f, be, nb: (jnp.clip(i - 1, 0, nb[0] - 1),)),
                  pl.BlockSpec(memory_space=pl.ANY),
                  pl.BlockSpec((None, d, tf), wg_map),
                  pl.BlockSpec((None, d, tf), wg_map),
                  pl.BlockSpec((None, tf, d), wd_map)],
        out_specs=pl.BlockSpec(memory_space=pl.ANY),
        scratch_shapes=[pltpu.VMEM((tm * pieces + 8, LANES), F32), pltpu.VMEM((tm * pieces + 8, LANES), F32),
                        pltpu.VMEM((tm, d), F32), pltpu.VMEM((tm, d), BF16),
                        pltpu.SemaphoreType.DMA(()), pltpu.SemaphoreType.DMA(())],
    )
    kern = functools.partial(_experts_kernel, rps=rps, nf=nf, n_tok=n, second=second)
    return pl.pallas_call(
        kern,
        grid_spec=grid_spec,
        out_shape=jax.ShapeDtypeStruct((2 * second * pieces, LANES), F32),
        compiler_params=_params("arbitrary", "arbitrary"),
        name="moe_experts",
    )(blk_e, nb_used, src, src, slot, h, w_gate, w_up, w_down)


def _combine_kernel(x_ref, route_ref, g_ref, ya_ref, yb_ref, o_ref):
    route = route_ref[...]
    lane = lax.broadcasted_iota(I32, route.shape, 1)
    gates = lax.bitcast_convert_type(route, F32)
    g1 = jnp.sum(jnp.where(lane == ROUTE_GATE, gates, 0.0), axis=-1, keepdims=True)
    g2 = jnp.sum(jnp.where(lane == ROUTE_GATE + 1, gates, 0.0), axis=-1, keepdims=True)
    tm, d = x_ref.shape
    pieces = d // LANES
    x4 = []
    for j in range(pieces):
        ya = _row_tile_piece(ya_ref, j, tm, pieces)
        yb = _row_tile_piece(yb_ref, j, tm, pieces)
        x4.append(x_ref[:, j * LANES:(j + 1) * LANES] + (ya * g1 + yb * g2))
    ms = sum(jnp.sum(p * p, axis=-1, keepdims=True) for p in x4) * (1.0 / d)
    inv = lax.rsqrt(ms + EPS)
    for j in range(pieces):
        cols = slice(j * LANES, (j + 1) * LANES)
        o_ref[:, cols] = x4[j] * inv * g_ref[:, cols]


def _combine(x, route, y, g, second, tm=512):
    n, d = x.shape
    off = second // tm
    pieces = d // LANES
    return pl.pallas_call(
        _combine_kernel,
        grid=(n // tm,),
        in_specs=[pl.BlockSpec((tm, d), lambda i: (i, 0)),
                  pl.BlockSpec((tm, LANES), lambda i: (i, 0)),
                  pl.BlockSpec((1, d), lambda i: (0, 0)),
                  pl.BlockSpec((tm * pieces, LANES), lambda i: (i, 0)),
                  pl.BlockSpec((tm * pieces, LANES), lambda i: (i + off, 0))],
        out_specs=pl.BlockSpec((tm, d), lambda i: (i, 0)),
        out_shape=jax.ShapeDtypeStruct((n, d), F32),
        compiler_params=_params("parallel"),
        name="moe_combine",
    )(x, route, g.reshape(1, d), y, y)


def _moe_layout(route, counts, n, tm):
    e = route[:, ROUTE_E:ROUTE_E + 2]
    rank = route[:, ROUTE_RANK:ROUTE_RANK + 2]
    sizes = counts[0, :N_EXPERTS]
    padded = ((sizes + tm - 1) // tm) * tm
    pend = jnp.cumsum(padded)
    pstart = pend - padded
    start_of = jnp.zeros_like(e)
    for j in range(N_EXPERTS):
        start_of = jnp.where(e == j, pstart[j], start_of)
    dest = (start_of + rank).astype(I32)
    n_blocks = (2 * n) // tm + N_EXPERTS
    rows = n_blocks * tm
    blk_e = jnp.minimum(jnp.searchsorted(pend, jnp.arange(n_blocks, dtype=I32) * tm, side='right'),
                        N_EXPERTS - 1).astype(I32)
    nb_used = (pend[-1:] // tm).astype(I32)
    second = n + N_EXPERTS * tm
    pair_slot = jnp.arange(n, dtype=I32)[:, None] + jnp.array([0, second], I32)[None, :]
    row = jnp.arange(rows, dtype=I32)
    row_e = jnp.repeat(blk_e, tm)
    pad_j = jnp.clip(row - (pstart + sizes)[row_e], 0, tm - 1)
    spare = n + row_e * tm + pad_j
    slot = spare.at[dest.reshape(-1)].set(pair_slot.reshape(-1))
    token = slot % second
    src = jnp.where(token < n, token, 0)
    return src, slot, blk_e, nb_used, second


def kernel(x, positions, g_mix, g_ffn, g_final, w_in_attn, w_out_attn, attn_sinks, w_in_rec,
           rec_lower_bounds, rec_norm_g, w_out_rec, w_gate_dense, w_up_dense, w_down_dense,
           w_router, w_gate_moe, w_up_moe, w_down_moe):
    bsz, seq, d = x.shape
    n = bsz * seq
    x0 = x.reshape(n, d)
    bf = lambda w: w.astype(BF16)

    cos_t, sin_t = _rope_tables(positions)
    qkv1, qkv4, qkv16, qkvb = _attn_inproj(x, g_mix[0], bf(w_in_attn[0]), cos_t, sin_t)
    mix_a = _dilated_attention(qkv1, qkv4, qkv16)
    mix_b = _swa_gqa(qkvb, attn_sinks[0])
    a_w = A_HEADS * HEAD_DIM
    w_out = bf(w_out_attn[0])
    x2 = _outproj_ffn(x0, mix_a.reshape(n, -1), mix_b.reshape(n, -1), w_out[:a_w], w_out[a_w:],
                      g_ffn[0], bf(w_gate_dense[0]), bf(w_up_dense[0]), bf(w_down_dense[0]))

    lb = jax.nn.softmax(rec_lower_bounds.astype(F32), axis=0)
    lb1 = (jnp.cumsum(lb, axis=0) - lb[0])[1]
    kw = lb1.shape[0]
    vw = rec_norm_g.shape[1]
    q, k, lf, v, gate = _rec_inproj(x2, g_mix[1], bf(w_in_rec[0]), lb1, kw, vw)
    rec = _hgrn(q, k, lf, v, gate, rec_norm_g[0], bsz, seq)
    x3, h, route, counts = _router(x2, rec, bf(w_out_rec[0]), g_ffn[1], w_router[0])
    tm_moe = min(1024, n)
    src, slot, blk_e, nb_used, second = _moe_layout(route, counts, n, tm_moe)
    ys = _experts(h, src, slot, blk_e, nb_used, bf(w_gate_moe[0]), bf(w_up_moe[0]),
                  bf(w_down_moe[0]), tm_moe, second)
    out = _combine(x3, route, ys, g_final, second)
    return out.reshape(bsz, seq, d)
```

```python
import functools

import jax
import jax.numpy as jnp
from jax import lax
from jax.experimental import pallas as pl
from jax.experimental.pallas import tpu as pltpu

F32 = jnp.float32
BF16 = jnp.bfloat16
I32 = jnp.int32

EPS = 1e-6
HEAD_DIM = 64
ROT_DIM = HEAD_DIM // 4
ROPE_THETA = 500000.0
LANES = 128
ATTN_BLOCK = 128
ATTN_UNROLL = 3
A_HEADS = 8
A_PATTERNS = ((128, 1), (512, 4), (2048, 16))
B_Q_HEADS = 8
B_KV_HEADS = 2
B_WINDOW = 128
C_HEADS = 8
N_EXPERTS = 8
REC_CHUNK = 128
REC_SUB = 16
REC_HEADS = 2
REC_SAFE_DROP = -80.0
VMEM_LIMIT = 56 * 1024 * 1024


def _params(*sem):
    return pltpu.CompilerParams(dimension_semantics=sem, vmem_limit_bytes=VMEM_LIMIT)


def _rms(x, g):
    return x * lax.rsqrt(jnp.mean(x * x, axis=-1, keepdims=True) + EPS) * g


def _silu(x):
    return x / (1.0 + jnp.exp(-x))


def _dot(a, b):
    return jnp.dot(a, b, preferred_element_type=F32)


def _dot_nt(a, b):
    return lax.dot_general(a, b, (((1,), (1,)), ((), ())), preferred_element_type=F32)


def _dot_tn(a, b):
    return lax.dot_general(a, b, (((0,), (0,)), ((), ())), preferred_element_type=F32)


def _rows(j, size, count=1):
    if isinstance(j, int):
        return slice(j * size, (j + count) * size)
    return pl.ds(pl.multiple_of(j * size, size), count * size)


def _to_row_tiles(ref, x):
    pieces = x.shape[1] // LANES
    for j in range(pieces):
        ref[pl.ds(j, x.shape[0], stride=pieces), :] = x[:, j * LANES:(j + 1) * LANES]


def _row_tile_piece(ref, j, m, pieces):
    return ref[pl.ds(j, m, stride=pieces), :]


def _rope_table_kernel(pos_ref, invf_ref, cos_ref, sin_ref):
    ang = pos_ref[...].astype(F32) * invf_ref[...]
    cos_ref[...] = jnp.cos(ang)
    sin_ref[...] = jnp.sin(ang)


def _rope_tables(positions):
    n = positions.size
    half = ROT_DIM // 2
    per_row = LANES // half
    inv_freq = jnp.power(ROPE_THETA, -jnp.arange(0, ROT_DIM, 2, dtype=F32) / ROT_DIM)
    pos_rep = jnp.repeat(positions.reshape(n // per_row, per_row), half, axis=1)
    invf_row = jnp.tile(inv_freq, per_row).reshape(1, LANES)
    rows = n // per_row
    tr = min(rows, 512)
    cos8, sin8 = pl.pallas_call(
        _rope_table_kernel,
        grid=(rows // tr,),
        in_specs=[pl.BlockSpec((tr, LANES), lambda i: (i, 0)),
                  pl.BlockSpec((1, LANES), lambda i: (0, 0))],
        out_specs=[pl.BlockSpec((tr, LANES), lambda i: (i, 0))] * 2,
        out_shape=[jax.ShapeDtypeStruct((rows, LANES), F32)] * 2,
        compiler_params=_params("parallel"),
        name="rope_tables",
    )(pos_rep, invf_row)
    cos8 = cos8.reshape(n, half)
    sin8 = sin8.reshape(n, half)
    rest = HEAD_DIM - ROT_DIM
    c64 = jnp.concatenate([cos8, cos8, jnp.ones((n, rest), F32)], axis=1)
    s64 = jnp.concatenate([-sin8, sin8, jnp.zeros((n, rest), F32)], axis=1)
    return jnp.tile(c64, (1, 2)), jnp.tile(s64, (1, 2))


A_BLOCKS = A_HEADS * HEAD_DIM // LANES
BQ_BLOCKS = B_Q_HEADS * HEAD_DIM // LANES
A_QKV = 3 * A_BLOCKS * LANES
B_QKV = (BQ_BLOCKS + 2 * B_KV_HEADS) * LANES


def _attn_inproj_kernel(x_ref, g_ref, w_ref, c_ref, s_ref, o1_ref, o4_ref, o16_ref, ob_ref, sc_ref):
    h = _rms(x_ref[0], g_ref[...]).astype(BF16)
    acc = _dot(h, w_ref[...])
    tm = acc.shape[0]
    c = c_ref[0]
    s = s_ref[0]
    lane = lax.broadcasted_iota(I32, c.shape, 1)
    first = (lane % HEAD_DIM) < (ROT_DIM // 2)
    lo_half = lane < HEAD_DIM

    def rope(blk):
        up = pltpu.roll(blk, LANES - ROT_DIM // 2, 1)
        dn = pltpu.roll(blk, ROT_DIM // 2, 1)
        return blk * c + jnp.where(first, up, dn) * s

    def col(cb):
        return acc[:, cb * LANES:(cb + 1) * LANES]

    scale = HEAD_DIM ** -0.5
    for cb in range(3 * A_BLOCKS):
        blk = col(cb)
        if cb < A_BLOCKS:
            blk = rope(blk) * scale
        elif cb < 2 * A_BLOCKS:
            blk = rope(blk)
        sc_ref[cb] = blk
        o1_ref[0, :, cb * LANES:(cb + 1) * LANES] = blk.astype(BF16)
    for o_ref in (o4_ref, o16_ref):
        dil = o_ref.shape[1]
        for r in range(dil):
            for cb in range(3 * A_BLOCKS):
                o_ref[0, r, :, cb * LANES:(cb + 1) * LANES] = (
                    sc_ref[cb, pl.ds(r, tm // dil, stride=dil), :].astype(BF16))
    base = 3 * A_BLOCKS
    for j in range(BQ_BLOCKS):
        ob_ref[0, :, j * LANES:(j + 1) * LANES] = (rope(col(base + j)) * scale).astype(BF16)
    for j, blk in enumerate((rope(col(base + BQ_BLOCKS)), col(base + BQ_BLOCKS + 1))):
        swapped = pltpu.roll(blk, HEAD_DIM, 1)
        for g, dup in enumerate((jnp.where(lo_half, blk, swapped), jnp.where(lo_half, swapped, blk))):
            cb = BQ_BLOCKS + j * B_KV_HEADS + g
            ob_ref[0, :, cb * LANES:(cb + 1) * LANES] = dup.astype(BF16)


def _attn_inproj(x, g, w, cos_t, sin_t, tm=512):
    bsz, seq, d = x.shape
    width = w.shape[1]
    assert B_KV_HEADS * HEAD_DIM == LANES and width == A_QKV + (BQ_BLOCKS + 2) * LANES
    tile = lambda b, i: (b, i, 0)
    fix = lambda b, i: (0, 0)
    d4, d16 = A_PATTERNS[1][1], A_PATTERNS[2][1]
    perm = lambda dil: pl.BlockSpec((1, dil, tm // dil, A_QKV), lambda b, i: (b, 0, i, 0))
    return pl.pallas_call(
        _attn_inproj_kernel,
        grid=(bsz, seq // tm),
        in_specs=[pl.BlockSpec((1, tm, d), tile),
                  pl.BlockSpec((1, d), fix),
                  pl.BlockSpec((d, width), fix),
                  pl.BlockSpec((1, tm, LANES), tile),
                  pl.BlockSpec((1, tm, LANES), tile)],
        out_specs=[pl.BlockSpec((1, tm, A_QKV), tile), perm(d4), perm(d16),
                   pl.BlockSpec((1, tm, B_QKV), tile)],
        out_shape=[jax.ShapeDtypeStruct((bsz, seq, A_QKV), BF16),
                   jax.ShapeDtypeStruct((bsz, d4, seq // d4, A_QKV), BF16),
                   jax.ShapeDtypeStruct((bsz, d16, seq // d16, A_QKV), BF16),
                   jax.ShapeDtypeStruct((bsz, seq, B_QKV), BF16)],
        scratch_shapes=[pltpu.VMEM((3 * A_BLOCKS, tm, LANES), F32)],
        compiler_params=_params("parallel", "parallel"),
        name="attn_inproj",
    )(x, g.reshape(1, d), w, cos_t.reshape(bsz, seq, LANES), sin_t.reshape(bsz, seq, LANES))


def _band_mask(n_back):
    qi = lax.broadcasted_iota(I32, (ATTN_BLOCK, 2 * ATTN_BLOCK), 0)
    kj = lax.broadcasted_iota(I32, (ATTN_BLOCK, 2 * ATTN_BLOCK), 1)
    dist = ATTN_BLOCK + qi - kj
    return (dist >= 0) & (dist <= n_back), kj >= ATTN_BLOCK


def _band_block(q_pairs, kk, vv, valid, sinks=None, want_lse=True):
    lane = lax.broadcasted_iota(I32, (ATTN_BLOCK, LANES), 1)
    lo_half = lane < HEAD_DIM
    zero = jnp.zeros((ATTN_BLOCK, LANES), BF16)
    lhs = []
    for q in q_pairs:
        lhs += [jnp.where(lo_half, q, zero), jnp.where(lo_half, zero, q)]
    s_all = _dot_nt(jnp.concatenate(lhs, axis=0), kk)
    ps, inv_l, lses = [], [], []
    for u in range(len(lhs)):
        s = jnp.where(valid, s_all[u * ATTN_BLOCK:(u + 1) * ATTN_BLOCK], -jnp.inf)
        m = jnp.max(s, axis=-1, keepdims=True)
        if sinks is not None:
            m = jnp.maximum(m, sinks[u])
        e = jnp.exp(s - m)
        l = jnp.sum(e, axis=-1, keepdims=True)
        if sinks is not None:
            l = l + jnp.exp(sinks[u] - m)
        ps.append(e.astype(BF16))
        inv_l.append(1.0 / l)
        lses.append(m + jnp.log(l) if want_lse else None)
    o_all = _dot(jnp.concatenate(ps, axis=0), vv)
    outs = []
    for j in range(len(q_pairs)):
        o0 = o_all[(2 * j) * ATTN_BLOCK:(2 * j + 1) * ATTN_BLOCK] * inv_l[2 * j]
        o1 = o_all[(2 * j + 1) * ATTN_BLOCK:(2 * j + 2) * ATTN_BLOCK] * inv_l[2 * j + 1]
        lse = jnp.where(lo_half, lses[2 * j], lses[2 * j + 1]) if want_lse else None
        outs.append((jnp.where(lo_half, o0, o1), lse))
    return outs


def _for_each_block(nq, first_fn, rest_fn):
    first_fn()
    rest = nq - 1
    if rest <= ATTN_UNROLL:
        for jb in range(1, nq):
            rest_fn(jb)
        return
    assert rest % ATTN_UNROLL == 0

    def body(it, carry):
        for u in range(ATTN_UNROLL):
            rest_fn(1 + it * ATTN_UNROLL + u)
        return carry

    lax.fori_loop(0, rest // ATTN_UNROLL, body, 0)


def _dilated_kernel(q1, kp1, kc1, vp1, vc1, q4, kp4, kc4, vp4, vc4, q16, kp16, kc16, vp16, vc16,
                    o_ref, o4_s, l4_s, o16_s, l16_s, *, n_backs):
    tile = o_ref.shape[1]
    not_first = pl.program_id(2) > 0

    def masks(n_back):
        band, in_cur = _band_mask(n_back)
        return band & (in_cur | not_first), band

    def halo(p_ref, c_ref, idx):
        return jnp.concatenate([p_ref[idx], c_ref[idx + (slice(0, ATTN_BLOCK),)]], axis=0)

    def window(c_ref, idx, jb):
        return c_ref[idx + (_rows(jb - 1, ATTN_BLOCK, 2),)]

    def rows(jb):
        return _rows(jb, ATTN_BLOCK)

    for (q, kp, kc, vp, vc, o_s, l_s), n_back in zip(
            ((q16, kp16, kc16, vp16, vc16, o16_s, l16_s), (q4, kp4, kc4, vp4, vc4, o4_s, l4_s)),
            (n_backs[2], n_backs[1])):
        dil = q.shape[1]
        nq = q.shape[2] // ATTN_BLOCK
        valid0, valid = masks(n_back)
        for r in range(dil):
            idx = (0, r)

            def put(jb, res, r=r, dil=dil, o_s=o_s, l_s=l_s):
                (o, lse), = res
                dst = pl.ds(jb * ATTN_BLOCK * dil + r, ATTN_BLOCK, stride=dil)
                o_s[dst, :] = o
                l_s[dst, :] = lse

            def first(q=q, kp=kp, kc=kc, vp=vp, vc=vc, idx=idx, put=put, valid0=valid0):
                put(0, _band_block([q[idx + (slice(0, ATTN_BLOCK),)]], halo(kp, kc, idx),
                                   halo(vp, vc, idx), valid0))

            def rest(jb, q=q, kc=kc, vc=vc, idx=idx, put=put, valid=valid):
                put(jb, _band_block([q[idx + (rows(jb),)]], window(kc, idx, jb),
                                    window(vc, idx, jb), valid))

            _for_each_block(nq, first, rest)

    valid0, valid = masks(n_backs[0])
    idx = (0,)

    def merge(jb, res):
        (o, lse), = res
        dst = rows(jb)
        l4 = l4_s[dst, :]
        l16 = l16_s[dst, :]
        mx = jnp.maximum(jnp.maximum(lse, l4), l16)
        w1 = jnp.exp(lse - mx)
        w4 = jnp.exp(l4 - mx)
        w16 = jnp.exp(l16 - mx)
        num = w1 * o + w4 * o4_s[dst, :] + w16 * o16_s[dst, :]
        o_ref[0, dst, :] = (num / (w1 + w4 + w16)).astype(o_ref.dtype)

    _for_each_block(
        tile // ATTN_BLOCK,
        lambda: merge(0, _band_block([q1[0, 0:ATTN_BLOCK]], halo(kp1, kc1, idx), halo(vp1, vc1, idx), valid0)),
        lambda jb: merge(jb, _band_block([q1[0, rows(jb)]], window(kc1, idx, jb), window(vc1, idx, jb), valid)))


def _dilated_attention(qkv1, qkv4, qkv16):
    bsz, seq, _ = qkv1.shape
    dils = tuple(p[1] for p in A_PATTERNS)
    assert dils[0] == 1 and qkv4.shape[1] == dils[1] and qkv16.shape[1] == dils[2]
    tile = dils[2] * ATTN_BLOCK
    nt = seq // tile

    def specs(dil):
        rows = tile // dil
        nb = rows // ATTN_BLOCK
        if dil == 1:
            cur = lambda off: pl.BlockSpec((1, rows, LANES), lambda b, hp, i: (b, i, off + hp))
            prev = lambda off: pl.BlockSpec(
                (1, ATTN_BLOCK, LANES), lambda b, hp, i: (b, jnp.maximum(i * nb - 1, 0), off + hp))
        else:
            cur = lambda off: pl.BlockSpec((1, dil, rows, LANES), lambda b, hp, i: (b, 0, i, off + hp))
            prev = lambda off: pl.BlockSpec(
                (1, dil, ATTN_BLOCK, LANES), lambda b, hp, i: (b, 0, jnp.maximum(i * nb - 1, 0), off + hp))
        return [cur(0), prev(A_BLOCKS), cur(A_BLOCKS), prev(2 * A_BLOCKS), cur(2 * A_BLOCKS)]

    kern = functools.partial(_dilated_kernel, n_backs=tuple(w // d for w, d in A_PATTERNS))
    return pl.pallas_call(
        kern,
        grid=(bsz, A_BLOCKS, nt),
        in_specs=specs(1) + specs(dils[1]) + specs(dils[2]),
        out_specs=pl.BlockSpec((1, tile, LANES), lambda b, hp, i: (b, i, hp)),
        out_shape=jax.ShapeDtypeStruct((bsz, seq, A_BLOCKS * LANES), BF16),
        scratch_shapes=[pltpu.VMEM((tile, LANES), F32)] * 4,
        compiler_params=_params("parallel", "parallel", "arbitrary"),
        name="dilated_attn",
    )(*([qkv1] * 5 + [qkv4] * 5 + [qkv16] * 5))


def _swa_kernel(sink_ref, q_ref, *rest, n_back, nq):
    kv = rest[:4 * B_KV_HEADS]
    o_ref = rest[4 * B_KV_HEADS]
    not_first = pl.program_id(1) > 0
    band, in_cur = _band_mask(n_back)
    valid0 = band & (in_cur | not_first)
    pairs = BQ_BLOCKS // B_KV_HEADS

    def run(q_rows, kv_of, valid):
        for g in range(B_KV_HEADS):
            cbs = [g * pairs + j for j in range(pairs)]
            qs = [q_ref[0, q_rows, cb * LANES:(cb + 1) * LANES] for cb in cbs]
            sinks = [sink_ref[2 * cb + p] for cb in cbs for p in range(2)]
            kk, vv = kv_of(g)
            res = _band_block(qs, kk, vv, valid, sinks, want_lse=False)
            for cb, (o, _) in zip(cbs, res):
                o_ref[0, q_rows, cb * LANES:(cb + 1) * LANES] = o.astype(o_ref.dtype)

    def halo_kv(g):
        kp, kc, vp, vc = kv[4 * g:4 * g + 4]
        return (jnp.concatenate([kp[0], kc[0, 0:ATTN_BLOCK]], axis=0),
                jnp.concatenate([vp[0], vc[0, 0:ATTN_BLOCK]], axis=0))

    def window_kv(jb):
        win = _rows(jb - 1, ATTN_BLOCK, 2)
        return lambda g: (kv[4 * g + 1][0, win], kv[4 * g + 3][0, win])

    _for_each_block(
        nq,
        lambda: run(_rows(0, ATTN_BLOCK), halo_kv, valid0),
        lambda jb: run(_rows(jb, ATTN_BLOCK), window_kv(jb), band))


def _swa_gqa(qkvb, sinks, tq=512):
    bsz, seq, _ = qkvb.shape
    tq = min(seq, tq)
    nq = tq // ATTN_BLOCK
    bq_w = BQ_BLOCKS * LANES
    in_specs = [pl.BlockSpec(memory_space=pltpu.SMEM),
                pl.BlockSpec((1, tq, bq_w), lambda b, i: (b, i, 0))]
    for g in range(B_KV_HEADS):
        for section in range(2):
            cb = BQ_BLOCKS + section * B_KV_HEADS + g
            in_specs += [pl.BlockSpec((1, ATTN_BLOCK, LANES),
                                      lambda b, i, cb=cb: (b, jnp.maximum(i * nq - 1, 0), cb)),
                         pl.BlockSpec((1, tq, LANES), lambda b, i, cb=cb: (b, i, cb))]
    args = [sinks.astype(F32)] + [qkvb] * (1 + 4 * B_KV_HEADS)
    kern = functools.partial(_swa_kernel, n_back=B_WINDOW - 1, nq=nq)
    return pl.pallas_call(
        kern,
        grid=(bsz, seq // tq),
        in_specs=in_specs,
        out_specs=pl.BlockSpec((1, tq, bq_w), lambda b, i: (b, i, 0)),
        out_shape=jax.ShapeDtypeStruct((bsz, seq, bq_w), BF16),
        compiler_params=_params("parallel", "arbitrary"),
        name="swa_gqa",
    )(*args)


def _outproj_ffn_kernel(x_ref, a_ref, b_ref, wa_ref, wb_ref, g_ref, wg_ref, wu_ref, wd_ref,
                        o_ref, h_ref):
    f = pl.program_id(1)

    @pl.when(f == 0)
    def _():
        x1 = x_ref[...] + _dot(a_ref[...], wa_ref[...]) + _dot(b_ref[...], wb_ref[...])
        o_ref[...] = x1
        h_ref[...] = _rms(x1, g_ref[...]).astype(BF16)

    h = h_ref[...]
    act = _silu(_dot(h, wg_ref[...])) * _dot(h, wu_ref[...])
    o_ref[...] += _dot(act.astype(BF16), wd_ref[...])


def _outproj_ffn(x, mix_a, mix_b, w_a, w_b, g, w_gate, w_up, w_down, tm=1024, tf=512):
    n, d = x.shape
    ff = w_gate.shape[1]
    ka, kb = mix_a.shape[1], mix_b.shape[1]
    return pl.pallas_call(
        _outproj_ffn_kernel,
        grid=(n // tm, ff // tf),
        in_specs=[pl.BlockSpec((tm, d), lambda i, f: (i, 0)),
                  pl.BlockSpec((tm, ka), lambda i, f: (i, 0)),
                  pl.BlockSpec((tm, kb), lambda i, f: (i, 0)),
                  pl.BlockSpec((ka, d), lambda i, f: (0, 0)),
                  pl.BlockSpec((kb, d), lambda i, f: (0, 0)),
                  pl.BlockSpec((1, d), lambda i, f: (0, 0)),
                  pl.BlockSpec((d, tf), lambda i, f: (0, f)),
                  pl.BlockSpec((d, tf), lambda i, f: (0, f)),
                  pl.BlockSpec((tf, d), lambda i, f: (f, 0))],
        out_specs=pl.BlockSpec((tm, d), lambda i, f: (i, 0)),
        out_shape=jax.ShapeDtypeStruct((n, d), F32),
        scratch_shapes=[pltpu.VMEM((tm, d), BF16)],
        compiler_params=_params("parallel", "arbitrary"),
        name="outproj_ffn",
    )(x, mix_a, mix_b, w_a, w_b, g.reshape(1, d), w_gate, w_up, w_down)


def _rec_inproj_kernel(x_ref, g_ref, w_ref, lb_ref, q_ref, k_ref, lf_ref, v_ref, gate_ref):
    h = _rms(x_ref[...], g_ref[...]).astype(BF16)
    acc = _dot(h, w_ref[...])
    kw = q_ref.shape[1]
    vw = v_ref.shape[1]
    lb = lb_ref[...]
    q_ref[...] = _silu(acc[:, :kw]).astype(BF16)
    fg = lb + (1.0 - lb) / (1.0 + jnp.exp(-acc[:, kw:2 * kw]))
    k_ref[...] = (1.0 - fg).astype(BF16)
    lf_ref[...] = jnp.log(fg)
    v_ref[...] = acc[:, 2 * kw:2 * kw + vw].astype(BF16)
    gate_ref[...] = _silu(acc[:, 2 * kw + vw:]).astype(BF16)


def _rec_inproj(x, g, w, lb, kw, vw, tm=512):
    n, d = x.shape
    width = w.shape[1]
    row = lambda i: (i, 0)
    fix = lambda i: (0, 0)
    return pl.pallas_call(
        _rec_inproj_kernel,
        grid=(n // tm,),
        in_specs=[pl.BlockSpec((tm, d), row),
                  pl.BlockSpec((1, d), fix),
                  pl.BlockSpec((d, width), fix),
                  pl.BlockSpec((1, kw), fix)],
        out_specs=[pl.BlockSpec((tm, kw), row), pl.BlockSpec((tm, kw), row),
                   pl.BlockSpec((tm, kw), row), pl.BlockSpec((tm, vw), row),
                   pl.BlockSpec((tm, vw), row)],
        out_shape=[jax.ShapeDtypeStruct((n, kw), BF16), jax.ShapeDtypeStruct((n, kw), BF16),
                   jax.ShapeDtypeStruct((n, kw), F32), jax.ShapeDtypeStruct((n, vw), BF16),
                   jax.ShapeDtypeStruct((n, vw), BF16)],
        compiler_params=_params("parallel"),
        name="rec_inproj",
    )(x, g.reshape(1, d), w, lb.reshape(1, kw))


def _hgrn_kernel(q_ref, k_ref, lf_ref, v_ref, gate_ref, ng_ref, o_ref, st_ref, b_ref, kf_ref, *,
                 n_chunks):
    @pl.when(pl.program_id(2) == 0)
    def _():
        st_ref[...] = jnp.zeros_like(st_ref)

    c_len, sub = REC_CHUNK, REC_SUB
    heads = [slice(hh * LANES, (hh + 1) * LANES) for hh in range(q_ref.shape[2] // LANES)]
    r_i = lax.broadcasted_iota(I32, (c_len, c_len), 0)
    c_i = lax.broadcasted_iota(I32, (c_len, c_len), 1)
    causal = c_i <= r_i
    tril = causal.astype(BF16)

    drop = jnp.zeros((1, LANES), F32)
    for c in range(n_chunks):
        rows = slice(c * c_len, (c + 1) * c_len)
        for hh, cols in enumerate(heads):
            lf = lf_ref[0, rows, cols]
            lf1 = lf.astype(BF16)
            rem = lf - lf1.astype(F32)
            lf2 = rem.astype(BF16)
            lf3 = (rem - lf2.astype(F32)).astype(BF16)
            b = _dot(tril, lf1) + _dot(tril, lf2) + _dot(tril, lf3)
            b_ref[hh, rows, :] = b
            for i in range(c_len // sub):
                end = b[(i + 1) * sub - 1:(i + 1) * sub, :]
                drop = jnp.minimum(drop, end - b[i * sub - 1:i * sub, :] if i > 0 else end)
    safe = jnp.min(drop) > REC_SAFE_DROP

    def finish(c, hh, att, st):
        rows = _rows(c, c_len)
        cols = heads[hh]
        b = b_ref[hh, rows, :]
        q = q_ref[0, rows, cols].astype(F32)
        k = k_ref[0, rows, cols].astype(F32)
        v = v_ref[0, rows, cols]
        b_last = b[c_len - 1:c_len, :]
        qe = (q * jnp.exp(b)).astype(BF16)
        o = _dot(att.astype(BF16), v) + _dot_nt(qe, st.astype(BF16))
        kd = (k * jnp.exp(b_last - b)).astype(BF16)
        st = st * jnp.exp(b_last) + _dot_tn(v, kd)
        y = o * lax.rsqrt(jnp.mean(o * o, axis=-1, keepdims=True) + EPS)
        o_ref[0, rows, cols] = (y * ng_ref[:, cols] * gate_ref[0, rows, cols].astype(F32)).astype(o_ref.dtype)
        return st

    @pl.when(safe)
    def _():
        sts = [st_ref[hh] for hh in range(len(heads))]
        for c in range(n_chunks):
            rows = slice(c * c_len, (c + 1) * c_len)
            for hh, cols in enumerate(heads):
                b = b_ref[hh, rows, :]
                q = q_ref[0, rows, cols].astype(F32)
                k = k_ref[0, rows, cols].astype(F32)
                att_rows = []
                for i in range(c_len // sub):
                    lo, hi = i * sub, (i + 1) * sub
                    ref = b[lo - 1:lo, :] if i > 0 else jnp.zeros((1, LANES), F32)
                    qt = (q[lo:hi] * jnp.exp(b[lo:hi] - ref)).astype(BF16)
                    kt = (k[:hi] * jnp.exp(ref - b[:hi])).astype(BF16)
                    if hi < c_len:
                        kt = jnp.concatenate([kt, jnp.zeros((c_len - hi, LANES), BF16)], axis=0)
                    att_rows.append(_dot_nt(qt, kt))
                att = jnp.where(causal, jnp.concatenate(att_rows, axis=0), 0.0)
                sts[hh] = finish(c, hh, att, sts[hh])
        for hh in range(len(heads)):
            st_ref[hh] = sts[hh]

    @pl.when(jnp.logical_not(safe))
    def _():
        for hh, cols in enumerate(heads):
            def chunk(c, st, hh=hh, cols=cols):
                rows = _rows(c, c_len)
                b = b_ref[hh, rows, :]
                q = q_ref[0, rows, cols].astype(F32)
                kf_ref[...] = k_ref[0, rows, cols].astype(F32)

                def key(s, att):
                    b_s = b_ref[hh, pl.ds(c * c_len + s, 1), :]
                    dec = jnp.exp(jnp.minimum(b - b_s, 0.0))
                    col = jnp.sum(q * kf_ref[pl.ds(s, 1), :] * dec, axis=-1, keepdims=True)
                    return jnp.where(c_i == s, col, att)

                att = lax.fori_loop(0, c_len, key, jnp.zeros((c_len, c_len), F32))
                return finish(c, hh, jnp.where(causal, att, 0.0), st)

            st_ref[hh] = lax.fori_loop(0, n_chunks, chunk, st_ref[hh])


def _hgrn(q, k, lf, v, gate, norm_g, bsz, seq, ts=512):
    kw = q.shape[-1] // C_HEADS
    vw = v.shape[-1] // C_HEADS
    assert kw == LANES and vw == LANES
    ts = min(ts, seq)
    v3 = lambda t: t.reshape(bsz, seq, t.shape[-1])
    wide = REC_HEADS * LANES
    blk = pl.BlockSpec((1, ts, wide), lambda b, h, c: (b, c, h))
    kern = functools.partial(_hgrn_kernel, n_chunks=ts // REC_CHUNK)
    out = pl.pallas_call(
        kern,
        grid=(bsz, C_HEADS // REC_HEADS, seq // ts),
        in_specs=[blk, blk, blk, blk, blk, pl.BlockSpec((1, wide), lambda b, h, c: (0, h))],
        out_specs=blk,
        out_shape=jax.ShapeDtypeStruct((bsz, seq, C_HEADS * vw), BF16),
        scratch_shapes=[pltpu.VMEM((REC_HEADS, vw, kw), F32), pltpu.VMEM((REC_HEADS, ts, LANES), F32),
                        pltpu.VMEM((REC_CHUNK, LANES), F32)],
        compiler_params=_params("parallel", "parallel", "arbitrary"),
        name="hgrn2",
    )(v3(q), v3(k), v3(lf), v3(v), v3(gate), norm_g.reshape(1, -1))
    return out.reshape(bsz * seq, C_HEADS * vw)


ROUTE_E, ROUTE_RANK, ROUTE_GATE = 0, 2, 4


def _router_kernel(x_ref, a_ref, wo_ref, g_ref, wr_hi_ref, wr_lo_ref, x3_ref, h_ref, route_ref,
                   cnt_ref, base_ref):
    @pl.when(pl.program_id(0) == 0)
    def _():
        base_ref[...] = jnp.zeros_like(base_ref)

    x3 = x_ref[...] + _dot(a_ref[...], wo_ref[...])
    x3_ref[...] = x3
    h = _rms(x3, g_ref[...])
    _to_row_tiles(h_ref, h)
    h_hi = h.astype(BF16)
    h_lo = (h - h_hi.astype(F32)).astype(BF16)
    logits = _dot(h_hi, wr_hi_ref[...]) + _dot(h_hi, wr_lo_ref[...]) + _dot(h_lo, wr_hi_ref[...])
    tm = logits.shape[0]
    lane = lax.broadcasted_iota(I32, logits.shape, 1)
    logits = jnp.where(lane < N_EXPERTS, logits, -jnp.inf)
    lane_f = lane.astype(F32)
    v1 = jnp.max(logits, axis=-1, keepdims=True)
    e1 = jnp.min(jnp.where(logits == v1, lane_f, float(LANES)), axis=-1, keepdims=True)
    hot1 = lane_f == e1
    rest = jnp.where(hot1, -jnp.inf, logits)
    v2 = jnp.max(rest, axis=-1, keepdims=True)
    e2 = jnp.min(jnp.where(rest == v2, lane_f, float(LANES)), axis=-1, keepdims=True)
    hot2 = lane_f == e2
    t = jnp.exp(v2 - v1)
    g1 = 1.0 / (1.0 + t)
    g2 = t / (1.0 + t)
    e1 = e1.astype(I32)
    e2 = e2.astype(I32)
    member = (hot1 | hot2).astype(BF16)
    r_i = lax.broadcasted_iota(I32, (tm, tm), 0)
    c_i = lax.broadcasted_iota(I32, (tm, tm), 1)
    before = _dot((c_i < r_i).astype(BF16), member) + base_ref[...]
    rank1 = jnp.sum(jnp.where(hot1, before, 0.0), axis=-1, keepdims=True).astype(I32)
    rank2 = jnp.sum(jnp.where(hot2, before, 0.0), axis=-1, keepdims=True).astype(I32)
    base = base_ref[...] + jnp.sum(member.astype(F32), axis=0, keepdims=True)
    base_ref[...] = base
    cnt_ref[...] = base.astype(I32)
    route = jnp.where(lane == ROUTE_E, e1, 0)
    route = jnp.where(lane == ROUTE_E + 1, e2, route)
    route = jnp.where(lane == ROUTE_RANK, rank1, route)
    route = jnp.where(lane == ROUTE_RANK + 1, rank2, route)
    gate_bits = lax.bitcast_convert_type(jnp.where(lane == ROUTE_GATE, g1, g2), I32)
    route = jnp.where((lane == ROUTE_GATE) | (lane == ROUTE_GATE + 1), gate_bits, route)
    route_ref[...] = route


def _router(x, a, wo, g, w_router, tm=512):
    n, d = x.shape
    ka = a.shape[1]
    wr = jnp.zeros((d, LANES), F32).at[:, :N_EXPERTS].set(w_router)
    wr_hi = wr.astype(BF16)
    wr_lo = (wr - wr_hi.astype(F32)).astype(BF16)
    row = lambda i: (i, 0)
    fix = lambda i: (0, 0)
    return pl.pallas_call(
        _router_kernel,
        grid=(n // tm,),
        in_specs=[pl.BlockSpec((tm, d), row), pl.BlockSpec((tm, ka), row),
                  pl.BlockSpec((ka, d), fix), pl.BlockSpec((1, d), fix),
                  pl.BlockSpec((d, LANES), fix), pl.BlockSpec((d, LANES), fix)],
        out_specs=[pl.BlockSpec((tm, d), row), pl.BlockSpec((tm * d // LANES, LANES), row),
                   pl.BlockSpec((tm, LANES), row), pl.BlockSpec((1, LANES), fix)],
        out_shape=[jax.ShapeDtypeStruct((n, d), F32), jax.ShapeDtypeStruct((n * d // LANES, LANES), F32),
                   jax.ShapeDtypeStruct((n, LANES), I32), jax.ShapeDtypeStruct((1, LANES), I32)],
        scratch_shapes=[pltpu.VMEM((1, LANES), F32)],
        compiler_params=_params("arbitrary"),
        name="router",
    )(x, a, wo, g.reshape(1, d), wr_hi, wr_lo)


def _experts_kernel(be_ref, nb_ref, src0_ref, srcn_ref, slot_ref, h_hbm, wg_ref, wu_ref, wd_ref,
                    y_hbm, in_ref, out_ref, acc_ref, hb_ref, gsem, ssem, *, rps, nf, n_tok, second):
    i = pl.program_id(0)
    f = pl.program_id(1)
    nb = nb_ref[0]
    tm, d = hb_ref.shape
    pieces = d // LANES
    assert rps * nf == tm

    def tile(r):
        return pl.ds(pl.multiple_of(r * pieces, pieces), pieces)

    def gather_row(src_ref, r):
        return pltpu.make_async_copy(h_hbm.at[tile(src_ref[r])], in_ref.at[tile(r)], gsem)

    def scatter_row(r):
        return pltpu.make_async_copy(out_ref.at[tile(r)], y_hbm.at[tile(slot_ref[r])], ssem)

    def spare_fill(region, k):
        rows = pl.ds((region * second + n_tok + k * tm) * pieces, tm * pieces)
        return pltpu.make_async_copy(out_ref, y_hbm.at[rows], ssem)

    def wait_gather():
        pltpu.make_async_copy(h_hbm.at[pl.ds(0, tm * pieces)], in_ref, gsem).wait()

    def wait_scatter():
        pltpu.make_async_copy(out_ref, y_hbm.at[pl.ds(0, tm * pieces)], ssem).wait()

    @pl.when((i == 0) & (f == 0))
    def _():
        out_ref[...] = jnp.zeros_like(out_ref)
        fills = [spare_fill(region, k) for region in range(2) for k in range((second - n_tok) // tm)]
        for cp in fills:
            cp.start()
        for cp in fills:
            cp.wait()

        def body(it, carry):
            for u in range(8):
                gather_row(src0_ref, it * 8 + u).start()
            return carry

        lax.fori_loop(0, tm // 8, body, 0)

    def start_rows(with_gather):
        for u in range(rps):
            r = f * rps + u
            if with_gather:
                gather_row(srcn_ref, r).start()
            scatter_row(r).start()

    @pl.when(i < nb)
    def _():
        @pl.when(f == 0)
        def _():
            wait_gather()
            for j in range(pieces):
                hb_ref[:, j * LANES:(j + 1) * LANES] = _row_tile_piece(in_ref, j, tm, pieces).astype(BF16)
            acc_ref[...] = jnp.zeros_like(acc_ref)

        start_rows(True)
        h = hb_ref[...]
        act = _silu(_dot(h, wg_ref[...])) * _dot(h, wu_ref[...])
        acc_ref[...] += _dot(act.astype(BF16), wd_ref[...])

        @pl.when(f == nf - 1)
        def _():
            wait_scatter()
            _to_row_tiles(out_ref, acc_ref[...])

    @pl.when(i == nb)
    def _():
        @pl.when(f == 0)
        def _():
            wait_gather()

        start_rows(False)

        @pl.when(f == nf - 1)
        def _():
            wait_scatter()


def _experts(h, src, slot, blk_e, nb_used, w_gate, w_up, w_down, tm, second, tf=896):
    d = w_gate.shape[1]
    pieces = d // LANES
    n = h.shape[0] // pieces
    ff = w_gate.shape[2]
    nf = ff // tf
    n_blocks = src.shape[0] // tm
    rps = tm // nf

    def clamp(i, f, be, nb):
        live = i < nb[0]
        return be[jnp.minimum(i, nb[0] - 1)], jnp.where(live, f, nf - 1)

    def wg_map(i, f, be, nb):
        e, fe = clamp(i, f, be, nb)
        return (e, 0, fe)

    def wd_map(i, f, be, nb):
        e, fe = clamp(i, f, be, nb)
        return (e, fe, 0)

    smem = lambda index_map: pl.BlockSpec((tm,), index_map, memory_space=pltpu.SMEM)
    grid_spec = pltpu.PrefetchScalarGridSpec(
        num_scalar_prefetch=2,
        grid=(n_blocks + 1, nf),
        in_specs=[smem(lambda i, f, be, nb: (0,)),
                  smem(lambda i, f, be, nb: (jnp.minimum(i + 1, nb[0] - 1),)),
                  smem(lambda i, f, be, nb: (jnp.clip(i - 1, 0, nb[0] - 1),)),
                  pl.BlockSpec(memory_space=pl.ANY),
                  pl.BlockSpec((None, d, tf), wg_map),
                  pl.BlockSpec((None, d, tf), wg_map),
                  pl.BlockSpec((None, tf, d), wd_map)],
        out_specs=pl.BlockSpec(memory_space=pl.ANY),
        scratch_shapes=[pltpu.VMEM((tm * pieces, LANES), F32), pltpu.VMEM((tm * pieces, LANES), F32),
                        pltpu.VMEM((tm, d), F32), pltpu.VMEM((tm, d), BF16),
                        pltpu.SemaphoreType.DMA(()), pltpu.SemaphoreType.DMA(())],
    )
    kern = functools.partial(_experts_kernel, rps=rps, nf=nf, n_tok=n, second=second)
    return pl.pallas_call(
        kern,
        grid_spec=grid_spec,
        out_shape=jax.ShapeDtypeStruct((2 * second * pieces, LANES), F32),
        compiler_params=_params("arbitrary", "arbitrary"),
        name="moe_experts",
    )(blk_e, nb_used, src, src, slot, h, w_gate, w_up, w_down)


def _combine_kernel(x_ref, route_ref, g_ref, ya_ref, yb_ref, o_ref):
    route = route_ref[...]
    lane = lax.broadcasted_iota(I32, route.shape, 1)
    gates = lax.bitcast_convert_type(route, F32)
    g1 = jnp.sum(jnp.where(lane == ROUTE_GATE, gates, 0.0), axis=-1, keepdims=True)
    g2 = jnp.sum(jnp.where(lane == ROUTE_GATE + 1, gates, 0.0), axis=-1, keepdims=True)
    tm, d = x_ref.shape
    pieces = d // LANES
    x4 = []
    for j in range(pieces):
        ya = _row_tile_piece(ya_ref, j, tm, pieces)
        yb = _row_tile_piece(yb_ref, j, tm, pieces)
        x4.append(x_ref[:, j * LANES:(j + 1) * LANES] + (ya * g1 + yb * g2))
    ms = sum(jnp.sum(p * p, axis=-1, keepdims=True) for p in x4) * (1.0 / d)
    inv = lax.rsqrt(ms + EPS)
    for j in range(pieces):
        cols = slice(j * LANES, (j + 1) * LANES)
        o_ref[:, cols] = x4[j] * inv * g_ref[:, cols]


def _combine(x, route, y, g, second, tm=512):
    n, d = x.shape
    off = second // tm
    pieces = d // LANES
    return pl.pallas_call(
        _combine_kernel,
        grid=(n // tm,),
        in_specs=[pl.BlockSpec((tm, d), lambda i: (i, 0)),
                  pl.BlockSpec((tm, LANES), lambda i: (i, 0)),
                  pl.BlockSpec((1, d), lambda i: (0, 0)),
                  pl.BlockSpec((tm * pieces, LANES), lambda i: (i, 0)),
                  pl.BlockSpec((tm * pieces, LANES), lambda i: (i + off, 0))],
        out_specs=pl.BlockSpec((tm, d), lambda i: (i, 0)),
        out_shape=jax.ShapeDtypeStruct((n, d), F32),
        compiler_params=_params("parallel"),
        name="moe_combine",
    )(x, route, g.reshape(1, d), y, y)


def _moe_layout(route, counts, n, tm):
    e = route[:, ROUTE_E:ROUTE_E + 2]
    rank = route[:, ROUTE_RANK:ROUTE_RANK + 2]
    sizes = counts[0, :N_EXPERTS]
    padded = ((sizes + tm - 1) // tm) * tm
    pend = jnp.cumsum(padded)
    pstart = pend - padded
    start_of = jnp.zeros_like(e)
    for j in range(N_EXPERTS):
        start_of = jnp.where(e == j, pstart[j], start_of)
    dest = (start_of + rank).astype(I32)
    n_blocks = (2 * n) // tm + N_EXPERTS
    rows = n_blocks * tm
    blk_e = jnp.minimum(jnp.searchsorted(pend, jnp.arange(n_blocks, dtype=I32) * tm, side='right'),
                        N_EXPERTS - 1).astype(I32)
    nb_used = (pend[-1:] // tm).astype(I32)
    second = n + N_EXPERTS * tm
    pair_slot = jnp.arange(n, dtype=I32)[:, None] + jnp.array([0, second], I32)[None, :]
    row = jnp.arange(rows, dtype=I32)
    row_e = jnp.repeat(blk_e, tm)
    pad_j = jnp.clip(row - (pstart + sizes)[row_e], 0, tm - 1)
    spare = n + row_e * tm + pad_j
    slot = spare.at[dest.reshape(-1)].set(pair_slot.reshape(-1))
    token = slot % second
    src = jnp.where(token < n, token, 0)
    return src, slot, blk_e, nb_used, second


def kernel(x, positions, g_mix, g_ffn, g_final, w_in_attn, w_out_attn, attn_sinks, w_in_rec,
           rec_lower_bounds, rec_norm_g, w_out_rec, w_gate_dense, w_up_dense, w_down_dense,
           w_router, w_gate_moe, w_up_moe, w_down_moe):
    bsz, seq, d = x.shape
    n = bsz * seq
    x0 = x.reshape(n, d)
    bf = lambda w: w.astype(BF16)

    cos_t, sin_t = _rope_tables(positions)
    qkv1, qkv4, qkv16, qkvb = _attn_inproj(x, g_mix[0], bf(w_in_attn[0]), cos_t, sin_t)
    mix_a = _dilated_attention(qkv1, qkv4, qkv16)
    mix_b = _swa_gqa(qkvb, attn_sinks[0])
    a_w = A_HEADS * HEAD_DIM
    w_out = bf(w_out_attn[0])
    x2 = _outproj_ffn(x0, mix_a.reshape(n, -1), mix_b.reshape(n, -1), w_out[:a_w], w_out[a_w:],
                      g_ffn[0], bf(w_gate_dense[0]), bf(w_up_dense[0]), bf(w_down_dense[0]))

    lb = jax.nn.softmax(rec_lower_bounds.astype(F32), axis=0)
    lb1 = (jnp.cumsum(lb, axis=0) - lb[0])[1]
    kw = lb1.shape[0]
    vw = rec_norm_g.shape[1]
    q, k, lf, v, gate = _rec_inproj(x2, g_mix[1], bf(w_in_rec[0]), lb1, kw, vw)
    rec = _hgrn(q, k, lf, v, gate, rec_norm_g[0], bsz, seq)
    x3, h, route, counts = _router(x2, rec, bf(w_out_rec[0]), g_ffn[1], w_router[0])
    tm_moe = min(1024, n)
    src, slot, blk_e, nb_used, second = _moe_layout(route, counts, n, tm_moe)
    ys = _experts(h, src, slot, blk_e, nb_used, bf(w_gate_moe[0]), bf(w_up_moe[0]),
                  bf(w_down_moe[0]), tm_moe, second)
    out = _combine(x3, route, ys, g_final, second)
    return out.reshape(bsz, seq, d)
```

```python
import functools

import jax
import jax.numpy as jnp
from jax import lax
from jax.experimental import pallas as pl
from jax.experimental.pallas import tpu as pltpu

F32 = jnp.float32
BF16 = jnp.bfloat16
I32 = jnp.int32

EPS = 1e-6
HEAD_DIM = 64
ROT_DIM = HEAD_DIM // 4
ROPE_THETA = 500000.0
LANES = 128
ATTN_BLOCK = 128
ATTN_UNROLL = 3
A_HEADS = 8
A_PATTERNS = ((128, 1), (512, 4), (2048, 16))
B_Q_HEADS = 8
B_KV_HEADS = 2
B_WINDOW = 128
C_HEADS = 8
N_EXPERTS = 8
REC_CHUNK = 128
REC_SUB = 16
REC_HEADS = 4
ROUTER_SUB = 256
REC_SAFE_DROP = -80.0
VMEM_LIMIT = 56 * 1024 * 1024


def _params(*sem):
    return pltpu.CompilerParams(dimension_semantics=sem, vmem_limit_bytes=VMEM_LIMIT)


def _rms(x, g):
    return x * lax.rsqrt(jnp.mean(x * x, axis=-1, keepdims=True) + EPS) * g


def _silu(x):
    return x / (1.0 + jnp.exp(-x))


def _dot(a, b):
    return jnp.dot(a, b, preferred_element_type=F32)


def _dot_nt(a, b):
    return lax.dot_general(a, b, (((1,), (1,)), ((), ())), preferred_element_type=F32)


def _dot_tn(a, b):
    return lax.dot_general(a, b, (((0,), (0,)), ((), ())), preferred_element_type=F32)


def _rows(j, size, count=1):
    if isinstance(j, int):
        return slice(j * size, (j + count) * size)
    return pl.ds(pl.multiple_of(j * size, size), count * size)


def _to_row_tiles(ref, x, row0=0):
    pieces = x.shape[1] // LANES
    for j in range(pieces):
        ref[pl.ds(row0 * pieces + j, x.shape[0], stride=pieces), :] = x[:, j * LANES:(j + 1) * LANES]


def _row_tile_piece(ref, j, m, pieces):
    return ref[pl.ds(j, m, stride=pieces), :]


def _rope_table_kernel(pos_ref, invf_ref, cos_ref, sin_ref):
    ang = pos_ref[...].astype(F32) * invf_ref[...]
    cos_ref[...] = jnp.cos(ang)
    sin_ref[...] = jnp.sin(ang)


def _rope_tables(positions):
    n = positions.size
    half = ROT_DIM // 2
    per_row = LANES // half
    inv_freq = jnp.power(ROPE_THETA, -jnp.arange(0, ROT_DIM, 2, dtype=F32) / ROT_DIM)
    pos_rep = jnp.repeat(positions.reshape(n // per_row, per_row), half, axis=1)
    invf_row = jnp.tile(inv_freq, per_row).reshape(1, LANES)
    rows = n // per_row
    tr = min(rows, 512)
    cos8, sin8 = pl.pallas_call(
        _rope_table_kernel,
        grid=(rows // tr,),
        in_specs=[pl.BlockSpec((tr, LANES), lambda i: (i, 0)),
                  pl.BlockSpec((1, LANES), lambda i: (0, 0))],
        out_specs=[pl.BlockSpec((tr, LANES), lambda i: (i, 0))] * 2,
        out_shape=[jax.ShapeDtypeStruct((rows, LANES), F32)] * 2,
        compiler_params=_params("parallel"),
        name="rope_tables",
    )(pos_rep, invf_row)
    cos8 = cos8.reshape(n, half)
    sin8 = sin8.reshape(n, half)
    rest = HEAD_DIM - ROT_DIM
    c64 = jnp.concatenate([cos8, cos8, jnp.ones((n, rest), F32)], axis=1)
    s64 = jnp.concatenate([-sin8, sin8, jnp.zeros((n, rest), F32)], axis=1)
    return jnp.tile(c64, (1, 2)), jnp.tile(s64, (1, 2))


A_BLOCKS = A_HEADS * HEAD_DIM // LANES
BQ_BLOCKS = B_Q_HEADS * HEAD_DIM // LANES
A_QKV = 3 * A_BLOCKS * LANES
B_QKV = (BQ_BLOCKS + 2 * B_KV_HEADS) * LANES


def _attn_inproj_kernel(x_ref, g_ref, w_ref, c_ref, s_ref, o1_ref, o4_ref, o16_ref, ob_ref, sc_ref):
    h = _rms(x_ref[0], g_ref[...]).astype(BF16)
    acc = _dot(h, w_ref[...])
    tm = acc.shape[0]
    c = c_ref[0]
    s = s_ref[0]
    lane = lax.broadcasted_iota(I32, c.shape, 1)
    first = (lane % HEAD_DIM) < (ROT_DIM // 2)
    lo_half = lane < HEAD_DIM

    def rope(blk):
        up = pltpu.roll(blk, LANES - ROT_DIM // 2, 1)
        dn = pltpu.roll(blk, ROT_DIM // 2, 1)
        return blk * c + jnp.where(first, up, dn) * s

    def col(cb):
        return acc[:, cb * LANES:(cb + 1) * LANES]

    scale = HEAD_DIM ** -0.5
    for cb in range(3 * A_BLOCKS):
        blk = col(cb)
        if cb < A_BLOCKS:
            blk = rope(blk) * scale
        elif cb < 2 * A_BLOCKS:
            blk = rope(blk)
        sc_ref[cb] = blk
        o1_ref[0, :, cb * LANES:(cb + 1) * LANES] = blk.astype(BF16)
    for o_ref in (o4_ref, o16_ref):
        dil = o_ref.shape[1]
        for r in range(dil):
            for cb in range(3 * A_BLOCKS):
                o_ref[0, r, :, cb * LANES:(cb + 1) * LANES] = (
                    sc_ref[cb, pl.ds(r, tm // dil, stride=dil), :].astype(BF16))
    base = 3 * A_BLOCKS
    for j in range(BQ_BLOCKS):
        ob_ref[0, :, j * LANES:(j + 1) * LANES] = (rope(col(base + j)) * scale).astype(BF16)
    for j, blk in enumerate((rope(col(base + BQ_BLOCKS)), col(base + BQ_BLOCKS + 1))):
        swapped = pltpu.roll(blk, HEAD_DIM, 1)
        for g, dup in enumerate((jnp.where(lo_half, blk, swapped), jnp.where(lo_half, swapped, blk))):
            cb = BQ_BLOCKS + j * B_KV_HEADS + g
            ob_ref[0, :, cb * LANES:(cb + 1) * LANES] = dup.astype(BF16)


def _attn_inproj(x, g, w, cos_t, sin_t, tm=512):
    bsz, seq, d = x.shape
    width = w.shape[1]
    assert B_KV_HEADS * HEAD_DIM == LANES and width == A_QKV + (BQ_BLOCKS + 2) * LANES
    tile = lambda b, i: (b, i, 0)
    fix = lambda b, i: (0, 0)
    d4, d16 = A_PATTERNS[1][1], A_PATTERNS[2][1]
    perm = lambda dil: pl.BlockSpec((1, dil, tm // dil, A_QKV), lambda b, i: (b, 0, i, 0))
    return pl.pallas_call(
        _attn_inproj_kernel,
        grid=(bsz, seq // tm),
        in_specs=[pl.BlockSpec((1, tm, d), tile),
                  pl.BlockSpec((1, d), fix),
                  pl.BlockSpec((d, width), fix),
                  pl.BlockSpec((1, tm, LANES), tile),
                  pl.BlockSpec((1, tm, LANES), tile)],
        out_specs=[pl.BlockSpec((1, tm, A_QKV), tile), perm(d4), perm(d16),
                   pl.BlockSpec((1, tm, B_QKV), tile)],
        out_shape=[jax.ShapeDtypeStruct((bsz, seq, A_QKV), BF16),
                   jax.ShapeDtypeStruct((bsz, d4, seq // d4, A_QKV), BF16),
                   jax.ShapeDtypeStruct((bsz, d16, seq // d16, A_QKV), BF16),
                   jax.ShapeDtypeStruct((bsz, seq, B_QKV), BF16)],
        scratch_shapes=[pltpu.VMEM((3 * A_BLOCKS, tm, LANES), F32)],
        compiler_params=_params("parallel", "parallel"),
        name="attn_inproj",
    )(x, g.reshape(1, d), w, cos_t.reshape(bsz, seq, LANES), sin_t.reshape(bsz, seq, LANES))


def _band_mask(n_back):
    qi = lax.broadcasted_iota(I32, (ATTN_BLOCK, 2 * ATTN_BLOCK), 0)
    kj = lax.broadcasted_iota(I32, (ATTN_BLOCK, 2 * ATTN_BLOCK), 1)
    dist = ATTN_BLOCK + qi - kj
    return (dist >= 0) & (dist <= n_back), kj >= ATTN_BLOCK


def _band_block(q_pairs, kk, vv, valid, sinks=None, want_lse=True):
    lane = lax.broadcasted_iota(I32, (ATTN_BLOCK, LANES), 1)
    lo_half = lane < HEAD_DIM
    zero = jnp.zeros((ATTN_BLOCK, LANES), BF16)
    lhs = []
    for q in q_pairs:
        lhs += [jnp.where(lo_half, q, zero), jnp.where(lo_half, zero, q)]
    s_all = _dot_nt(jnp.concatenate(lhs, axis=0), kk)
    ps, inv_l, lses = [], [], []
    for u in range(len(lhs)):
        s = jnp.where(valid, s_all[u * ATTN_BLOCK:(u + 1) * ATTN_BLOCK], -jnp.inf)
        m = jnp.max(s, axis=-1, keepdims=True)
        if sinks is not None:
            m = jnp.maximum(m, sinks[u])
        e = jnp.exp(s - m)
        l = jnp.sum(e, axis=-1, keepdims=True)
        if sinks is not None:
            l = l + jnp.exp(sinks[u] - m)
        ps.append(e.astype(BF16))
        inv_l.append(1.0 / l)
        lses.append(m + jnp.log(l) if want_lse else None)
    o_all = _dot(jnp.concatenate(ps, axis=0), vv)
    outs = []
    for j in range(len(q_pairs)):
        o0 = o_all[(2 * j) * ATTN_BLOCK:(2 * j + 1) * ATTN_BLOCK] * inv_l[2 * j]
        o1 = o_all[(2 * j + 1) * ATTN_BLOCK:(2 * j + 2) * ATTN_BLOCK] * inv_l[2 * j + 1]
        lse = jnp.where(lo_half, lses[2 * j], lses[2 * j + 1]) if want_lse else None
        outs.append((jnp.where(lo_half, o0, o1), lse))
    return outs


def _for_each_block(nq, first_fn, rest_fn):
    first_fn()
    rest = nq - 1
    if rest <= ATTN_UNROLL:
        for jb in range(1, nq):
            rest_fn(jb)
        return
    assert rest % ATTN_UNROLL == 0

    def body(it, carry):
        for u in range(ATTN_UNROLL):
            rest_fn(1 + it * ATTN_UNROLL + u)
        return carry

    lax.fori_loop(0, rest // ATTN_UNROLL, body, 0)


def _dilated_kernel(q1, kp1, kc1, vp1, vc1, q4, kp4, kc4, vp4, vc4, q16, kp16, kc16, vp16, vc16,
                    o_ref, o4_s, l4_s, o16_s, l16_s, *, n_backs):
    tile = o_ref.shape[1]
    not_first = pl.program_id(2) > 0

    def masks(n_back):
        band, in_cur = _band_mask(n_back)
        return band & (in_cur | not_first), band

    def halo(p_ref, c_ref, idx):
        return jnp.concatenate([p_ref[idx], c_ref[idx + (slice(0, ATTN_BLOCK),)]], axis=0)

    def window(c_ref, idx, jb):
        return c_ref[idx + (_rows(jb - 1, ATTN_BLOCK, 2),)]

    def rows(jb):
        return _rows(jb, ATTN_BLOCK)

    for (q, kp, kc, vp, vc, o_s, l_s), n_back in zip(
            ((q16, kp16, kc16, vp16, vc16, o16_s, l16_s), (q4, kp4, kc4, vp4, vc4, o4_s, l4_s)),
            (n_backs[2], n_backs[1])):
        dil = q.shape[1]
        nq = q.shape[2] // ATTN_BLOCK
        valid0, valid = masks(n_back)
        for r in range(dil):
            idx = (0, r)

            def put(jb, res, r=r, dil=dil, o_s=o_s, l_s=l_s):
                (o, lse), = res
                dst = pl.ds(jb * ATTN_BLOCK * dil + r, ATTN_BLOCK, stride=dil)
                o_s[dst, :] = o
                l_s[dst, :] = lse

            def first(q=q, kp=kp, kc=kc, vp=vp, vc=vc, idx=idx, put=put, valid0=valid0):
                put(0, _band_block([q[idx + (slice(0, ATTN_BLOCK),)]], halo(kp, kc, idx),
                                   halo(vp, vc, idx), valid0))

            def rest(jb, q=q, kc=kc, vc=vc, idx=idx, put=put, valid=valid):
                put(jb, _band_block([q[idx + (rows(jb),)]], window(kc, idx, jb),
                                    window(vc, idx, jb), valid))

            _for_each_block(nq, first, rest)

    valid0, valid = masks(n_backs[0])
    idx = (0,)

    def merge(jb, res):
        (o, lse), = res
        dst = rows(jb)
        l4 = l4_s[dst, :]
        l16 = l16_s[dst, :]
        mx = jnp.maximum(jnp.maximum(lse, l4), l16)
        w1 = jnp.exp(lse - mx)
        w4 = jnp.exp(l4 - mx)
        w16 = jnp.exp(l16 - mx)
        num = w1 * o + w4 * o4_s[dst, :] + w16 * o16_s[dst, :]
        o_ref[0, dst, :] = (num / (w1 + w4 + w16)).astype(o_ref.dtype)

    _for_each_block(
        tile // ATTN_BLOCK,
        lambda: merge(0, _band_block([q1[0, 0:ATTN_BLOCK]], halo(kp1, kc1, idx), halo(vp1, vc1, idx), valid0)),
        lambda jb: merge(jb, _band_block([q1[0, rows(jb)]], window(kc1, idx, jb), window(vc1, idx, jb), valid)))


def _dilated_attention(qkv1, qkv4, qkv16):
    bsz, seq, _ = qkv1.shape
    dils = tuple(p[1] for p in A_PATTERNS)
    assert dils[0] == 1 and qkv4.shape[1] == dils[1] and qkv16.shape[1] == dils[2]
    tile = dils[2] * ATTN_BLOCK
    nt = seq // tile

    def specs(dil):
        rows = tile // dil
        nb = rows // ATTN_BLOCK
        if dil == 1:
            cur = lambda off: pl.BlockSpec((1, rows, LANES), lambda b, hp, i: (b, i, off + hp))
            prev = lambda off: pl.BlockSpec(
                (1, ATTN_BLOCK, LANES), lambda b, hp, i: (b, jnp.maximum(i * nb - 1, 0), off + hp))
        else:
            cur = lambda off: pl.BlockSpec((1, dil, rows, LANES), lambda b, hp, i: (b, 0, i, off + hp))
            prev = lambda off: pl.BlockSpec(
                (1, dil, ATTN_BLOCK, LANES), lambda b, hp, i: (b, 0, jnp.maximum(i * nb - 1, 0), off + hp))
        return [cur(0), prev(A_BLOCKS), cur(A_BLOCKS), prev(2 * A_BLOCKS), cur(2 * A_BLOCKS)]

    kern = functools.partial(_dilated_kernel, n_backs=tuple(w // d for w, d in A_PATTERNS))
    return pl.pallas_call(
        kern,
        grid=(bsz, A_BLOCKS, nt),
        in_specs=specs(1) + specs(dils[1]) + specs(dils[2]),
        out_specs=pl.BlockSpec((1, tile, LANES), lambda b, hp, i: (b, i, hp)),
        out_shape=jax.ShapeDtypeStruct((bsz, seq, A_BLOCKS * LANES), BF16),
        scratch_shapes=[pltpu.VMEM((tile, LANES), F32)] * 4,
        compiler_params=_params("parallel", "parallel", "arbitrary"),
        name="dilated_attn",
    )(*([qkv1] * 5 + [qkv4] * 5 + [qkv16] * 5))


def _swa_kernel(sink_ref, q_ref, *rest, n_back, nq):
    kv = rest[:4 * B_KV_HEADS]
    o_ref = rest[4 * B_KV_HEADS]
    not_first = pl.program_id(1) > 0
    band, in_cur = _band_mask(n_back)
    valid0 = band & (in_cur | not_first)
    pairs = BQ_BLOCKS // B_KV_HEADS

    def run(q_rows, kv_of, valid):
        for g in range(B_KV_HEADS):
            cbs = [g * pairs + j for j in range(pairs)]
            qs = [q_ref[0, q_rows, cb * LANES:(cb + 1) * LANES] for cb in cbs]
            sinks = [sink_ref[2 * cb + p] for cb in cbs for p in range(2)]
            kk, vv = kv_of(g)
            res = _band_block(qs, kk, vv, valid, sinks, want_lse=False)
            for cb, (o, _) in zip(cbs, res):
                o_ref[0, q_rows, cb * LANES:(cb + 1) * LANES] = o.astype(o_ref.dtype)

    def halo_kv(g):
        kp, kc, vp, vc = kv[4 * g:4 * g + 4]
        return (jnp.concatenate([kp[0], kc[0, 0:ATTN_BLOCK]], axis=0),
                jnp.concatenate([vp[0], vc[0, 0:ATTN_BLOCK]], axis=0))

    def window_kv(jb):
        win = _rows(jb - 1, ATTN_BLOCK, 2)
        return lambda g: (kv[4 * g + 1][0, win], kv[4 * g + 3][0, win])

    _for_each_block(
        nq,
        lambda: run(_rows(0, ATTN_BLOCK), halo_kv, valid0),
        lambda jb: run(_rows(jb, ATTN_BLOCK), window_kv(jb), band))


def _swa_gqa(qkvb, sinks, tq=512):
    bsz, seq, _ = qkvb.shape
    tq = min(seq, tq)
    nq = tq // ATTN_BLOCK
    bq_w = BQ_BLOCKS * LANES
    in_specs = [pl.BlockSpec(memory_space=pltpu.SMEM),
                pl.BlockSpec((1, tq, bq_w), lambda b, i: (b, i, 0))]
    for g in range(B_KV_HEADS):
        for section in range(2):
            cb = BQ_BLOCKS + section * B_KV_HEADS + g
            in_specs += [pl.BlockSpec((1, ATTN_BLOCK, LANES),
                                      lambda b, i, cb=cb: (b, jnp.maximum(i * nq - 1, 0), cb)),
                         pl.BlockSpec((1, tq, LANES), lambda b, i, cb=cb: (b, i, cb))]
    args = [sinks.astype(F32)] + [qkvb] * (1 + 4 * B_KV_HEADS)
    kern = functools.partial(_swa_kernel, n_back=B_WINDOW - 1, nq=nq)
    return pl.pallas_call(
        kern,
        grid=(bsz, seq // tq),
        in_specs=in_specs,
        out_specs=pl.BlockSpec((1, tq, bq_w), lambda b, i: (b, i, 0)),
        out_shape=jax.ShapeDtypeStruct((bsz, seq, bq_w), BF16),
        compiler_params=_params("parallel", "arbitrary"),
        name="swa_gqa",
    )(*args)


def _outproj_ffn_kernel(x_ref, a_ref, b_ref, wa_ref, wb_ref, g_ref, wg_ref, wu_ref, wd_ref,
                        o_ref, h_ref):
    f = pl.program_id(1)

    @pl.when(f == 0)
    def _():
        x1 = x_ref[...] + _dot(a_ref[...], wa_ref[...]) + _dot(b_ref[...], wb_ref[...])
        o_ref[...] = x1
        h_ref[...] = _rms(x1, g_ref[...]).astype(BF16)

    h = h_ref[...]
    act = _silu(_dot(h, wg_ref[...])) * _dot(h, wu_ref[...])
    o_ref[...] += _dot(act.astype(BF16), wd_ref[...])


def _outproj_ffn(x, mix_a, mix_b, w_a, w_b, g, w_gate, w_up, w_down, tm=1024, tf=512):
    n, d = x.shape
    ff = w_gate.shape[1]
    ka, kb = mix_a.shape[1], mix_b.shape[1]
    return pl.pallas_call(
        _outproj_ffn_kernel,
        grid=(n // tm, ff // tf),
        in_specs=[pl.BlockSpec((tm, d), lambda i, f: (i, 0)),
                  pl.BlockSpec((tm, ka), lambda i, f: (i, 0)),
                  pl.BlockSpec((tm, kb), lambda i, f: (i, 0)),
                  pl.BlockSpec((ka, d), lambda i, f: (0, 0)),
                  pl.BlockSpec((kb, d), lambda i, f: (0, 0)),
                  pl.BlockSpec((1, d), lambda i, f: (0, 0)),
                  pl.BlockSpec((d, tf), lambda i, f: (0, f)),
                  pl.BlockSpec((d, tf), lambda i, f: (0, f)),
                  pl.BlockSpec((tf, d), lambda i, f: (f, 0))],
        out_specs=pl.BlockSpec((tm, d), lambda i, f: (i, 0)),
        out_shape=jax.ShapeDtypeStruct((n, d), F32),
        scratch_shapes=[pltpu.VMEM((tm, d), BF16)],
        compiler_params=_params("parallel", "arbitrary"),
        name="outproj_ffn",
    )(x, mix_a, mix_b, w_a, w_b, g.reshape(1, d), w_gate, w_up, w_down)


def _rec_inproj_kernel(x_ref, g_ref, w_ref, lb_ref, q_ref, k_ref, lf_ref, v_ref, gate_ref):
    h = _rms(x_ref[...], g_ref[...]).astype(BF16)
    acc = _dot(h, w_ref[...])
    kw = q_ref.shape[1]
    vw = v_ref.shape[1]
    lb = lb_ref[...]
    q_ref[...] = _silu(acc[:, :kw]).astype(BF16)
    fg = lb + (1.0 - lb) / (1.0 + jnp.exp(-acc[:, kw:2 * kw]))
    k_ref[...] = (1.0 - fg).astype(BF16)
    lf_ref[...] = jnp.log(fg)
    v_ref[...] = acc[:, 2 * kw:2 * kw + vw].astype(BF16)
    gate_ref[...] = _silu(acc[:, 2 * kw + vw:]).astype(BF16)


def _rec_inproj(x, g, w, lb, kw, vw, tm=512):
    n, d = x.shape
    width = w.shape[1]
    row = lambda i: (i, 0)
    fix = lambda i: (0, 0)
    return pl.pallas_call(
        _rec_inproj_kernel,
        grid=(n // tm,),
        in_specs=[pl.BlockSpec((tm, d), row),
                  pl.BlockSpec((1, d), fix),
                  pl.BlockSpec((d, width), fix),
                  pl.BlockSpec((1, kw), fix)],
        out_specs=[pl.BlockSpec((tm, kw), row), pl.BlockSpec((tm, kw), row),
                   pl.BlockSpec((tm, kw), row), pl.BlockSpec((tm, vw), row),
                   pl.BlockSpec((tm, vw), row)],
        out_shape=[jax.ShapeDtypeStruct((n, kw), BF16), jax.ShapeDtypeStruct((n, kw), BF16),
                   jax.ShapeDtypeStruct((n, kw), F32), jax.ShapeDtypeStruct((n, vw), BF16),
                   jax.ShapeDtypeStruct((n, vw), BF16)],
        compiler_params=_params("parallel"),
        name="rec_inproj",
    )(x, g.reshape(1, d), w, lb.reshape(1, kw))


def _hgrn_kernel(q_ref, k_ref, lf_ref, v_ref, gate_ref, ng_ref, o_ref, st_ref, b_ref, kf_ref, *,
                 n_chunks):
    @pl.when(pl.program_id(2) == 0)
    def _():
        st_ref[...] = jnp.zeros_like(st_ref)

    c_len, sub = REC_CHUNK, REC_SUB
    heads = [slice(hh * LANES, (hh + 1) * LANES) for hh in range(q_ref.shape[2] // LANES)]
    r_i = lax.broadcasted_iota(I32, (c_len, c_len), 0)
    c_i = lax.broadcasted_iota(I32, (c_len, c_len), 1)
    causal = c_i <= r_i
    tril = causal.astype(BF16)

    drop = jnp.zeros((1, LANES), F32)
    for c in range(n_chunks):
        rows = slice(c * c_len, (c + 1) * c_len)
        for hh, cols in enumerate(heads):
            lf = lf_ref[0, rows, cols]
            lf1 = lf.astype(BF16)
            rem = lf - lf1.astype(F32)
            lf2 = rem.astype(BF16)
            lf3 = (rem - lf2.astype(F32)).astype(BF16)
            b = _dot(tril, lf1) + _dot(tril, lf2) + _dot(tril, lf3)
            b_ref[hh, rows, :] = b
            for i in range(c_len // sub):
                end = b[(i + 1) * sub - 1:(i + 1) * sub, :]
                drop = jnp.minimum(drop, end - b[i * sub - 1:i * sub, :] if i > 0 else end)
    safe = jnp.min(drop) > REC_SAFE_DROP

    def finish(c, hh, att, st):
        rows = _rows(c, c_len)
        cols = heads[hh]
        b = b_ref[hh, rows, :]
        q = q_ref[0, rows, cols].astype(F32)
        k = k_ref[0, rows, cols].astype(F32)
        v = v_ref[0, rows, cols]
        b_last = b[c_len - 1:c_len, :]
        qe = (q * jnp.exp(b)).astype(BF16)
        o = _dot(att.astype(BF16), v) + _dot_nt(qe, st.astype(BF16))
        kd = (k * jnp.exp(b_last - b)).astype(BF16)
        st = st * jnp.exp(b_last) + _dot_tn(v, kd)
        y = o * lax.rsqrt(jnp.mean(o * o, axis=-1, keepdims=True) + EPS)
        o_ref[0, rows, cols] = (y * ng_ref[:, cols] * gate_ref[0, rows, cols].astype(F32)).astype(o_ref.dtype)
        return st

    @pl.when(safe)
    def _():
        sts = [st_ref[hh] for hh in range(len(heads))]
        for c in range(n_chunks):
            rows = slice(c * c_len, (c + 1) * c_len)
            for hh, cols in enumerate(heads):
                b = b_ref[hh, rows, :]
                q = q_ref[0, rows, cols].astype(F32)
                k = k_ref[0, rows, cols].astype(F32)
                att_rows = []
                for i in range(c_len // sub):
                    lo, hi = i * sub, (i + 1) * sub
                    ref = b[lo - 1:lo, :] if i > 0 else jnp.zeros((1, LANES), F32)
                    qt = (q[lo:hi] * jnp.exp(b[lo:hi] - ref)).astype(BF16)
                    kt = (k[:hi] * jnp.exp(ref - b[:hi])).astype(BF16)
                    if hi < c_len:
                        kt = jnp.concatenate([kt, jnp.zeros((c_len - hi, LANES), BF16)], axis=0)
                    att_rows.append(_dot_nt(qt, kt))
                att = jnp.where(causal, jnp.concatenate(att_rows, axis=0), 0.0)
                sts[hh] = finish(c, hh, att, sts[hh])
        for hh in range(len(heads)):
            st_ref[hh] = sts[hh]

    @pl.when(jnp.logical_not(safe))
    def _():
        for hh, cols in enumerate(heads):
            def chunk(c, st, hh=hh, cols=cols):
                rows = _rows(c, c_len)
                b = b_ref[hh, rows, :]
                q = q_ref[0, rows, cols].astype(F32)
                kf_ref[...] = k_ref[0, rows, cols].astype(F32)

                def key(s, att):
                    b_s = b_ref[hh, pl.ds(c * c_len + s, 1), :]
                    dec = jnp.exp(jnp.minimum(b - b_s, 0.0))
                    col = jnp.sum(q * kf_ref[pl.ds(s, 1), :] * dec, axis=-1, keepdims=True)
                    return jnp.where(c_i == s, col, att)

                att = lax.fori_loop(0, c_len, key, jnp.zeros((c_len, c_len), F32))
                return finish(c, hh, jnp.where(causal, att, 0.0), st)

            st_ref[hh] = lax.fori_loop(0, n_chunks, chunk, st_ref[hh])


def _hgrn(q, k, lf, v, gate, norm_g, bsz, seq, ts=512):
    kw = q.shape[-1] // C_HEADS
    vw = v.shape[-1] // C_HEADS
    assert kw == LANES and vw == LANES
    ts = min(ts, seq)
    v3 = lambda t: t.reshape(bsz, seq, t.shape[-1])
    wide = REC_HEADS * LANES
    blk = pl.BlockSpec((1, ts, wide), lambda b, h, c: (b, c, h))
    kern = functools.partial(_hgrn_kernel, n_chunks=ts // REC_CHUNK)
    out = pl.pallas_call(
        kern,
        grid=(bsz, C_HEADS // REC_HEADS, seq // ts),
        in_specs=[blk, blk, blk, blk, blk, pl.BlockSpec((1, wide), lambda b, h, c: (0, h))],
        out_specs=blk,
        out_shape=jax.ShapeDtypeStruct((bsz, seq, C_HEADS * vw), BF16),
        scratch_shapes=[pltpu.VMEM((REC_HEADS, vw, kw), F32), pltpu.VMEM((REC_HEADS, ts, LANES), F32),
                        pltpu.VMEM((REC_CHUNK, LANES), F32)],
        compiler_params=_params("parallel", "parallel", "arbitrary"),
        name="hgrn2",
    )(v3(q), v3(k), v3(lf), v3(v), v3(gate), norm_g.reshape(1, -1))
    return out.reshape(bsz * seq, C_HEADS * vw)


ROUTE_E, ROUTE_RANK, ROUTE_GATE = 0, 2, 4


def _router_kernel(x_ref, a_ref, wo_ref, g_ref, wr_hi_ref, wr_lo_ref, x3_ref, h_ref, route_ref,
                   cnt_ref, base_ref):
    @pl.when(pl.program_id(0) == 0)
    def _():
        base_ref[...] = jnp.zeros_like(base_ref)

    sub = min(ROUTER_SUB, x_ref.shape[0])
    lane = lax.broadcasted_iota(I32, (sub, LANES), 1)
    lane_f = lane.astype(F32)
    r_i = lax.broadcasted_iota(I32, (sub, sub), 0)
    c_i = lax.broadcasted_iota(I32, (sub, sub), 1)
    earlier = (c_i < r_i).astype(BF16)
    base = base_ref[...]
    for r0 in range(0, x_ref.shape[0], sub):
        rows = slice(r0, r0 + sub)
        x3 = x_ref[rows, :] + _dot(a_ref[rows, :], wo_ref[...])
        x3_ref[rows, :] = x3
        h = _rms(x3, g_ref[...])
        _to_row_tiles(h_ref, h, r0)
        h_hi = h.astype(BF16)
        h_lo = (h - h_hi.astype(F32)).astype(BF16)
        logits = _dot(h_hi, wr_hi_ref[...]) + _dot(h_hi, wr_lo_ref[...]) + _dot(h_lo, wr_hi_ref[...])
        logits = jnp.where(lane < N_EXPERTS, logits, -jnp.inf)
        v1 = jnp.max(logits, axis=-1, keepdims=True)
        e1 = jnp.min(jnp.where(logits == v1, lane_f, float(LANES)), axis=-1, keepdims=True)
        hot1 = lane_f == e1
        rest = jnp.where(hot1, -jnp.inf, logits)
        v2 = jnp.max(rest, axis=-1, keepdims=True)
        e2 = jnp.min(jnp.where(rest == v2, lane_f, float(LANES)), axis=-1, keepdims=True)
        hot2 = lane_f == e2
        t = jnp.exp(v2 - v1)
        g1 = 1.0 / (1.0 + t)
        g2 = t / (1.0 + t)
        member = (hot1 | hot2).astype(BF16)
        before = _dot(earlier, member) + base
        rank1 = jnp.sum(jnp.where(hot1, before, 0.0), axis=-1, keepdims=True).astype(I32)
        rank2 = jnp.sum(jnp.where(hot2, before, 0.0), axis=-1, keepdims=True).astype(I32)
        base = base + jnp.sum(member.astype(F32), axis=0, keepdims=True)
        route = jnp.where(lane == ROUTE_E, e1.astype(I32), 0)
        route = jnp.where(lane == ROUTE_E + 1, e2.astype(I32), route)
        route = jnp.where(lane == ROUTE_RANK, rank1, route)
        route = jnp.where(lane == ROUTE_RANK + 1, rank2, route)
        gate_bits = lax.bitcast_convert_type(jnp.where(lane == ROUTE_GATE, g1, g2), I32)
        route = jnp.where((lane == ROUTE_GATE) | (lane == ROUTE_GATE + 1), gate_bits, route)
        route_ref[rows, :] = route
    base_ref[...] = base
    cnt_ref[...] = base.astype(I32)


def _router(x, a, wo, g, w_router, tm=512):
    n, d = x.shape
    ka = a.shape[1]
    wr = jnp.zeros((d, LANES), F32).at[:, :N_EXPERTS].set(w_router)
    wr_hi = wr.astype(BF16)
    wr_lo = (wr - wr_hi.astype(F32)).astype(BF16)
    row = lambda i: (i, 0)
    fix = lambda i: (0, 0)
    return pl.pallas_call(
        _router_kernel,
        grid=(n // tm,),
        in_specs=[pl.BlockSpec((tm, d), row), pl.BlockSpec((tm, ka), row),
                  pl.BlockSpec((ka, d), fix), pl.BlockSpec((1, d), fix),
                  pl.BlockSpec((d, LANES), fix), pl.BlockSpec((d, LANES), fix)],
        out_specs=[pl.BlockSpec((tm, d), row), pl.BlockSpec((tm * d // LANES, LANES), row),
                   pl.BlockSpec((tm, LANES), row), pl.BlockSpec((1, LANES), fix)],
        out_shape=[jax.ShapeDtypeStruct((n, d), F32), jax.ShapeDtypeStruct((n * d // LANES, LANES), F32),
                   jax.ShapeDtypeStruct((n, LANES), I32), jax.ShapeDtypeStruct((1, LANES), I32)],
        scratch_shapes=[pltpu.VMEM((1, LANES), F32)],
        compiler_params=_params("arbitrary"),
        name="router",
    )(x, a, wo, g.reshape(1, d), wr_hi, wr_lo)


def _experts_kernel(be_ref, nb_ref, src0_ref, srcn_ref, slot_ref, h_hbm, wg_ref, wu_ref, wd_ref,
                    y_hbm, in_ref, out_ref, acc_ref, hb_ref, gsem, ssem, *, rps, nf, n_tok, second):
    i = pl.program_id(0)
    f = pl.program_id(1)
    nb = nb_ref[0]
    tm, d = hb_ref.shape
    pieces = d // LANES
    assert rps * nf == tm

    def tile(r):
        return pl.ds(pl.multiple_of(r * pieces, pieces), pieces)

    def gather_row(src_ref, r):
        return pltpu.make_async_copy(h_hbm.at[tile(src_ref[r])], in_ref.at[tile(r)], gsem)

    def scatter_row(r):
        return pltpu.make_async_copy(out_ref.at[tile(r)], y_hbm.at[tile(slot_ref[r])], ssem)

    def spare_fill(region, k):
        rows = pl.ds((region * second + n_tok + k * tm) * pieces, tm * pieces)
        return pltpu.make_async_copy(out_ref, y_hbm.at[rows], ssem)

    def wait_gather():
        pltpu.make_async_copy(h_hbm.at[pl.ds(0, tm * pieces)], in_ref, gsem).wait()

    def wait_scatter():
        pltpu.make_async_copy(out_ref, y_hbm.at[pl.ds(0, tm * pieces)], ssem).wait()

    @pl.when((i == 0) & (f == 0))
    def _():
        out_ref[...] = jnp.zeros_like(out_ref)
        fills = [spare_fill(region, k) for region in range(2) for k in range((second - n_tok) // tm)]
        for cp in fills:
            cp.start()
        for cp in fills:
            cp.wait()

        def body(it, carry):
            for u in range(8):
                gather_row(src0_ref, it * 8 + u).start()
            return carry

        lax.fori_loop(0, tm // 8, body, 0)

    def start_rows(with_gather):
        for u in range(rps):
            r = f * rps + u
            if with_gather:
                gather_row(srcn_ref, r).start()
            scatter_row(r).start()

    @pl.when(i < nb)
    def _():
        @pl.when(f == 0)
        def _():
            wait_gather()
            for j in range(pieces):
                hb_ref[:, j * LANES:(j + 1) * LANES] = _row_tile_piece(in_ref, j, tm, pieces).astype(BF16)
            acc_ref[...] = jnp.zeros_like(acc_ref)

        start_rows(True)
        h = hb_ref[...]
        act = _silu(_dot(h, wg_ref[...])) * _dot(h, wu_ref[...])
        acc_ref[...] += _dot(act.astype(BF16), wd_ref[...])

        @pl.when(f == nf - 1)
        def _():
            wait_scatter()
            _to_row_tiles(out_ref, acc_ref[...])

    @pl.when(i == nb)
    def _():
        @pl.when(f == 0)
        def _():
            wait_gather()

        start_rows(False)

        @pl.when(f == nf - 1)
        def _():
            wait_scatter()


def _experts(h, src, slot, blk_e, nb_used, w_gate, w_up, w_down, tm, second, tf=896):
    d = w_gate.shape[1]
    pieces = d // LANES
    n = h.shape[0] // pieces
    ff = w_gate.shape[2]
    nf = ff // tf
    n_blocks = src.shape[0] // tm
    rps = tm // nf

    def clamp(i, f, be, nb):
        live = i < nb[0]
        return be[jnp.minimum(i, nb[0] - 1)], jnp.where(live, f, nf - 1)

    def wg_map(i, f, be, nb):
        e, fe = clamp(i, f, be, nb)
        return (e, 0, fe)

    def wd_map(i, f, be, nb):
        e, fe = clamp(i, f, be, nb)
        return (e, fe, 0)

    smem = lambda index_map: pl.BlockSpec((tm,), index_map, memory_space=pltpu.SMEM)
    grid_spec = pltpu.PrefetchScalarGridSpec(
        num_scalar_prefetch=2,
        grid=(n_blocks + 1, nf),
        in_specs=[smem(lambda i, f, be, nb: (0,)),
                  smem(lambda i, f, be, nb: (jnp.minimum(i + 1, nb[0] - 1),)),
                  smem(lambda i, f, be, nb: (jnp.clip(i - 1, 0, nb[0] - 1),)),
                  pl.BlockSpec(memory_space=pl.ANY),
                  pl.BlockSpec((None, d, tf), wg_map),
                  pl.BlockSpec((None, d, tf), wg_map),
                  pl.BlockSpec((None, tf, d), wd_map)],
        out_specs=pl.BlockSpec(memory_space=pl.ANY),
        scratch_shapes=[pltpu.VMEM((tm * pieces, LANES), F32), pltpu.VMEM((tm * pieces, LANES), F32),
                        pltpu.VMEM((tm, d), F32), pltpu.VMEM((tm, d), BF16),
                        pltpu.SemaphoreType.DMA(()), pltpu.SemaphoreType.DMA(())],
    )
    kern = functools.partial(_experts_kernel, rps=rps, nf=nf, n_tok=n, second=second)
    return pl.pallas_call(
        kern,
        grid_spec=grid_spec,
        out_shape=jax.ShapeDtypeStruct((2 * second * pieces, LANES), F32),
        compiler_params=_params("arbitrary", "arbitrary"),
        name="moe_experts",
    )(blk_e, nb_used, src, src, slot, h, w_gate, w_up, w_down)


def _combine_kernel(x_ref, route_ref, g_ref, ya_ref, yb_ref, o_ref):
    route = route_ref[...]
    lane = lax.broadcasted_iota(I32, route.shape, 1)
    gates = lax.bitcast_convert_type(route, F32)
    g1 = jnp.sum(jnp.where(lane == ROUTE_GATE, gates, 0.0), axis=-1, keepdims=True)
    g2 = jnp.sum(jnp.where(lane == ROUTE_GATE + 1, gates, 0.0), axis=-1, keepdims=True)
    tm, d = x_ref.shape
    pieces = d // LANES
    x4 = []
    for j in range(pieces):
        ya = _row_tile_piece(ya_ref, j, tm, pieces)
        yb = _row_tile_piece(yb_ref, j, tm, pieces)
        x4.append(x_ref[:, j * LANES:(j + 1) * LANES] + (ya * g1 + yb * g2))
    ms = sum(jnp.sum(p * p, axis=-1, keepdims=True) for p in x4) * (1.0 / d)
    inv = lax.rsqrt(ms + EPS)
    for j in range(pieces):
        cols = slice(j * LANES, (j + 1) * LANES)
        o_ref[:, cols] = x4[j] * inv * g_ref[:, cols]


def _combine(x, route, y, g, second, tm=512):
    n, d = x.shape
    off = second // tm
    pieces = d // LANES
    return pl.pallas_call(
        _combine_kernel,
        grid=(n // tm,),
        in_specs=[pl.BlockSpec((tm, d), lambda i: (i, 0)),
                  pl.BlockSpec((tm, LANES), lambda i: (i, 0)),
                  pl.BlockSpec((1, d), lambda i: (0, 0)),
                  pl.BlockSpec((tm * pieces, LANES), lambda i: (i, 0)),
                  pl.BlockSpec((tm * pieces, LANES), lambda i: (i + off, 0))],
        out_specs=pl.BlockSpec((tm, d), lambda i: (i, 0)),
        out_shape=jax.ShapeDtypeStruct((n, d), F32),
        compiler_params=_params("parallel"),
        name="moe_combine",
    )(x, route, g.reshape(1, d), y, y)


def _moe_layout(route, counts, n, tm):
    e = route[:, ROUTE_E:ROUTE_E + 2]
    rank = route[:, ROUTE_RANK:ROUTE_RANK + 2]
    sizes = counts[0, :N_EXPERTS]
    padded = ((sizes + tm - 1) // tm) * tm
    pend = jnp.cumsum(padded)
    pstart = pend - padded
    start_of = jnp.zeros_like(e)
    for j in range(N_EXPERTS):
        start_of = jnp.where(e == j, pstart[j], start_of)
    dest = (start_of + rank).astype(I32)
    n_blocks = (2 * n) // tm + N_EXPERTS
    rows = n_blocks * tm
    blk_e = jnp.minimum(jnp.searchsorted(pend, jnp.arange(n_blocks, dtype=I32) * tm, side='right'),
                        N_EXPERTS - 1).astype(I32)
    nb_used = (pend[-1:] // tm).astype(I32)
    second = n + N_EXPERTS * tm
    pair_slot = jnp.arange(n, dtype=I32)[:, None] + jnp.array([0, second], I32)[None, :]
    row = jnp.arange(rows, dtype=I32)
    row_e = jnp.repeat(blk_e, tm)
    pad_j = jnp.clip(row - (pstart + sizes)[row_e], 0, tm - 1)
    spare = n + row_e * tm + pad_j
    slot = spare.at[dest.reshape(-1)].set(pair_slot.reshape(-1))
    token = slot % second
    src = jnp.where(token < n, token, 0)
    return src, slot, blk_e, nb_used, second


def kernel(x, positions, g_mix, g_ffn, g_final, w_in_attn, w_out_attn, attn_sinks, w_in_rec,
           rec_lower_bounds, rec_norm_g, w_out_rec, w_gate_dense, w_up_dense, w_down_dense,
           w_router, w_gate_moe, w_up_moe, w_down_moe):
    bsz, seq, d = x.shape
    n = bsz * seq
    x0 = x.reshape(n, d)
    bf = lambda w: w.astype(BF16)

    cos_t, sin_t = _rope_tables(positions)
    qkv1, qkv4, qkv16, qkvb = _attn_inproj(x, g_mix[0], bf(w_in_attn[0]), cos_t, sin_t)
    mix_a = _dilated_attention(qkv1, qkv4, qkv16)
    mix_b = _swa_gqa(qkvb, attn_sinks[0])
    a_w = A_HEADS * HEAD_DIM
    w_out = bf(w_out_attn[0])
    x2 = _outproj_ffn(x0, mix_a.reshape(n, -1), mix_b.reshape(n, -1), w_out[:a_w], w_out[a_w:],
                      g_ffn[0], bf(w_gate_dense[0]), bf(w_up_dense[0]), bf(w_down_dense[0]))

    lb = jax.nn.softmax(rec_lower_bounds.astype(F32), axis=0)
    lb1 = (jnp.cumsum(lb, axis=0) - lb[0])[1]
    kw = lb1.shape[0]
    vw = rec_norm_g.shape[1]
    q, k, lf, v, gate = _rec_inproj(x2, g_mix[1], bf(w_in_rec[0]), lb1, kw, vw)
    rec = _hgrn(q, k, lf, v, gate, rec_norm_g[0], bsz, seq)
    x3, h, route, counts = _router(x2, rec, bf(w_out_rec[0]), g_ffn[1], w_router[0])
    tm_moe = min(1024, n)
    src, slot, blk_e, nb_used, second = _moe_layout(route, counts, n, tm_moe)
    ys = _experts(h, src, slot, blk_e, nb_used, bf(w_gate_moe[0]), bf(w_up_moe[0]),
                  bf(w_down_moe[0]), tm_moe, second)
    out = _combine(x3, route, ys, g_final, second)
    return out.reshape(bsz, seq, d)
```

```python
import functools

import jax
import jax.numpy as jnp
from jax import lax
from jax.experimental import pallas as pl
from jax.experimental.pallas import tpu as pltpu

F32 = jnp.float32
BF16 = jnp.bfloat16
I32 = jnp.int32

EPS = 1e-6
HEAD_DIM = 64
ROT_DIM = HEAD_DIM // 4
ROPE_THETA = 500000.0
LANES = 128
ATTN_BLOCK = 128
ATTN_UNROLL = 15
A_HEADS = 8
A_PATTERNS = ((128, 1), (512, 4), (2048, 16))
B_Q_HEADS = 8
B_KV_HEADS = 2
B_WINDOW = 128
C_HEADS = 8
N_EXPERTS = 8
REC_CHUNK = 128
REC_SUB = 16
REC_HEADS = 4
ROUTER_SUB = 256
REC_SAFE_DROP = -80.0
VMEM_LIMIT = 56 * 1024 * 1024


def _params(*sem):
    return pltpu.CompilerParams(dimension_semantics=sem, vmem_limit_bytes=VMEM_LIMIT)


def _rms(x, g):
    return x * lax.rsqrt(jnp.mean(x * x, axis=-1, keepdims=True) + EPS) * g


def _silu(x):
    return x / (1.0 + jnp.exp(-x))


def _dot(a, b):
    return jnp.dot(a, b, preferred_element_type=F32)


def _dot_nt(a, b):
    return lax.dot_general(a, b, (((1,), (1,)), ((), ())), preferred_element_type=F32)


def _dot_tn(a, b):
    return lax.dot_general(a, b, (((0,), (0,)), ((), ())), preferred_element_type=F32)


def _rows(j, size, count=1):
    if isinstance(j, int):
        return slice(j * size, (j + count) * size)
    return pl.ds(pl.multiple_of(j * size, size), count * size)


def _to_row_tiles(ref, x, row0=0):
    pieces = x.shape[1] // LANES
    for j in range(pieces):
        ref[pl.ds(row0 * pieces + j, x.shape[0], stride=pieces), :] = x[:, j * LANES:(j + 1) * LANES]


def _row_tile_piece(ref, j, m, pieces):
    return ref[pl.ds(j, m, stride=pieces), :]


def _rope_table_kernel(pos_ref, invf_ref, cos_ref, sin_ref):
    ang = pos_ref[...].astype(F32) * invf_ref[...]
    cos_ref[...] = jnp.cos(ang)
    sin_ref[...] = jnp.sin(ang)


def _rope_tables(positions):
    n = positions.size
    half = ROT_DIM // 2
    per_row = LANES // half
    inv_freq = jnp.power(ROPE_THETA, -jnp.arange(0, ROT_DIM, 2, dtype=F32) / ROT_DIM)
    pos_rep = jnp.repeat(positions.reshape(n // per_row, per_row), half, axis=1)
    invf_row = jnp.tile(inv_freq, per_row).reshape(1, LANES)
    rows = n // per_row
    tr = min(rows, 512)
    cos8, sin8 = pl.pallas_call(
        _rope_table_kernel,
        grid=(rows // tr,),
        in_specs=[pl.BlockSpec((tr, LANES), lambda i: (i, 0)),
                  pl.BlockSpec((1, LANES), lambda i: (0, 0))],
        out_specs=[pl.BlockSpec((tr, LANES), lambda i: (i, 0))] * 2,
        out_shape=[jax.ShapeDtypeStruct((rows, LANES), F32)] * 2,
        compiler_params=_params("parallel"),
        name="rope_tables",
    )(pos_rep, invf_row)
    cos8 = cos8.reshape(n, half)
    sin8 = sin8.reshape(n, half)
    rest = HEAD_DIM - ROT_DIM
    c64 = jnp.concatenate([cos8, cos8, jnp.ones((n, rest), F32)], axis=1)
    s64 = jnp.concatenate([-sin8, sin8, jnp.zeros((n, rest), F32)], axis=1)
    return jnp.tile(c64, (1, 2)), jnp.tile(s64, (1, 2))


A_BLOCKS = A_HEADS * HEAD_DIM // LANES
BQ_BLOCKS = B_Q_HEADS * HEAD_DIM // LANES
A_QKV = 3 * A_BLOCKS * LANES
B_QKV = (BQ_BLOCKS + 2 * B_KV_HEADS) * LANES


def _attn_inproj_kernel(x_ref, g_ref, w_ref, c_ref, s_ref, o1_ref, o4_ref, o16_ref, ob_ref, sc_ref):
    h = _rms(x_ref[0], g_ref[...]).astype(BF16)
    acc = _dot(h, w_ref[...])
    tm = acc.shape[0]
    c = c_ref[0]
    s = s_ref[0]
    lane = lax.broadcasted_iota(I32, c.shape, 1)
    first = (lane % HEAD_DIM) < (ROT_DIM // 2)
    lo_half = lane < HEAD_DIM

    def rope(blk):
        up = pltpu.roll(blk, LANES - ROT_DIM // 2, 1)
        dn = pltpu.roll(blk, ROT_DIM // 2, 1)
        return blk * c + jnp.where(first, up, dn) * s

    def col(cb):
        return acc[:, cb * LANES:(cb + 1) * LANES]

    scale = HEAD_DIM ** -0.5
    for cb in range(3 * A_BLOCKS):
        blk = col(cb)
        if cb < A_BLOCKS:
            blk = rope(blk) * scale
        elif cb < 2 * A_BLOCKS:
            blk = rope(blk)
        sc_ref[cb] = blk
        o1_ref[0, :, cb * LANES:(cb + 1) * LANES] = blk.astype(BF16)
    for o_ref in (o4_ref, o16_ref):
        dil = o_ref.shape[1]
        for r in range(dil):
            for cb in range(3 * A_BLOCKS):
                o_ref[0, r, :, cb * LANES:(cb + 1) * LANES] = (
                    sc_ref[cb, pl.ds(r, tm // dil, stride=dil), :].astype(BF16))
    base = 3 * A_BLOCKS
    for j in range(BQ_BLOCKS):
        ob_ref[0, :, j * LANES:(j + 1) * LANES] = (rope(col(base + j)) * scale).astype(BF16)
    for j, blk in enumerate((rope(col(base + BQ_BLOCKS)), col(base + BQ_BLOCKS + 1))):
        swapped = pltpu.roll(blk, HEAD_DIM, 1)
        for g, dup in enumerate((jnp.where(lo_half, blk, swapped), jnp.where(lo_half, swapped, blk))):
            cb = BQ_BLOCKS + j * B_KV_HEADS + g
            ob_ref[0, :, cb * LANES:(cb + 1) * LANES] = dup.astype(BF16)


def _attn_inproj(x, g, w, cos_t, sin_t, tm=512):
    bsz, seq, d = x.shape
    width = w.shape[1]
    assert B_KV_HEADS * HEAD_DIM == LANES and width == A_QKV + (BQ_BLOCKS + 2) * LANES
    tile = lambda b, i: (b, i, 0)
    fix = lambda b, i: (0, 0)
    d4, d16 = A_PATTERNS[1][1], A_PATTERNS[2][1]
    perm = lambda dil: pl.BlockSpec((1, dil, tm // dil, A_QKV), lambda b, i: (b, 0, i, 0))
    return pl.pallas_call(
        _attn_inproj_kernel,
        grid=(bsz, seq // tm),
        in_specs=[pl.BlockSpec((1, tm, d), tile),
                  pl.BlockSpec((1, d), fix),
                  pl.BlockSpec((d, width), fix),
                  pl.BlockSpec((1, tm, LANES), tile),
                  pl.BlockSpec((1, tm, LANES), tile)],
        out_specs=[pl.BlockSpec((1, tm, A_QKV), tile), perm(d4), perm(d16),
                   pl.BlockSpec((1, tm, B_QKV), tile)],
        out_shape=[jax.ShapeDtypeStruct((bsz, seq, A_QKV), BF16),
                   jax.ShapeDtypeStruct((bsz, d4, seq // d4, A_QKV), BF16),
                   jax.ShapeDtypeStruct((bsz, d16, seq // d16, A_QKV), BF16),
                   jax.ShapeDtypeStruct((bsz, seq, B_QKV), BF16)],
        scratch_shapes=[pltpu.VMEM((3 * A_BLOCKS, tm, LANES), F32)],
        compiler_params=_params("parallel", "parallel"),
        name="attn_inproj",
    )(x, g.reshape(1, d), w, cos_t.reshape(bsz, seq, LANES), sin_t.reshape(bsz, seq, LANES))


def _band_mask(n_back):
    qi = lax.broadcasted_iota(I32, (ATTN_BLOCK, 2 * ATTN_BLOCK), 0)
    kj = lax.broadcasted_iota(I32, (ATTN_BLOCK, 2 * ATTN_BLOCK), 1)
    dist = ATTN_BLOCK + qi - kj
    return (dist >= 0) & (dist <= n_back), kj >= ATTN_BLOCK


def _band_block(q_pairs, kk, vv, valid, sinks=None, want_lse=True):
    lane = lax.broadcasted_iota(I32, (ATTN_BLOCK, LANES), 1)
    lo_half = lane < HEAD_DIM
    zero = jnp.zeros((ATTN_BLOCK, LANES), BF16)
    lhs = []
    for q in q_pairs:
        lhs += [jnp.where(lo_half, q, zero), jnp.where(lo_half, zero, q)]
    s_all = _dot_nt(jnp.concatenate(lhs, axis=0), kk)
    ps, inv_l, lses = [], [], []
    for u in range(len(lhs)):
        s = jnp.where(valid, s_all[u * ATTN_BLOCK:(u + 1) * ATTN_BLOCK], -jnp.inf)
        m = jnp.max(s, axis=-1, keepdims=True)
        if sinks is not None:
            m = jnp.maximum(m, sinks[u])
        e = jnp.exp(s - m)
        l = jnp.sum(e, axis=-1, keepdims=True)
        if sinks is not None:
            l = l + jnp.exp(sinks[u] - m)
        ps.append(e.astype(BF16))
        inv_l.append(1.0 / l)
        lses.append(m + jnp.log(l) if want_lse else None)
    o_all = _dot(jnp.concatenate(ps, axis=0), vv)
    outs = []
    for j in range(len(q_pairs)):
        o0 = o_all[(2 * j) * ATTN_BLOCK:(2 * j + 1) * ATTN_BLOCK] * inv_l[2 * j]
        o1 = o_all[(2 * j + 1) * ATTN_BLOCK:(2 * j + 2) * ATTN_BLOCK] * inv_l[2 * j + 1]
        lse = jnp.where(lo_half, lses[2 * j], lses[2 * j + 1]) if want_lse else None
        outs.append((jnp.where(lo_half, o0, o1), lse))
    return outs


def _for_each_block(nq, first_fn, rest_fn):
    first_fn()
    rest = nq - 1
    if rest <= ATTN_UNROLL:
        for jb in range(1, nq):
            rest_fn(jb)
        return
    assert rest % ATTN_UNROLL == 0

    def body(it, carry):
        for u in range(ATTN_UNROLL):
            rest_fn(1 + it * ATTN_UNROLL + u)
        return carry

    lax.fori_loop(0, rest // ATTN_UNROLL, body, 0)


def _dilated_kernel(q1, kp1, kc1, vp1, vc1, q4, kp4, kc4, vp4, vc4, q16, kp16, kc16, vp16, vc16,
                    o_ref, o4_s, l4_s, o16_s, l16_s, *, n_backs):
    tile = o_ref.shape[1]
    not_first = pl.program_id(2) > 0

    def masks(n_back):
        band, in_cur = _band_mask(n_back)
        return band & (in_cur | not_first), band

    def halo(p_ref, c_ref, idx):
        return jnp.concatenate([p_ref[idx], c_ref[idx + (slice(0, ATTN_BLOCK),)]], axis=0)

    def window(c_ref, idx, jb):
        return c_ref[idx + (_rows(jb - 1, ATTN_BLOCK, 2),)]

    def rows(jb):
        return _rows(jb, ATTN_BLOCK)

    for (q, kp, kc, vp, vc, o_s, l_s), n_back in zip(
            ((q16, kp16, kc16, vp16, vc16, o16_s, l16_s), (q4, kp4, kc4, vp4, vc4, o4_s, l4_s)),
            (n_backs[2], n_backs[1])):
        dil = q.shape[1]
        nq = q.shape[2] // ATTN_BLOCK
        valid0, valid = masks(n_back)
        for r in range(dil):
            idx = (0, r)

            def put(jb, res, r=r, dil=dil, o_s=o_s, l_s=l_s):
                (o, lse), = res
                dst = pl.ds(jb * ATTN_BLOCK * dil + r, ATTN_BLOCK, stride=dil)
                o_s[dst, :] = o
                l_s[dst, :] = lse

            def first(q=q, kp=kp, kc=kc, vp=vp, vc=vc, idx=idx, put=put, valid0=valid0):
                put(0, _band_block([q[idx + (slice(0, ATTN_BLOCK),)]], halo(kp, kc, idx),
                                   halo(vp, vc, idx), valid0))

            def rest(jb, q=q, kc=kc, vc=vc, idx=idx, put=put, valid=valid):
                put(jb, _band_block([q[idx + (rows(jb),)]], window(kc, idx, jb),
                                    window(vc, idx, jb), valid))

            _for_each_block(nq, first, rest)

    valid0, valid = masks(n_backs[0])
    idx = (0,)

    def merge(jb, res):
        (o, lse), = res
        dst = rows(jb)
        l4 = l4_s[dst, :]
        l16 = l16_s[dst, :]
        mx = jnp.maximum(jnp.maximum(lse, l4), l16)
        w1 = jnp.exp(lse - mx)
        w4 = jnp.exp(l4 - mx)
        w16 = jnp.exp(l16 - mx)
        num = w1 * o + w4 * o4_s[dst, :] + w16 * o16_s[dst, :]
        o_ref[0, dst, :] = (num / (w1 + w4 + w16)).astype(o_ref.dtype)

    _for_each_block(
        tile // ATTN_BLOCK,
        lambda: merge(0, _band_block([q1[0, 0:ATTN_BLOCK]], halo(kp1, kc1, idx), halo(vp1, vc1, idx), valid0)),
        lambda jb: merge(jb, _band_block([q1[0, rows(jb)]], window(kc1, idx, jb), window(vc1, idx, jb), valid)))


def _dilated_attention(qkv1, qkv4, qkv16):
    bsz, seq, _ = qkv1.shape
    dils = tuple(p[1] for p in A_PATTERNS)
    assert dils[0] == 1 and qkv4.shape[1] == dils[1] and qkv16.shape[1] == dils[2]
    tile = dils[2] * ATTN_BLOCK
    nt = seq // tile

    def specs(dil):
        rows = tile // dil
        nb = rows // ATTN_BLOCK
        if dil == 1:
            cur = lambda off: pl.BlockSpec((1, rows, LANES), lambda b, hp, i: (b, i, off + hp))
            prev = lambda off: pl.BlockSpec(
                (1, ATTN_BLOCK, LANES), lambda b, hp, i: (b, jnp.maximum(i * nb - 1, 0), off + hp))
        else:
            cur = lambda off: pl.BlockSpec((1, dil, rows, LANES), lambda b, hp, i: (b, 0, i, off + hp))
            prev = lambda off: pl.BlockSpec(
                (1, dil, ATTN_BLOCK, LANES), lambda b, hp, i: (b, 0, jnp.maximum(i * nb - 1, 0), off + hp))
        return [cur(0), prev(A_BLOCKS), cur(A_BLOCKS), prev(2 * A_BLOCKS), cur(2 * A_BLOCKS)]

    kern = functools.partial(_dilated_kernel, n_backs=tuple(w // d for w, d in A_PATTERNS))
    return pl.pallas_call(
        kern,
        grid=(bsz, A_BLOCKS, nt),
        in_specs=specs(1) + specs(dils[1]) + specs(dils[2]),
        out_specs=pl.BlockSpec((1, tile, LANES), lambda b, hp, i: (b, i, hp)),
        out_shape=jax.ShapeDtypeStruct((bsz, seq, A_BLOCKS * LANES), BF16),
        scratch_shapes=[pltpu.VMEM((tile, LANES), F32)] * 4,
        compiler_params=_params("parallel", "parallel", "arbitrary"),
        name="dilated_attn",
    )(*([qkv1] * 5 + [qkv4] * 5 + [qkv16] * 5))


def _swa_kernel(sink_ref, q_ref, *rest, n_back, nq):
    kv = rest[:4 * B_KV_HEADS]
    o_ref = rest[4 * B_KV_HEADS]
    not_first = pl.program_id(1) > 0
    band, in_cur = _band_mask(n_back)
    valid0 = band & (in_cur | not_first)
    pairs = BQ_BLOCKS // B_KV_HEADS

    def run(q_rows, kv_of, valid):
        for g in range(B_KV_HEADS):
            cbs = [g * pairs + j for j in range(pairs)]
            qs = [q_ref[0, q_rows, cb * LANES:(cb + 1) * LANES] for cb in cbs]
            sinks = [sink_ref[2 * cb + p] for cb in cbs for p in range(2)]
            kk, vv = kv_of(g)
            res = _band_block(qs, kk, vv, valid, sinks, want_lse=False)
            for cb, (o, _) in zip(cbs, res):
                o_ref[0, q_rows, cb * LANES:(cb + 1) * LANES] = o.astype(o_ref.dtype)

    def halo_kv(g):
        kp, kc, vp, vc = kv[4 * g:4 * g + 4]
        return (jnp.concatenate([kp[0], kc[0, 0:ATTN_BLOCK]], axis=0),
                jnp.concatenate([vp[0], vc[0, 0:ATTN_BLOCK]], axis=0))

    def window_kv(jb):
        win = _rows(jb - 1, ATTN_BLOCK, 2)
        return lambda g: (kv[4 * g + 1][0, win], kv[4 * g + 3][0, win])

    _for_each_block(
        nq,
        lambda: run(_rows(0, ATTN_BLOCK), halo_kv, valid0),
        lambda jb: run(_rows(jb, ATTN_BLOCK), window_kv(jb), band))


def _swa_gqa(qkvb, sinks, tq=512):
    bsz, seq, _ = qkvb.shape
    tq = min(seq, tq)
    nq = tq // ATTN_BLOCK
    bq_w = BQ_BLOCKS * LANES
    in_specs = [pl.BlockSpec(memory_space=pltpu.SMEM),
                pl.BlockSpec((1, tq, bq_w), lambda b, i: (b, i, 0))]
    for g in range(B_KV_HEADS):
        for section in range(2):
            cb = BQ_BLOCKS + section * B_KV_HEADS + g
            in_specs += [pl.BlockSpec((1, ATTN_BLOCK, LANES),
                                      lambda b, i, cb=cb: (b, jnp.maximum(i * nq - 1, 0), cb)),
                         pl.BlockSpec((1, tq, LANES), lambda b, i, cb=cb: (b, i, cb))]
    args = [sinks.astype(F32)] + [qkvb] * (1 + 4 * B_KV_HEADS)
    kern = functools.partial(_swa_kernel, n_back=B_WINDOW - 1, nq=nq)
    return pl.pallas_call(
        kern,
        grid=(bsz, seq // tq),
        in_specs=in_specs,
        out_specs=pl.BlockSpec((1, tq, bq_w), lambda b, i: (b, i, 0)),
        out_shape=jax.ShapeDtypeStruct((bsz, seq, bq_w), BF16),
        compiler_params=_params("parallel", "arbitrary"),
        name="swa_gqa",
    )(*args)


def _outproj_ffn_kernel(x_ref, a_ref, b_ref, wa_ref, wb_ref, g_ref, wg_ref, wu_ref, wd_ref,
                        o_ref, h_ref):
    f = pl.program_id(1)

    @pl.when(f == 0)
    def _():
        x1 = x_ref[...] + _dot(a_ref[...], wa_ref[...]) + _dot(b_ref[...], wb_ref[...])
        o_ref[...] = x1
        h_ref[...] = _rms(x1, g_ref[...]).astype(BF16)

    h = h_ref[...]
    act = _silu(_dot(h, wg_ref[...])) * _dot(h, wu_ref[...])
    o_ref[...] += _dot(act.astype(BF16), wd_ref[...])


def _outproj_ffn(x, mix_a, mix_b, w_a, w_b, g, w_gate, w_up, w_down, tm=1024, tf=512):
    n, d = x.shape
    ff = w_gate.shape[1]
    ka, kb = mix_a.shape[1], mix_b.shape[1]
    return pl.pallas_call(
        _outproj_ffn_kernel,
        grid=(n // tm, ff // tf),
        in_specs=[pl.BlockSpec((tm, d), lambda i, f: (i, 0)),
                  pl.BlockSpec((tm, ka), lambda i, f: (i, 0)),
                  pl.BlockSpec((tm, kb), lambda i, f: (i, 0)),
                  pl.BlockSpec((ka, d), lambda i, f: (0, 0)),
                  pl.BlockSpec((kb, d), lambda i, f: (0, 0)),
                  pl.BlockSpec((1, d), lambda i, f: (0, 0)),
                  pl.BlockSpec((d, tf), lambda i, f: (0, f)),
                  pl.BlockSpec((d, tf), lambda i, f: (0, f)),
                  pl.BlockSpec((tf, d), lambda i, f: (f, 0))],
        out_specs=pl.BlockSpec((tm, d), lambda i, f: (i, 0)),
        out_shape=jax.ShapeDtypeStruct((n, d), F32),
        scratch_shapes=[pltpu.VMEM((tm, d), BF16)],
        compiler_params=_params("parallel", "arbitrary"),
        name="outproj_ffn",
    )(x, mix_a, mix_b, w_a, w_b, g.reshape(1, d), w_gate, w_up, w_down)


def _rec_inproj_kernel(x_ref, g_ref, w_ref, lb_ref, q_ref, k_ref, lf_ref, v_ref, gate_ref):
    h = _rms(x_ref[...], g_ref[...]).astype(BF16)
    acc = _dot(h, w_ref[...])
    kw = q_ref.shape[1]
    vw = v_ref.shape[1]
    lb = lb_ref[...]
    q_ref[...] = _silu(acc[:, :kw]).astype(BF16)
    fg = lb + (1.0 - lb) / (1.0 + jnp.exp(-acc[:, kw:2 * kw]))
    k_ref[...] = (1.0 - fg).astype(BF16)
    lf_ref[...] = jnp.log(fg)
    v_ref[...] = acc[:, 2 * kw:2 * kw + vw].astype(BF16)
    gate_ref[...] = _silu(acc[:, 2 * kw + vw:]).astype(BF16)


def _rec_inproj(x, g, w, lb, kw, vw, tm=512):
    n, d = x.shape
    width = w.shape[1]
    row = lambda i: (i, 0)
    fix = lambda i: (0, 0)
    return pl.pallas_call(
        _rec_inproj_kernel,
        grid=(n // tm,),
        in_specs=[pl.BlockSpec((tm, d), row),
                  pl.BlockSpec((1, d), fix),
                  pl.BlockSpec((d, width), fix),
                  pl.BlockSpec((1, kw), fix)],
        out_specs=[pl.BlockSpec((tm, kw), row), pl.BlockSpec((tm, kw), row),
                   pl.BlockSpec((tm, kw), row), pl.BlockSpec((tm, vw), row),
                   pl.BlockSpec((tm, vw), row)],
        out_shape=[jax.ShapeDtypeStruct((n, kw), BF16), jax.ShapeDtypeStruct((n, kw), BF16),
                   jax.ShapeDtypeStruct((n, kw), F32), jax.ShapeDtypeStruct((n, vw), BF16),
                   jax.ShapeDtypeStruct((n, vw), BF16)],
        compiler_params=_params("parallel"),
        name="rec_inproj",
    )(x, g.reshape(1, d), w, lb.reshape(1, kw))


def _hgrn_kernel(q_ref, k_ref, lf_ref, v_ref, gate_ref, ng_ref, o_ref, st_ref, b_ref, kf_ref, *,
                 n_chunks):
    @pl.when(pl.program_id(2) == 0)
    def _():
        st_ref[...] = jnp.zeros_like(st_ref)

    c_len, sub = REC_CHUNK, REC_SUB
    heads = [slice(hh * LANES, (hh + 1) * LANES) for hh in range(q_ref.shape[2] // LANES)]
    r_i = lax.broadcasted_iota(I32, (c_len, c_len), 0)
    c_i = lax.broadcasted_iota(I32, (c_len, c_len), 1)
    causal = c_i <= r_i
    tril = causal.astype(BF16)

    drop = jnp.zeros((1, LANES), F32)
    for c in range(n_chunks):
        rows = slice(c * c_len, (c + 1) * c_len)
        for hh, cols in enumerate(heads):
            lf = lf_ref[0, rows, cols]
            lf1 = lf.astype(BF16)
            rem = lf - lf1.astype(F32)
            lf2 = rem.astype(BF16)
            lf3 = (rem - lf2.astype(F32)).astype(BF16)
            b = _dot(tril, lf1) + _dot(tril, lf2) + _dot(tril, lf3)
            b_ref[hh, rows, :] = b
            for i in range(c_len // sub):
                end = b[(i + 1) * sub - 1:(i + 1) * sub, :]
                drop = jnp.minimum(drop, end - b[i * sub - 1:i * sub, :] if i > 0 else end)
    safe = jnp.min(drop) > REC_SAFE_DROP

    def finish(c, hh, att, st):
        rows = _rows(c, c_len)
        cols = heads[hh]
        b = b_ref[hh, rows, :]
        q = q_ref[0, rows, cols].astype(F32)
        k = k_ref[0, rows, cols].astype(F32)
        v = v_ref[0, rows, cols]
        b_last = b[c_len - 1:c_len, :]
        qe = (q * jnp.exp(b)).astype(BF16)
        o = _dot(att.astype(BF16), v) + _dot_nt(qe, st.astype(BF16))
        kd = (k * jnp.exp(b_last - b)).astype(BF16)
        st = st * jnp.exp(b_last) + _dot_tn(v, kd)
        y = o * lax.rsqrt(jnp.mean(o * o, axis=-1, keepdims=True) + EPS)
        o_ref[0, rows, cols] = (y * ng_ref[:, cols] * gate_ref[0, rows, cols].astype(F32)).astype(o_ref.dtype)
        return st

    @pl.when(safe)
    def _():
        sts = [st_ref[hh] for hh in range(len(heads))]
        for c in range(n_chunks):
            rows = slice(c * c_len, (c + 1) * c_len)
            for hh, cols in enumerate(heads):
                b = b_ref[hh, rows, :]
                q = q_ref[0, rows, cols].astype(F32)
                k = k_ref[0, rows, cols].astype(F32)
                att_rows = []
                for i in range(c_len // sub):
                    lo, hi = i * sub, (i + 1) * sub
                    ref = b[lo - 1:lo, :] if i > 0 else jnp.zeros((1, LANES), F32)
                    qt = (q[lo:hi] * jnp.exp(b[lo:hi] - ref)).astype(BF16)
                    kt = (k[:hi] * jnp.exp(ref - b[:hi])).astype(BF16)
                    if hi < c_len:
                        kt = jnp.concatenate([kt, jnp.zeros((c_len - hi, LANES), BF16)], axis=0)
                    att_rows.append(_dot_nt(qt, kt))
                att = jnp.where(causal, jnp.concatenate(att_rows, axis=0), 0.0)
                sts[hh] = finish(c, hh, att, sts[hh])
        for hh in range(len(heads)):
            st_ref[hh] = sts[hh]

    @pl.when(jnp.logical_not(safe))
    def _():
        for hh, cols in enumerate(heads):
            def chunk(c, st, hh=hh, cols=cols):
                rows = _rows(c, c_len)
                b = b_ref[hh, rows, :]
                q = q_ref[0, rows, cols].astype(F32)
                kf_ref[...] = k_ref[0, rows, cols].astype(F32)

                def key(s, att):
                    b_s = b_ref[hh, pl.ds(c * c_len + s, 1), :]
                    dec = jnp.exp(jnp.minimum(b - b_s, 0.0))
                    col = jnp.sum(q * kf_ref[pl.ds(s, 1), :] * dec, axis=-1, keepdims=True)
                    return jnp.where(c_i == s, col, att)

                att = lax.fori_loop(0, c_len, key, jnp.zeros((c_len, c_len), F32))
                return finish(c, hh, jnp.where(causal, att, 0.0), st)

            st_ref[hh] = lax.fori_loop(0, n_chunks, chunk, st_ref[hh])


def _hgrn(q, k, lf, v, gate, norm_g, bsz, seq, ts=512):
    kw = q.shape[-1] // C_HEADS
    vw = v.shape[-1] // C_HEADS
    assert kw == LANES and vw == LANES
    ts = min(ts, seq)
    v3 = lambda t: t.reshape(bsz, seq, t.shape[-1])
    wide = REC_HEADS * LANES
    blk = pl.BlockSpec((1, ts, wide), lambda b, h, c: (b, c, h))
    kern = functools.partial(_hgrn_kernel, n_chunks=ts // REC_CHUNK)
    out = pl.pallas_call(
        kern,
        grid=(bsz, C_HEADS // REC_HEADS, seq // ts),
        in_specs=[blk, blk, blk, blk, blk, pl.BlockSpec((1, wide), lambda b, h, c: (0, h))],
        out_specs=blk,
        out_shape=jax.ShapeDtypeStruct((bsz, seq, C_HEADS * vw), BF16),
        scratch_shapes=[pltpu.VMEM((REC_HEADS, vw, kw), F32), pltpu.VMEM((REC_HEADS, ts, LANES), F32),
                        pltpu.VMEM((REC_CHUNK, LANES), F32)],
        compiler_params=_params("parallel", "parallel", "arbitrary"),
        name="hgrn2",
    )(v3(q), v3(k), v3(lf), v3(v), v3(gate), norm_g.reshape(1, -1))
    return out.reshape(bsz * seq, C_HEADS * vw)


ROUTE_E, ROUTE_RANK, ROUTE_GATE = 0, 2, 4


def _router_kernel(x_ref, a_ref, wo_ref, g_ref, wr_hi_ref, wr_lo_ref, x3_ref, h_ref, route_ref,
                   cnt_ref, base_ref):
    @pl.when(pl.program_id(0) == 0)
    def _():
        base_ref[...] = jnp.zeros_like(base_ref)

    sub = min(ROUTER_SUB, x_ref.shape[0])
    lane = lax.broadcasted_iota(I32, (sub, LANES), 1)
    lane_f = lane.astype(F32)
    r_i = lax.broadcasted_iota(I32, (sub, sub), 0)
    c_i = lax.broadcasted_iota(I32, (sub, sub), 1)
    earlier = (c_i < r_i).astype(BF16)
    base = base_ref[...]
    for r0 in range(0, x_ref.shape[0], sub):
        rows = slice(r0, r0 + sub)
        x3 = x_ref[rows, :] + _dot(a_ref[rows, :], wo_ref[...])
        x3_ref[rows, :] = x3
        h = _rms(x3, g_ref[...])
        _to_row_tiles(h_ref, h, r0)
        h_hi = h.astype(BF16)
        h_lo = (h - h_hi.astype(F32)).astype(BF16)
        logits = _dot(h_hi, wr_hi_ref[...]) + _dot(h_hi, wr_lo_ref[...]) + _dot(h_lo, wr_hi_ref[...])
        logits = jnp.where(lane < N_EXPERTS, logits, -jnp.inf)
        v1 = jnp.max(logits, axis=-1, keepdims=True)
        e1 = jnp.min(jnp.where(logits == v1, lane_f, float(LANES)), axis=-1, keepdims=True)
        hot1 = lane_f == e1
        rest = jnp.where(hot1, -jnp.inf, logits)
        v2 = jnp.max(rest, axis=-1, keepdims=True)
        e2 = jnp.min(jnp.where(rest == v2, lane_f, float(LANES)), axis=-1, keepdims=True)
        hot2 = lane_f == e2
        t = jnp.exp(v2 - v1)
        g1 = 1.0 / (1.0 + t)
        g2 = t / (1.0 + t)
        member = (hot1 | hot2).astype(BF16)
        before = _dot(earlier, member) + base
        rank1 = jnp.sum(jnp.where(hot1, before, 0.0), axis=-1, keepdims=True).astype(I32)
        rank2 = jnp.sum(jnp.where(hot2, before, 0.0), axis=-1, keepdims=True).astype(I32)
        base = base + jnp.sum(member.astype(F32), axis=0, keepdims=True)
        route = jnp.where(lane == ROUTE_E, e1.astype(I32), 0)
        route = jnp.where(lane == ROUTE_E + 1, e2.astype(I32), route)
        route = jnp.where(lane == ROUTE_RANK, rank1, route)
        route = jnp.where(lane == ROUTE_RANK + 1, rank2, route)
        gate_bits = lax.bitcast_convert_type(jnp.where(lane == ROUTE_GATE, g1, g2), I32)
        route = jnp.where((lane == ROUTE_GATE) | (lane == ROUTE_GATE + 1), gate_bits, route)
        route_ref[rows, :] = route
    base_ref[...] = base
    cnt_ref[...] = base.astype(I32)


def _router(x, a, wo, g, w_router, tm=512):
    n, d = x.shape
    ka = a.shape[1]
    wr = jnp.zeros((d, LANES), F32).at[:, :N_EXPERTS].set(w_router)
    wr_hi = wr.astype(BF16)
    wr_lo = (wr - wr_hi.astype(F32)).astype(BF16)
    row = lambda i: (i, 0)
    fix = lambda i: (0, 0)
    return pl.pallas_call(
        _router_kernel,
        grid=(n // tm,),
        in_specs=[pl.BlockSpec((tm, d), row), pl.BlockSpec((tm, ka), row),
                  pl.BlockSpec((ka, d), fix), pl.BlockSpec((1, d), fix),
                  pl.BlockSpec((d, LANES), fix), pl.BlockSpec((d, LANES), fix)],
        out_specs=[pl.BlockSpec((tm, d), row), pl.BlockSpec((tm * d // LANES, LANES), row),
                   pl.BlockSpec((tm, LANES), row), pl.BlockSpec((1, LANES), fix)],
        out_shape=[jax.ShapeDtypeStruct((n, d), F32), jax.ShapeDtypeStruct((n * d // LANES, LANES), F32),
                   jax.ShapeDtypeStruct((n, LANES), I32), jax.ShapeDtypeStruct((1, LANES), I32)],
        scratch_shapes=[pltpu.VMEM((1, LANES), F32)],
        compiler_params=_params("arbitrary"),
        name="router",
    )(x, a, wo, g.reshape(1, d), wr_hi, wr_lo)


def _experts_kernel(be_ref, nb_ref, src0_ref, srcn_ref, slot_ref, h_hbm, wg_ref, wu_ref, wd_ref,
                    y_hbm, in_ref, out_ref, acc_ref, hb_ref, gsem, ssem, *, rps, nf, n_tok, second):
    i = pl.program_id(0)
    f = pl.program_id(1)
    nb = nb_ref[0]
    tm, d = hb_ref.shape
    pieces = d // LANES
    assert rps * nf == tm

    def tile(r):
        return pl.ds(pl.multiple_of(r * pieces, pieces), pieces)

    def gather_row(src_ref, r):
        return pltpu.make_async_copy(h_hbm.at[tile(src_ref[r])], in_ref.at[tile(r)], gsem)

    def scatter_row(r):
        return pltpu.make_async_copy(out_ref.at[tile(r)], y_hbm.at[tile(slot_ref[r])], ssem)

    def spare_fill(region, k):
        rows = pl.ds((region * second + n_tok + k * tm) * pieces, tm * pieces)
        return pltpu.make_async_copy(out_ref, y_hbm.at[rows], ssem)

    def wait_gather():
        pltpu.make_async_copy(h_hbm.at[pl.ds(0, tm * pieces)], in_ref, gsem).wait()

    def wait_scatter():
        pltpu.make_async_copy(out_ref, y_hbm.at[pl.ds(0, tm * pieces)], ssem).wait()

    @pl.when((i == 0) & (f == 0))
    def _():
        out_ref[...] = jnp.zeros_like(out_ref)
        fills = [spare_fill(region, k) for region in range(2) for k in range((second - n_tok) // tm)]
        for cp in fills:
            cp.start()
        for cp in fills:
            cp.wait()

        def body(it, carry):
            for u in range(8):
                gather_row(src0_ref, it * 8 + u).start()
            return carry

        lax.fori_loop(0, tm // 8, body, 0)

    def start_rows(with_gather):
        for u in range(rps):
            r = f * rps + u
            if with_gather:
                gather_row(srcn_ref, r).start()
            scatter_row(r).start()

    @pl.when(i < nb)
    def _():
        @pl.when(f == 0)
        def _():
            wait_gather()
            for j in range(pieces):
                hb_ref[:, j * LANES:(j + 1) * LANES] = _row_tile_piece(in_ref, j, tm, pieces).astype(BF16)
            acc_ref[...] = jnp.zeros_like(acc_ref)

        start_rows(True)
        h = hb_ref[...]
        act = _silu(_dot(h, wg_ref[...])) * _dot(h, wu_ref[...])
        acc_ref[...] += _dot(act.astype(BF16), wd_ref[...])

        @pl.when(f == nf - 1)
        def _():
            wait_scatter()
            _to_row_tiles(out_ref, acc_ref[...])

    @pl.when(i == nb)
    def _():
        @pl.when(f == 0)
        def _():
            wait_gather()

        start_rows(False)

        @pl.when(f == nf - 1)
        def _():
            wait_scatter()


def _experts(h, src, slot, blk_e, nb_used, w_gate, w_up, w_down, tm, second, tf=896):
    d = w_gate.shape[1]
    pieces = d // LANES
    n = h.shape[0] // pieces
    ff = w_gate.shape[2]
    nf = ff // tf
    n_blocks = src.shape[0] // tm
    rps = tm // nf

    def clamp(i, f, be, nb):
        live = i < nb[0]
        return be[jnp.minimum(i, nb[0] - 1)], jnp.where(live, f, nf - 1)

    def wg_map(i, f, be, nb):
        e, fe = clamp(i, f, be, nb)
        return (e, 0, fe)

    def wd_map(i, f, be, nb):
        e, fe = clamp(i, f, be, nb)
        return (e, fe, 0)

    smem = lambda index_map: pl.BlockSpec((tm,), index_map, memory_space=pltpu.SMEM)
    grid_spec = pltpu.PrefetchScalarGridSpec(
        num_scalar_prefetch=2,
        grid=(n_blocks + 1, nf),
        in_specs=[smem(lambda i, f, be, nb: (0,)),
                  smem(lambda i, f, be, nb: (jnp.minimum(i + 1, nb[0] - 1),)),
                  smem(lambda i, f, be, nb: (jnp.clip(i - 1, 0, nb[0] - 1),)),
                  pl.BlockSpec(memory_space=pl.ANY),
                  pl.BlockSpec((None, d, tf), wg_map),
                  pl.BlockSpec((None, d, tf), wg_map),
                  pl.BlockSpec((None, tf, d), wd_map)],
        out_specs=pl.BlockSpec(memory_space=pl.ANY),
        scratch_shapes=[pltpu.VMEM((tm * pieces, LANES), F32), pltpu.VMEM((tm * pieces, LANES), F32),
                        pltpu.VMEM((tm, d), F32), pltpu.VMEM((tm, d), BF16),
                        pltpu.SemaphoreType.DMA(()), pltpu.SemaphoreType.DMA(())],
    )
    kern = functools.partial(_experts_kernel, rps=rps, nf=nf, n_tok=n, second=second)
    return pl.pallas_call(
        kern,
        grid_spec=grid_spec,
        out_shape=jax.ShapeDtypeStruct((2 * second * pieces, LANES), F32),
        compiler_params=_params("arbitrary", "arbitrary"),
        name="moe_experts",
    )(blk_e, nb_used, src, src, slot, h, w_gate, w_up, w_down)


def _combine_kernel(x_ref, route_ref, g_ref, ya_ref, yb_ref, o_ref):
    route = route_ref[...]
    lane = lax.broadcasted_iota(I32, route.shape, 1)
    gates = lax.bitcast_convert_type(route, F32)
    g1 = jnp.sum(jnp.where(lane == ROUTE_GATE, gates, 0.0), axis=-1, keepdims=True)
    g2 = jnp.sum(jnp.where(lane == ROUTE_GATE + 1, gates, 0.0), axis=-1, keepdims=True)
    tm, d = x_ref.shape
    pieces = d // LANES
    x4 = []
    for j in range(pieces):
        ya = _row_tile_piece(ya_ref, j, tm, pieces)
        yb = _row_tile_piece(yb_ref, j, tm, pieces)
        x4.append(x_ref[:, j * LANES:(j + 1) * LANES] + (ya * g1 + yb * g2))
    ms = sum(jnp.sum(p * p, axis=-1, keepdims=True) for p in x4) * (1.0 / d)
    inv = lax.rsqrt(ms + EPS)
    for j in range(pieces):
        cols = slice(j * LANES, (j + 1) * LANES)
        o_ref[:, cols] = x4[j] * inv * g_ref[:, cols]


def _combine(x, route, y, g, second, tm=512):
    n, d = x.shape
    off = second // tm
    pieces = d // LANES
    return pl.pallas_call(
        _combine_kernel,
        grid=(n // tm,),
        in_specs=[pl.BlockSpec((tm, d), lambda i: (i, 0)),
                  pl.BlockSpec((tm, LANES), lambda i: (i, 0)),
                  pl.BlockSpec((1, d), lambda i: (0, 0)),
                  pl.BlockSpec((tm * pieces, LANES), lambda i: (i, 0)),
                  pl.BlockSpec((tm * pieces, LANES), lambda i: (i + off, 0))],
        out_specs=pl.BlockSpec((tm, d), lambda i: (i, 0)),
        out_shape=jax.ShapeDtypeStruct((n, d), F32),
        compiler_params=_params("parallel"),
        name="moe_combine",
    )(x, route, g.reshape(1, d), y, y)


def _moe_layout(route, counts, n, tm):
    e = route[:, ROUTE_E:ROUTE_E + 2]
    rank = route[:, ROUTE_RANK:ROUTE_RANK + 2]
    sizes = counts[0, :N_EXPERTS]
    padded = ((sizes + tm - 1) // tm) * tm
    pend = jnp.cumsum(padded)
    pstart = pend - padded
    start_of = jnp.zeros_like(e)
    for j in range(N_EXPERTS):
        start_of = jnp.where(e == j, pstart[j], start_of)
    dest = (start_of + rank).astype(I32)
    n_blocks = (2 * n) // tm + N_EXPERTS
    rows = n_blocks * tm
    blk_e = jnp.minimum(jnp.searchsorted(pend, jnp.arange(n_blocks, dtype=I32) * tm, side='right'),
                        N_EXPERTS - 1).astype(I32)
    nb_used = (pend[-1:] // tm).astype(I32)
    second = n + N_EXPERTS * tm
    pair_slot = jnp.arange(n, dtype=I32)[:, None] + jnp.array([0, second], I32)[None, :]
    row = jnp.arange(rows, dtype=I32)
    row_e = jnp.repeat(blk_e, tm)
    pad_j = jnp.clip(row - (pstart + sizes)[row_e], 0, tm - 1)
    spare = n + row_e * tm + pad_j
    slot = spare.at[dest.reshape(-1)].set(pair_slot.reshape(-1))
    token = slot % second
    src = jnp.where(token < n, token, 0)
    return src, slot, blk_e, nb_used, second


def kernel(x, positions, g_mix, g_ffn, g_final, w_in_attn, w_out_attn, attn_sinks, w_in_rec,
           rec_lower_bounds, rec_norm_g, w_out_rec, w_gate_dense, w_up_dense, w_down_dense,
           w_router, w_gate_moe, w_up_moe, w_down_moe):
    bsz, seq, d = x.shape
    n = bsz * seq
    x0 = x.reshape(n, d)
    bf = lambda w: w.astype(BF16)

    cos_t, sin_t = _rope_tables(positions)
    qkv1, qkv4, qkv16, qkvb = _attn_inproj(x, g_mix[0], bf(w_in_attn[0]), cos_t, sin_t)
    mix_a = _dilated_attention(qkv1, qkv4, qkv16)
    mix_b = _swa_gqa(qkvb, attn_sinks[0])
    a_w = A_HEADS * HEAD_DIM
    w_out = bf(w_out_attn[0])
    x2 = _outproj_ffn(x0, mix_a.reshape(n, -1), mix_b.reshape(n, -1), w_out[:a_w], w_out[a_w:],
                      g_ffn[0], bf(w_gate_dense[0]), bf(w_up_dense[0]), bf(w_down_dense[0]))

    lb = jax.nn.softmax(rec_lower_bounds.astype(F32), axis=0)
    lb1 = (jnp.cumsum(lb, axis=0) - lb[0])[1]
    kw = lb1.shape[0]
    vw = rec_norm_g.shape[1]
    q, k, lf, v, gate = _rec_inproj(x2, g_mix[1], bf(w_in_rec[0]), lb1, kw, vw)
    rec = _hgrn(q, k, lf, v, gate, rec_norm_g[0], bsz, seq)
    x3, h, route, counts = _router(x2, rec, bf(w_out_rec[0]), g_ffn[1], w_router[0])
    tm_moe = min(1024, n)
    src, slot, blk_e, nb_used, second = _moe_layout(route, counts, n, tm_moe)
    ys = _experts(h, src, slot, blk_e, nb_used, bf(w_gate_moe[0]), bf(w_up_moe[0]),
                  bf(w_down_moe[0]), tm_moe, second)
    out = _combine(x3, route, ys, g_final, second)
    return out.reshape(bsz, seq, d)
```

```python
import functools

import jax
import jax.numpy as jnp
from jax import lax
from jax.experimental import pallas as pl
from jax.experimental.pallas import tpu as pltpu

F32 = jnp.float32
BF16 = jnp.bfloat16
I32 = jnp.int32

EPS = 1e-6
HEAD_DIM = 64
ROT_DIM = HEAD_DIM // 4
ROPE_THETA = 500000.0
LANES = 128
ATTN_BLOCK = 128
ATTN_UNROLL = 15
A_HEADS = 8
A_PATTERNS = ((128, 1), (512, 4), (2048, 16))
B_Q_HEADS = 8
B_KV_HEADS = 2
B_WINDOW = 128
C_HEADS = 8
N_EXPERTS = 8
REC_CHUNK = 128
REC_SUB = 16
REC_HEADS = 8
ROUTER_SUB = 128
REC_SAFE_DROP = -80.0
VMEM_LIMIT = 56 * 1024 * 1024


def _params(*sem):
    return pltpu.CompilerParams(dimension_semantics=sem, vmem_limit_bytes=VMEM_LIMIT)


def _rms(x, g):
    return x * lax.rsqrt(jnp.mean(x * x, axis=-1, keepdims=True) + EPS) * g


def _silu(x):
    return x / (1.0 + jnp.exp(-x))


def _dot(a, b):
    return jnp.dot(a, b, preferred_element_type=F32)


def _dot_nt(a, b):
    return lax.dot_general(a, b, (((1,), (1,)), ((), ())), preferred_element_type=F32)


def _dot_tn(a, b):
    return lax.dot_general(a, b, (((0,), (0,)), ((), ())), preferred_element_type=F32)


def _rows(j, size, count=1):
    if isinstance(j, int):
        return slice(j * size, (j + count) * size)
    return pl.ds(pl.multiple_of(j * size, size), count * size)


def _to_row_tiles(ref, x, row0=0):
    pieces = x.shape[1] // LANES
    for j in range(pieces):
        ref[pl.ds(row0 * pieces + j, x.shape[0], stride=pieces), :] = x[:, j * LANES:(j + 1) * LANES]


def _row_tile_piece(ref, j, m, pieces):
    return ref[pl.ds(j, m, stride=pieces), :]


def _rope_table_kernel(pos_ref, invf_ref, cos_ref, sin_ref):
    ang = pos_ref[...].astype(F32) * invf_ref[...]
    cos_ref[...] = jnp.cos(ang)
    sin_ref[...] = jnp.sin(ang)


def _rope_tables(positions):
    n = positions.size
    half = ROT_DIM // 2
    per_row = LANES // half
    inv_freq = jnp.power(ROPE_THETA, -jnp.arange(0, ROT_DIM, 2, dtype=F32) / ROT_DIM)
    pos_rep = jnp.repeat(positions.reshape(n // per_row, per_row), half, axis=1)
    invf_row = jnp.tile(inv_freq, per_row).reshape(1, LANES)
    rows = n // per_row
    tr = min(rows, 512)
    cos8, sin8 = pl.pallas_call(
        _rope_table_kernel,
        grid=(rows // tr,),
        in_specs=[pl.BlockSpec((tr, LANES), lambda i: (i, 0)),
                  pl.BlockSpec((1, LANES), lambda i: (0, 0))],
        out_specs=[pl.BlockSpec((tr, LANES), lambda i: (i, 0))] * 2,
        out_shape=[jax.ShapeDtypeStruct((rows, LANES), F32)] * 2,
        compiler_params=_params("parallel"),
        name="rope_tables",
    )(pos_rep, invf_row)
    cos8 = cos8.reshape(n, half)
    sin8 = sin8.reshape(n, half)
    rest = HEAD_DIM - ROT_DIM
    c64 = jnp.concatenate([cos8, cos8, jnp.ones((n, rest), F32)], axis=1)
    s64 = jnp.concatenate([-sin8, sin8, jnp.zeros((n, rest), F32)], axis=1)
    return jnp.tile(c64, (1, 2)), jnp.tile(s64, (1, 2))


A_BLOCKS = A_HEADS * HEAD_DIM // LANES
BQ_BLOCKS = B_Q_HEADS * HEAD_DIM // LANES
A_QKV = 3 * A_BLOCKS * LANES
B_QKV = (BQ_BLOCKS + 2 * B_KV_HEADS) * LANES


def _attn_inproj_kernel(x_ref, g_ref, w_ref, c_ref, s_ref, o1_ref, o4_ref, o16_ref, ob_ref, sc_ref):
    h = _rms(x_ref[0], g_ref[...]).astype(BF16)
    acc = _dot(h, w_ref[...])
    tm = acc.shape[0]
    c = c_ref[0]
    s = s_ref[0]
    lane = lax.broadcasted_iota(I32, c.shape, 1)
    first = (lane % HEAD_DIM) < (ROT_DIM // 2)
    lo_half = lane < HEAD_DIM

    def rope(blk):
        up = pltpu.roll(blk, LANES - ROT_DIM // 2, 1)
        dn = pltpu.roll(blk, ROT_DIM // 2, 1)
        return blk * c + jnp.where(first, up, dn) * s

    def col(cb):
        return acc[:, cb * LANES:(cb + 1) * LANES]

    scale = HEAD_DIM ** -0.5
    for cb in range(3 * A_BLOCKS):
        blk = col(cb)
        if cb < A_BLOCKS:
            blk = rope(blk) * scale
        elif cb < 2 * A_BLOCKS:
            blk = rope(blk)
        sc_ref[cb] = blk
        o1_ref[0, :, cb * LANES:(cb + 1) * LANES] = blk.astype(BF16)
    for o_ref in (o4_ref, o16_ref):
        dil = o_ref.shape[1]
        for r in range(dil):
            for cb in range(3 * A_BLOCKS):
                o_ref[0, r, :, cb * LANES:(cb + 1) * LANES] = (
                    sc_ref[cb, pl.ds(r, tm // dil, stride=dil), :].astype(BF16))
    base = 3 * A_BLOCKS
    for j in range(BQ_BLOCKS):
        ob_ref[0, :, j * LANES:(j + 1) * LANES] = (rope(col(base + j)) * scale).astype(BF16)
    for j, blk in enumerate((rope(col(base + BQ_BLOCKS)), col(base + BQ_BLOCKS + 1))):
        swapped = pltpu.roll(blk, HEAD_DIM, 1)
        for g, dup in enumerate((jnp.where(lo_half, blk, swapped), jnp.where(lo_half, swapped, blk))):
            cb = BQ_BLOCKS + j * B_KV_HEADS + g
            ob_ref[0, :, cb * LANES:(cb + 1) * LANES] = dup.astype(BF16)


def _attn_inproj(x, g, w, cos_t, sin_t, tm=512):
    bsz, seq, d = x.shape
    width = w.shape[1]
    assert B_KV_HEADS * HEAD_DIM == LANES and width == A_QKV + (BQ_BLOCKS + 2) * LANES
    tile = lambda b, i: (b, i, 0)
    fix = lambda b, i: (0, 0)
    d4, d16 = A_PATTERNS[1][1], A_PATTERNS[2][1]
    perm = lambda dil: pl.BlockSpec((1, dil, tm // dil, A_QKV), lambda b, i: (b, 0, i, 0))
    return pl.pallas_call(
        _attn_inproj_kernel,
        grid=(bsz, seq // tm),
        in_specs=[pl.BlockSpec((1, tm, d), tile),
                  pl.BlockSpec((1, d), fix),
                  pl.BlockSpec((d, width), fix),
                  pl.BlockSpec((1, tm, LANES), tile),
                  pl.BlockSpec((1, tm, LANES), tile)],
        out_specs=[pl.BlockSpec((1, tm, A_QKV), tile), perm(d4), perm(d16),
                   pl.BlockSpec((1, tm, B_QKV), tile)],
        out_shape=[jax.ShapeDtypeStruct((bsz, seq, A_QKV), BF16),
                   jax.ShapeDtypeStruct((bsz, d4, seq // d4, A_QKV), BF16),
                   jax.ShapeDtypeStruct((bsz, d16, seq // d16, A_QKV), BF16),
                   jax.ShapeDtypeStruct((bsz, seq, B_QKV), BF16)],
        scratch_shapes=[pltpu.VMEM((3 * A_BLOCKS, tm, LANES), F32)],
        compiler_params=_params("parallel", "parallel"),
        name="attn_inproj",
    )(x, g.reshape(1, d), w, cos_t.reshape(bsz, seq, LANES), sin_t.reshape(bsz, seq, LANES))


def _band_mask(n_back):
    qi = lax.broadcasted_iota(I32, (ATTN_BLOCK, 2 * ATTN_BLOCK), 0)
    kj = lax.broadcasted_iota(I32, (ATTN_BLOCK, 2 * ATTN_BLOCK), 1)
    dist = ATTN_BLOCK + qi - kj
    return (dist >= 0) & (dist <= n_back), kj >= ATTN_BLOCK


def _band_block(q_pairs, kk, vv, valid, sinks=None, want_lse=True):
    lane = lax.broadcasted_iota(I32, (ATTN_BLOCK, LANES), 1)
    lo_half = lane < HEAD_DIM
    zero = jnp.zeros((ATTN_BLOCK, LANES), BF16)
    lhs = []
    for q in q_pairs:
        lhs += [jnp.where(lo_half, q, zero), jnp.where(lo_half, zero, q)]
    s_all = _dot_nt(jnp.concatenate(lhs, axis=0), kk)
    ps, inv_l, lses = [], [], []
    for u in range(len(lhs)):
        s = jnp.where(valid, s_all[u * ATTN_BLOCK:(u + 1) * ATTN_BLOCK], -jnp.inf)
        m = jnp.max(s, axis=-1, keepdims=True)
        if sinks is not None:
            m = jnp.maximum(m, sinks[u])
        e = jnp.exp(s - m)
        l = jnp.sum(e, axis=-1, keepdims=True)
        if sinks is not None:
            l = l + jnp.exp(sinks[u] - m)
        ps.append(e.astype(BF16))
        inv_l.append(1.0 / l)
        lses.append(m + jnp.log(l) if want_lse else None)
    o_all = _dot(jnp.concatenate(ps, axis=0), vv)
    outs = []
    for j in range(len(q_pairs)):
        o0 = o_all[(2 * j) * ATTN_BLOCK:(2 * j + 1) * ATTN_BLOCK] * inv_l[2 * j]
        o1 = o_all[(2 * j + 1) * ATTN_BLOCK:(2 * j + 2) * ATTN_BLOCK] * inv_l[2 * j + 1]
        lse = jnp.where(lo_half, lses[2 * j], lses[2 * j + 1]) if want_lse else None
        outs.append((jnp.where(lo_half, o0, o1), lse))
    return outs


def _for_each_block(nq, first_fn, rest_fn):
    first_fn()
    rest = nq - 1
    if rest <= ATTN_UNROLL:
        for jb in range(1, nq):
            rest_fn(jb)
        return
    assert rest % ATTN_UNROLL == 0

    def body(it, carry):
        for u in range(ATTN_UNROLL):
            rest_fn(1 + it * ATTN_UNROLL + u)
        return carry

    lax.fori_loop(0, rest // ATTN_UNROLL, body, 0)


def _dilated_kernel(q1, kp1, kc1, vp1, vc1, q4, kp4, kc4, vp4, vc4, q16, kp16, kc16, vp16, vc16,
                    o_ref, o4_s, l4_s, o16_s, l16_s, *, n_backs):
    tile = o_ref.shape[1]
    not_first = pl.program_id(2) > 0

    def masks(n_back):
        band, in_cur = _band_mask(n_back)
        return band & (in_cur | not_first), band

    def halo(p_ref, c_ref, idx):
        return jnp.concatenate([p_ref[idx], c_ref[idx + (slice(0, ATTN_BLOCK),)]], axis=0)

    def window(c_ref, idx, jb):
        return c_ref[idx + (_rows(jb - 1, ATTN_BLOCK, 2),)]

    def rows(jb):
        return _rows(jb, ATTN_BLOCK)

    for (q, kp, kc, vp, vc, o_s, l_s), n_back in zip(
            ((q16, kp16, kc16, vp16, vc16, o16_s, l16_s), (q4, kp4, kc4, vp4, vc4, o4_s, l4_s)),
            (n_backs[2], n_backs[1])):
        dil = q.shape[1]
        nq = q.shape[2] // ATTN_BLOCK
        valid0, valid = masks(n_back)
        for r in range(dil):
            idx = (0, r)

            def put(jb, res, r=r, dil=dil, o_s=o_s, l_s=l_s):
                (o, lse), = res
                dst = pl.ds(jb * ATTN_BLOCK * dil + r, ATTN_BLOCK, stride=dil)
                o_s[dst, :] = o
                l_s[dst, :] = lse

            def first(q=q, kp=kp, kc=kc, vp=vp, vc=vc, idx=idx, put=put, valid0=valid0):
                put(0, _band_block([q[idx + (slice(0, ATTN_BLOCK),)]], halo(kp, kc, idx),
                                   halo(vp, vc, idx), valid0))

            def rest(jb, q=q, kc=kc, vc=vc, idx=idx, put=put, valid=valid):
                put(jb, _band_block([q[idx + (rows(jb),)]], window(kc, idx, jb),
                                    window(vc, idx, jb), valid))

            _for_each_block(nq, first, rest)

    valid0, valid = masks(n_backs[0])
    idx = (0,)

    def merge(jb, res):
        (o, lse), = res
        dst = rows(jb)
        l4 = l4_s[dst, :]
        l16 = l16_s[dst, :]
        mx = jnp.maximum(jnp.maximum(lse, l4), l16)
        w1 = jnp.exp(lse - mx)
        w4 = jnp.exp(l4 - mx)
        w16 = jnp.exp(l16 - mx)
        num = w1 * o + w4 * o4_s[dst, :] + w16 * o16_s[dst, :]
        o_ref[0, dst, :] = (num / (w1 + w4 + w16)).astype(o_ref.dtype)

    _for_each_block(
        tile // ATTN_BLOCK,
        lambda: merge(0, _band_block([q1[0, 0:ATTN_BLOCK]], halo(kp1, kc1, idx), halo(vp1, vc1, idx), valid0)),
        lambda jb: merge(jb, _band_block([q1[0, rows(jb)]], window(kc1, idx, jb), window(vc1, idx, jb), valid)))


def _dilated_attention(qkv1, qkv4, qkv16):
    bsz, seq, _ = qkv1.shape
    dils = tuple(p[1] for p in A_PATTERNS)
    assert dils[0] == 1 and qkv4.shape[1] == dils[1] and qkv16.shape[1] == dils[2]
    tile = dils[2] * ATTN_BLOCK
    nt = seq // tile

    def specs(dil):
        rows = tile // dil
        nb = rows // ATTN_BLOCK
        if dil == 1:
            cur = lambda off: pl.BlockSpec((1, rows, LANES), lambda b, hp, i: (b, i, off + hp))
            prev = lambda off: pl.BlockSpec(
                (1, ATTN_BLOCK, LANES), lambda b, hp, i: (b, jnp.maximum(i * nb - 1, 0), off + hp))
        else:
            cur = lambda off: pl.BlockSpec((1, dil, rows, LANES), lambda b, hp, i: (b, 0, i, off + hp))
            prev = lambda off: pl.BlockSpec(
                (1, dil, ATTN_BLOCK, LANES), lambda b, hp, i: (b, 0, jnp.maximum(i * nb - 1, 0), off + hp))
        return [cur(0), prev(A_BLOCKS), cur(A_BLOCKS), prev(2 * A_BLOCKS), cur(2 * A_BLOCKS)]

    kern = functools.partial(_dilated_kernel, n_backs=tuple(w // d for w, d in A_PATTERNS))
    return pl.pallas_call(
        kern,
        grid=(bsz, A_BLOCKS, nt),
        in_specs=specs(1) + specs(dils[1]) + specs(dils[2]),
        out_specs=pl.BlockSpec((1, tile, LANES), lambda b, hp, i: (b, i, hp)),
        out_shape=jax.ShapeDtypeStruct((bsz, seq, A_BLOCKS * LANES), BF16),
        scratch_shapes=[pltpu.VMEM((tile, LANES), F32)] * 4,
        compiler_params=_params("parallel", "parallel", "arbitrary"),
        name="dilated_attn",
    )(*([qkv1] * 5 + [qkv4] * 5 + [qkv16] * 5))


def _swa_kernel(sink_ref, q_ref, *rest, n_back, nq):
    kv = rest[:4 * B_KV_HEADS]
    o_ref = rest[4 * B_KV_HEADS]
    not_first = pl.program_id(1) > 0
    band, in_cur = _band_mask(n_back)
    valid0 = band & (in_cur | not_first)
    pairs = BQ_BLOCKS // B_KV_HEADS

    def run(q_rows, kv_of, valid):
        for g in range(B_KV_HEADS):
            cbs = [g * pairs + j for j in range(pairs)]
            qs = [q_ref[0, q_rows, cb * LANES:(cb + 1) * LANES] for cb in cbs]
            sinks = [sink_ref[2 * cb + p] for cb in cbs for p in range(2)]
            kk, vv = kv_of(g)
            res = _band_block(qs, kk, vv, valid, sinks, want_lse=False)
            for cb, (o, _) in zip(cbs, res):
                o_ref[0, q_rows, cb * LANES:(cb + 1) * LANES] = o.astype(o_ref.dtype)

    def halo_kv(g):
        kp, kc, vp, vc = kv[4 * g:4 * g + 4]
        return (jnp.concatenate([kp[0], kc[0, 0:ATTN_BLOCK]], axis=0),
                jnp.concatenate([vp[0], vc[0, 0:ATTN_BLOCK]], axis=0))

    def window_kv(jb):
        win = _rows(jb - 1, ATTN_BLOCK, 2)
        return lambda g: (kv[4 * g + 1][0, win], kv[4 * g + 3][0, win])

    _for_each_block(
        nq,
        lambda: run(_rows(0, ATTN_BLOCK), halo_kv, valid0),
        lambda jb: run(_rows(jb, ATTN_BLOCK), window_kv(jb), band))


def _swa_gqa(qkvb, sinks, tq=512):
    bsz, seq, _ = qkvb.shape
    tq = min(seq, tq)
    nq = tq // ATTN_BLOCK
    bq_w = BQ_BLOCKS * LANES
    in_specs = [pl.BlockSpec(memory_space=pltpu.SMEM),
                pl.BlockSpec((1, tq, bq_w), lambda b, i: (b, i, 0))]
    for g in range(B_KV_HEADS):
        for section in range(2):
            cb = BQ_BLOCKS + section * B_KV_HEADS + g
            in_specs += [pl.BlockSpec((1, ATTN_BLOCK, LANES),
                                      lambda b, i, cb=cb: (b, jnp.maximum(i * nq - 1, 0), cb)),
                         pl.BlockSpec((1, tq, LANES), lambda b, i, cb=cb: (b, i, cb))]
    args = [sinks.astype(F32)] + [qkvb] * (1 + 4 * B_KV_HEADS)
    kern = functools.partial(_swa_kernel, n_back=B_WINDOW - 1, nq=nq)
    return pl.pallas_call(
        kern,
        grid=(bsz, seq // tq),
        in_specs=in_specs,
        out_specs=pl.BlockSpec((1, tq, bq_w), lambda b, i: (b, i, 0)),
        out_shape=jax.ShapeDtypeStruct((bsz, seq, bq_w), BF16),
        compiler_params=_params("parallel", "arbitrary"),
        name="swa_gqa",
    )(*args)


def _outproj_ffn_kernel(x_ref, a_ref, b_ref, wa_ref, wb_ref, g_ref, wg_ref, wu_ref, wd_ref,
                        o_ref, h_ref):
    f = pl.program_id(1)

    @pl.when(f == 0)
    def _():
        x1 = x_ref[...] + _dot(a_ref[...], wa_ref[...]) + _dot(b_ref[...], wb_ref[...])
        o_ref[...] = x1
        h_ref[...] = _rms(x1, g_ref[...]).astype(BF16)

    h = h_ref[...]
    act = _silu(_dot(h, wg_ref[...])) * _dot(h, wu_ref[...])
    o_ref[...] += _dot(act.astype(BF16), wd_ref[...])


def _outproj_ffn(x, mix_a, mix_b, w_a, w_b, g, w_gate, w_up, w_down, tm=1024, tf=512):
    n, d = x.shape
    ff = w_gate.shape[1]
    ka, kb = mix_a.shape[1], mix_b.shape[1]
    return pl.pallas_call(
        _outproj_ffn_kernel,
        grid=(n // tm, ff // tf),
        in_specs=[pl.BlockSpec((tm, d), lambda i, f: (i, 0)),
                  pl.BlockSpec((tm, ka), lambda i, f: (i, 0)),
                  pl.BlockSpec((tm, kb), lambda i, f: (i, 0)),
                  pl.BlockSpec((ka, d), lambda i, f: (0, 0)),
                  pl.BlockSpec((kb, d), lambda i, f: (0, 0)),
                  pl.BlockSpec((1, d), lambda i, f: (0, 0)),
                  pl.BlockSpec((d, tf), lambda i, f: (0, f)),
                  pl.BlockSpec((d, tf), lambda i, f: (0, f)),
                  pl.BlockSpec((tf, d), lambda i, f: (f, 0))],
        out_specs=pl.BlockSpec((tm, d), lambda i, f: (i, 0)),
        out_shape=jax.ShapeDtypeStruct((n, d), F32),
        scratch_shapes=[pltpu.VMEM((tm, d), BF16)],
        compiler_params=_params("parallel", "arbitrary"),
        name="outproj_ffn",
    )(x, mix_a, mix_b, w_a, w_b, g.reshape(1, d), w_gate, w_up, w_down)


def _rec_inproj_kernel(x_ref, g_ref, w_ref, lb_ref, q_ref, k_ref, lf_ref, v_ref, gate_ref):
    h = _rms(x_ref[...], g_ref[...]).astype(BF16)
    acc = _dot(h, w_ref[...])
    kw = q_ref.shape[1]
    vw = v_ref.shape[1]
    lb = lb_ref[...]
    q_ref[...] = _silu(acc[:, :kw]).astype(BF16)
    fg = lb + (1.0 - lb) / (1.0 + jnp.exp(-acc[:, kw:2 * kw]))
    k_ref[...] = (1.0 - fg).astype(BF16)
    lf_ref[...] = jnp.log(fg)
    v_ref[...] = acc[:, 2 * kw:2 * kw + vw].astype(BF16)
    gate_ref[...] = _silu(acc[:, 2 * kw + vw:]).astype(BF16)


def _rec_inproj(x, g, w, lb, kw, vw, tm=512):
    n, d = x.shape
    width = w.shape[1]
    row = lambda i: (i, 0)
    fix = lambda i: (0, 0)
    return pl.pallas_call(
        _rec_inproj_kernel,
        grid=(n // tm,),
        in_specs=[pl.BlockSpec((tm, d), row),
                  pl.BlockSpec((1, d), fix),
                  pl.BlockSpec((d, width), fix),
                  pl.BlockSpec((1, kw), fix)],
        out_specs=[pl.BlockSpec((tm, kw), row), pl.BlockSpec((tm, kw), row),
                   pl.BlockSpec((tm, kw), row), pl.BlockSpec((tm, vw), row),
                   pl.BlockSpec((tm, vw), row)],
        out_shape=[jax.ShapeDtypeStruct((n, kw), BF16), jax.ShapeDtypeStruct((n, kw), BF16),
                   jax.ShapeDtypeStruct((n, kw), F32), jax.ShapeDtypeStruct((n, vw), BF16),
                   jax.ShapeDtypeStruct((n, vw), BF16)],
        compiler_params=_params("parallel"),
        name="rec_inproj",
    )(x, g.reshape(1, d), w, lb.reshape(1, kw))


def _hgrn_kernel(q_ref, k_ref, lf_ref, v_ref, gate_ref, ng_ref, o_ref, st_ref, b_ref, kf_ref, *,
                 n_chunks):
    @pl.when(pl.program_id(2) == 0)
    def _():
        st_ref[...] = jnp.zeros_like(st_ref)

    c_len, sub = REC_CHUNK, REC_SUB
    heads = [slice(hh * LANES, (hh + 1) * LANES) for hh in range(q_ref.shape[2] // LANES)]
    r_i = lax.broadcasted_iota(I32, (c_len, c_len), 0)
    c_i = lax.broadcasted_iota(I32, (c_len, c_len), 1)
    causal = c_i <= r_i
    tril = causal.astype(BF16)

    drop = jnp.zeros((1, LANES), F32)
    for c in range(n_chunks):
        rows = slice(c * c_len, (c + 1) * c_len)
        for hh, cols in enumerate(heads):
            lf = lf_ref[0, rows, cols]
            lf1 = lf.astype(BF16)
            rem = lf - lf1.astype(F32)
            lf2 = rem.astype(BF16)
            lf3 = (rem - lf2.astype(F32)).astype(BF16)
            b = _dot(tril, lf1) + _dot(tril, lf2) + _dot(tril, lf3)
            b_ref[hh, rows, :] = b
            for i in range(c_len // sub):
                end = b[(i + 1) * sub - 1:(i + 1) * sub, :]
                drop = jnp.minimum(drop, end - b[i * sub - 1:i * sub, :] if i > 0 else end)
    safe = jnp.min(drop) > REC_SAFE_DROP

    def finish(c, hh, att, st):
        rows = _rows(c, c_len)
        cols = heads[hh]
        b = b_ref[hh, rows, :]
        q = q_ref[0, rows, cols].astype(F32)
        k = k_ref[0, rows, cols].astype(F32)
        v = v_ref[0, rows, cols]
        b_last = b[c_len - 1:c_len, :]
        qe = (q * jnp.exp(b)).astype(BF16)
        o = _dot(att.astype(BF16), v) + _dot_nt(qe, st.astype(BF16))
        kd = (k * jnp.exp(b_last - b)).astype(BF16)
        st = st * jnp.exp(b_last) + _dot_tn(v, kd)
        y = o * lax.rsqrt(jnp.mean(o * o, axis=-1, keepdims=True) + EPS)
        o_ref[0, rows, cols] = (y * ng_ref[:, cols] * gate_ref[0, rows, cols].astype(F32)).astype(o_ref.dtype)
        return st

    @pl.when(safe)
    def _():
        sts = [st_ref[hh] for hh in range(len(heads))]
        for c in range(n_chunks):
            rows = slice(c * c_len, (c + 1) * c_len)
            for hh, cols in enumerate(heads):
                b = b_ref[hh, rows, :]
                q = q_ref[0, rows, cols].astype(F32)
                k = k_ref[0, rows, cols].astype(F32)
                att_rows = []
                for i in range(c_len // sub):
                    lo, hi = i * sub, (i + 1) * sub
                    ref = b[lo - 1:lo, :] if i > 0 else jnp.zeros((1, LANES), F32)
                    qt = (q[lo:hi] * jnp.exp(b[lo:hi] - ref)).astype(BF16)
                    kt = (k[:hi] * jnp.exp(ref - b[:hi])).astype(BF16)
                    if hi < c_len:
                        kt = jnp.concatenate([kt, jnp.zeros((c_len - hi, LANES), BF16)], axis=0)
                    att_rows.append(_dot_nt(qt, kt))
                att = jnp.where(causal, jnp.concatenate(att_rows, axis=0), 0.0)
                sts[hh] = finish(c, hh, att, sts[hh])
        for hh in range(len(heads)):
            st_ref[hh] = sts[hh]

    @pl.when(jnp.logical_not(safe))
    def _():
        for hh, cols in enumerate(heads):
            def chunk(c, st, hh=hh, cols=cols):
                rows = _rows(c, c_len)
                b = b_ref[hh, rows, :]
                q = q_ref[0, rows, cols].astype(F32)
                kf_ref[...] = k_ref[0, rows, cols].astype(F32)

                def key(s, att):
                    b_s = b_ref[hh, pl.ds(c * c_len + s, 1), :]
                    dec = jnp.exp(jnp.minimum(b - b_s, 0.0))
                    col = jnp.sum(q * kf_ref[pl.ds(s, 1), :] * dec, axis=-1, keepdims=True)
                    return jnp.where(c_i == s, col, att)

                att = lax.fori_loop(0, c_len, key, jnp.zeros((c_len, c_len), F32))
                return finish(c, hh, jnp.where(causal, att, 0.0), st)

            st_ref[hh] = lax.fori_loop(0, n_chunks, chunk, st_ref[hh])


def _hgrn(q, k, lf, v, gate, norm_g, bsz, seq, ts=512):
    kw = q.shape[-1] // C_HEADS
    vw = v.shape[-1] // C_HEADS
    assert kw == LANES and vw == LANES
    ts = min(ts, seq)
    v3 = lambda t: t.reshape(bsz, seq, t.shape[-1])
    wide = REC_HEADS * LANES
    blk = pl.BlockSpec((1, ts, wide), lambda b, h, c: (b, c, h))
    kern = functools.partial(_hgrn_kernel, n_chunks=ts // REC_CHUNK)
    out = pl.pallas_call(
        kern,
        grid=(bsz, C_HEADS // REC_HEADS, seq // ts),
        in_specs=[blk, blk, blk, blk, blk, pl.BlockSpec((1, wide), lambda b, h, c: (0, h))],
        out_specs=blk,
        out_shape=jax.ShapeDtypeStruct((bsz, seq, C_HEADS * vw), BF16),
        scratch_shapes=[pltpu.VMEM((REC_HEADS, vw, kw), F32), pltpu.VMEM((REC_HEADS, ts, LANES), F32),
                        pltpu.VMEM((REC_CHUNK, LANES), F32)],
        compiler_params=_params("parallel", "parallel", "arbitrary"),
        name="hgrn2",
    )(v3(q), v3(k), v3(lf), v3(v), v3(gate), norm_g.reshape(1, -1))
    return out.reshape(bsz * seq, C_HEADS * vw)


ROUTE_E, ROUTE_RANK, ROUTE_GATE = 0, 2, 4


def _router_kernel(x_ref, a_ref, wo_ref, g_ref, wr_hi_ref, wr_lo_ref, x3_ref, h_ref, route_ref,
                   cnt_ref, base_ref):
    @pl.when(pl.program_id(0) == 0)
    def _():
        base_ref[...] = jnp.zeros_like(base_ref)

    sub = min(ROUTER_SUB, x_ref.shape[0])
    lane = lax.broadcasted_iota(I32, (sub, LANES), 1)
    lane_f = lane.astype(F32)
    r_i = lax.broadcasted_iota(I32, (sub, sub), 0)
    c_i = lax.broadcasted_iota(I32, (sub, sub), 1)
    earlier = (c_i < r_i).astype(BF16)
    base = base_ref[...]
    for r0 in range(0, x_ref.shape[0], sub):
        rows = slice(r0, r0 + sub)
        x3 = x_ref[rows, :] + _dot(a_ref[rows, :], wo_ref[...])
        x3_ref[rows, :] = x3
        h = _rms(x3, g_ref[...])
        _to_row_tiles(h_ref, h, r0)
        h_hi = h.astype(BF16)
        h_lo = (h - h_hi.astype(F32)).astype(BF16)
        logits = _dot(h_hi, wr_hi_ref[...]) + _dot(h_hi, wr_lo_ref[...]) + _dot(h_lo, wr_hi_ref[...])
        logits = jnp.where(lane < N_EXPERTS, logits, -jnp.inf)
        v1 = jnp.max(logits, axis=-1, keepdims=True)
        e1 = jnp.min(jnp.where(logits == v1, lane_f, float(LANES)), axis=-1, keepdims=True)
        hot1 = lane_f == e1
        rest = jnp.where(hot1, -jnp.inf, logits)
        v2 = jnp.max(rest, axis=-1, keepdims=True)
        e2 = jnp.min(jnp.where(rest == v2, lane_f, float(LANES)), axis=-1, keepdims=True)
        hot2 = lane_f == e2
        t = jnp.exp(v2 - v1)
        g1 = 1.0 / (1.0 + t)
        g2 = t / (1.0 + t)
        member = (hot1 | hot2).astype(BF16)
        before = _dot(earlier, member) + base
        rank1 = jnp.sum(jnp.where(hot1, before, 0.0), axis=-1, keepdims=True).astype(I32)
        rank2 = jnp.sum(jnp.where(hot2, before, 0.0), axis=-1, keepdims=True).astype(I32)
        base = base + jnp.sum(member.astype(F32), axis=0, keepdims=True)
        route = jnp.where(lane == ROUTE_E, e1.astype(I32), 0)
        route = jnp.where(lane == ROUTE_E + 1, e2.astype(I32), route)
        route = jnp.where(lane == ROUTE_RANK, rank1, route)
        route = jnp.where(lane == ROUTE_RANK + 1, rank2, route)
        gate_bits = lax.bitcast_convert_type(jnp.where(lane == ROUTE_GATE, g1, g2), I32)
        route = jnp.where((lane == ROUTE_GATE) | (lane == ROUTE_GATE + 1), gate_bits, route)
        route_ref[rows, :] = route
    base_ref[...] = base
    cnt_ref[...] = base.astype(I32)


def _router(x, a, wo, g, w_router, tm=512):
    n, d = x.shape
    ka = a.shape[1]
    wr = jnp.zeros((d, LANES), F32).at[:, :N_EXPERTS].set(w_router)
    wr_hi = wr.astype(BF16)
    wr_lo = (wr - wr_hi.astype(F32)).astype(BF16)
    row = lambda i: (i, 0)
    fix = lambda i: (0, 0)
    return pl.pallas_call(
        _router_kernel,
        grid=(n // tm,),
        in_specs=[pl.BlockSpec((tm, d), row), pl.BlockSpec((tm, ka), row),
                  pl.BlockSpec((ka, d), fix), pl.BlockSpec((1, d), fix),
                  pl.BlockSpec((d, LANES), fix), pl.BlockSpec((d, LANES), fix)],
        out_specs=[pl.BlockSpec((tm, d), row), pl.BlockSpec((tm * d // LANES, LANES), row),
                   pl.BlockSpec((tm, LANES), row), pl.BlockSpec((1, LANES), fix)],
        out_shape=[jax.ShapeDtypeStruct((n, d), F32), jax.ShapeDtypeStruct((n * d // LANES, LANES), F32),
                   jax.ShapeDtypeStruct((n, LANES), I32), jax.ShapeDtypeStruct((1, LANES), I32)],
        scratch_shapes=[pltpu.VMEM((1, LANES), F32)],
        compiler_params=_params("arbitrary"),
        name="router",
    )(x, a, wo, g.reshape(1, d), wr_hi, wr_lo)


def _experts_kernel(be_ref, nb_ref, src0_ref, srcn_ref, slot_ref, h_hbm, wg_ref, wu_ref, wd_ref,
                    y_hbm, in_ref, out_ref, acc_ref, hb_ref, gsem, ssem, *, rps, nf, n_tok, second):
    i = pl.program_id(0)
    f = pl.program_id(1)
    nb = nb_ref[0]
    tm, d = hb_ref.shape
    pieces = d // LANES
    assert rps * nf == tm

    def tile(r):
        return pl.ds(pl.multiple_of(r * pieces, pieces), pieces)

    def gather_row(src_ref, r):
        return pltpu.make_async_copy(h_hbm.at[tile(src_ref[r])], in_ref.at[tile(r)], gsem)

    def scatter_row(r):
        return pltpu.make_async_copy(out_ref.at[tile(r)], y_hbm.at[tile(slot_ref[r])], ssem)

    def spare_fill(region, k):
        rows = pl.ds((region * second + n_tok + k * tm) * pieces, tm * pieces)
        return pltpu.make_async_copy(out_ref, y_hbm.at[rows], ssem)

    def wait_gather():
        pltpu.make_async_copy(h_hbm.at[pl.ds(0, tm * pieces)], in_ref, gsem).wait()

    def wait_scatter():
        pltpu.make_async_copy(out_ref, y_hbm.at[pl.ds(0, tm * pieces)], ssem).wait()

    @pl.when((i == 0) & (f == 0))
    def _():
        out_ref[...] = jnp.zeros_like(out_ref)
        fills = [spare_fill(region, k) for region in range(2) for k in range((second - n_tok) // tm)]
        for cp in fills:
            cp.start()
        for cp in fills:
            cp.wait()

        def body(it, carry):
            for u in range(8):
                gather_row(src0_ref, it * 8 + u).start()
            return carry

        lax.fori_loop(0, tm // 8, body, 0)

    def start_rows(with_gather):
        for u in range(rps):
            r = f * rps + u
            if with_gather:
                gather_row(srcn_ref, r).start()
            scatter_row(r).start()

    @pl.when(i < nb)
    def _():
        @pl.when(f == 0)
        def _():
            wait_gather()
            for j in range(pieces):
                hb_ref[:, j * LANES:(j + 1) * LANES] = _row_tile_piece(in_ref, j, tm, pieces).astype(BF16)
            acc_ref[...] = jnp.zeros_like(acc_ref)

        start_rows(True)
        h = hb_ref[...]
        act = _silu(_dot(h, wg_ref[...])) * _dot(h, wu_ref[...])
        acc_ref[...] += _dot(act.astype(BF16), wd_ref[...])

        @pl.when(f == nf - 1)
        def _():
            wait_scatter()
            _to_row_tiles(out_ref, acc_ref[...])

    @pl.when(i == nb)
    def _():
        @pl.when(f == 0)
        def _():
            wait_gather()

        start_rows(False)

        @pl.when(f == nf - 1)
        def _():
            wait_scatter()


def _experts(h, src, slot, blk_e, nb_used, w_gate, w_up, w_down, tm, second, tf=896):
    d = w_gate.shape[1]
    pieces = d // LANES
    n = h.shape[0] // pieces
    ff = w_gate.shape[2]
    nf = ff // tf
    n_blocks = src.shape[0] // tm
    rps = tm // nf

    def clamp(i, f, be, nb):
        live = i < nb[0]
        return be[jnp.minimum(i, nb[0] - 1)], jnp.where(live, f, nf - 1)

    def wg_map(i, f, be, nb):
        e, fe = clamp(i, f, be, nb)
        return (e, 0, fe)

    def wd_map(i, f, be, nb):
        e, fe = clamp(i, f, be, nb)
        return (e, fe, 0)

    smem = lambda index_map: pl.BlockSpec((tm,), index_map, memory_space=pltpu.SMEM)
    grid_spec = pltpu.PrefetchScalarGridSpec(
        num_scalar_prefetch=2,
        grid=(n_blocks + 1, nf),
        in_specs=[smem(lambda i, f, be, nb: (0,)),
                  smem(lambda i, f, be, nb: (jnp.minimum(i + 1, nb[0] - 1),)),
                  smem(lambda i, f, be, nb: (jnp.clip(i - 1, 0, nb[0] - 1),)),
                  pl.BlockSpec(memory_space=pl.ANY),
                  pl.BlockSpec((None, d, tf), wg_map),
                  pl.BlockSpec((None, d, tf), wg_map),
                  pl.BlockSpec((None, tf, d), wd_map)],
        out_specs=pl.BlockSpec(memory_space=pl.ANY),
        scratch_shapes=[pltpu.VMEM((tm * pieces, LANES), F32), pltpu.VMEM((tm * pieces, LANES), F32),
                        pltpu.VMEM((tm, d), F32), pltpu.VMEM((tm, d), BF16),
                        pltpu.SemaphoreType.DMA(()), pltpu.SemaphoreType.DMA(())],
    )
    kern = functools.partial(_experts_kernel, rps=rps, nf=nf, n_tok=n, second=second)
    return pl.pallas_call(
        kern,
        grid_spec=grid_spec,
        out_shape=jax.ShapeDtypeStruct((2 * second * pieces, LANES), F32),
        compiler_params=_params("arbitrary", "arbitrary"),
        name="moe_experts",
    )(blk_e, nb_used, src, src, slot, h, w_gate, w_up, w_down)


def _combine_kernel(x_ref, route_ref, g_ref, ya_ref, yb_ref, o_ref):
    route = route_ref[...]
    lane = lax.broadcasted_iota(I32, route.shape, 1)
    gates = lax.bitcast_convert_type(route, F32)
    g1 = jnp.sum(jnp.where(lane == ROUTE_GATE, gates, 0.0), axis=-1, keepdims=True)
    g2 = jnp.sum(jnp.where(lane == ROUTE_GATE + 1, gates, 0.0), axis=-1, keepdims=True)
    tm, d = x_ref.shape
    pieces = d // LANES
    x4 = []
    for j in range(pieces):
        ya = _row_tile_piece(ya_ref, j, tm, pieces)
        yb = _row_tile_piece(yb_ref, j, tm, pieces)
        x4.append(x_ref[:, j * LANES:(j + 1) * LANES] + (ya * g1 + yb * g2))
    ms = sum(jnp.sum(p * p, axis=-1, keepdims=True) for p in x4) * (1.0 / d)
    inv = lax.rsqrt(ms + EPS)
    for j in range(pieces):
        cols = slice(j * LANES, (j + 1) * LANES)
        o_ref[:, cols] = x4[j] * inv * g_ref[:, cols]


def _combine(x, route, y, g, second, tm=512):
    n, d = x.shape
    off = second // tm
    pieces = d // LANES
    return pl.pallas_call(
        _combine_kernel,
        grid=(n // tm,),
        in_specs=[pl.BlockSpec((tm, d), lambda i: (i, 0)),
                  pl.BlockSpec((tm, LANES), lambda i: (i, 0)),
                  pl.BlockSpec((1, d), lambda i: (0, 0)),
                  pl.BlockSpec((tm * pieces, LANES), lambda i: (i, 0)),
                  pl.BlockSpec((tm * pieces, LANES), lambda i: (i + off, 0))],
        out_specs=pl.BlockSpec((tm, d), lambda i: (i, 0)),
        out_shape=jax.ShapeDtypeStruct((n, d), F32),
        compiler_params=_params("parallel"),
        name="moe_combine",
    )(x, route, g.reshape(1, d), y, y)


def _moe_layout(route, counts, n, tm):
    e = route[:, ROUTE_E:ROUTE_E + 2]
    rank = route[:, ROUTE_RANK:ROUTE_RANK + 2]
    sizes = counts[0, :N_EXPERTS]
    padded = ((sizes + tm - 1) // tm) * tm
    pend = jnp.cumsum(padded)
    pstart = pend - padded
    start_of = jnp.zeros_like(e)
    for j in range(N_EXPERTS):
        start_of = jnp.where(e == j, pstart[j], start_of)
    dest = (start_of + rank).astype(I32)
    n_blocks = (2 * n) // tm + N_EXPERTS
    rows = n_blocks * tm
    blk_e = jnp.minimum(jnp.searchsorted(pend, jnp.arange(n_blocks, dtype=I32) * tm, side='right'),
                        N_EXPERTS - 1).astype(I32)
    nb_used = (pend[-1:] // tm).astype(I32)
    second = n + N_EXPERTS * tm
    pair_slot = jnp.arange(n, dtype=I32)[:, None] + jnp.array([0, second], I32)[None, :]
    row = jnp.arange(rows, dtype=I32)
    row_e = jnp.repeat(blk_e, tm)
    pad_j = jnp.clip(row - (pstart + sizes)[row_e], 0, tm - 1)
    spare = n + row_e * tm + pad_j
    slot = spare.at[dest.reshape(-1)].set(pair_slot.reshape(-1))
    token = slot % second
    src = jnp.where(token < n, token, 0)
    return src, slot, blk_e, nb_used, second


def kernel(x, positions, g_mix, g_ffn, g_final, w_in_attn, w_out_attn, attn_sinks, w_in_rec,
           rec_lower_bounds, rec_norm_g, w_out_rec, w_gate_dense, w_up_dense, w_down_dense,
           w_router, w_gate_moe, w_up_moe, w_down_moe):
    bsz, seq, d = x.shape
    n = bsz * seq
    x0 = x.reshape(n, d)
    bf = lambda w: w.astype(BF16)

    cos_t, sin_t = _rope_tables(positions)
    qkv1, qkv4, qkv16, qkvb = _attn_inproj(x, g_mix[0], bf(w_in_attn[0]), cos_t, sin_t)
    mix_a = _dilated_attention(qkv1, qkv4, qkv16)
    mix_b = _swa_gqa(qkvb, attn_sinks[0])
    a_w = A_HEADS * HEAD_DIM
    w_out = bf(w_out_attn[0])
    x2 = _outproj_ffn(x0, mix_a.reshape(n, -1), mix_b.reshape(n, -1), w_out[:a_w], w_out[a_w:],
                      g_ffn[0], bf(w_gate_dense[0]), bf(w_up_dense[0]), bf(w_down_dense[0]))

    lb = jax.nn.softmax(rec_lower_bounds.astype(F32), axis=0)
    lb1 = (jnp.cumsum(lb, axis=0) - lb[0])[1]
    kw = lb1.shape[0]
    vw = rec_norm_g.shape[1]
    q, k, lf, v, gate = _rec_inproj(x2, g_mix[1], bf(w_in_rec[0]), lb1, kw, vw)
    rec = _hgrn(q, k, lf, v, gate, rec_norm_g[0], bsz, seq)
    x3, h, route, counts = _router(x2, rec, bf(w_out_rec[0]), g_ffn[1], w_router[0])
    tm_moe = min(1024, n)
    src, slot, blk_e, nb_used, second = _moe_layout(route, counts, n, tm_moe)
    ys = _experts(h, src, slot, blk_e, nb_used, bf(w_gate_moe[0]), bf(w_up_moe[0]),
                  bf(w_down_moe[0]), tm_moe, second)
    out = _combine(x3, route, ys, g_final, second)
    return out.reshape(bsz, seq, d)
```

```python
import functools

import jax
import jax.numpy as jnp
from jax import lax
from jax.experimental import pallas as pl
from jax.experimental.pallas import tpu as pltpu

F32 = jnp.float32
BF16 = jnp.bfloat16
I32 = jnp.int32

EPS = 1e-6
HEAD_DIM = 64
ROT_DIM = HEAD_DIM // 4
ROPE_THETA = 500000.0
LANES = 128
ATTN_BLOCK = 128
ATTN_UNROLL = 15
A_HEADS = 8
A_PATTERNS = ((128, 1), (512, 4), (2048, 16))
B_Q_HEADS = 8
B_KV_HEADS = 2
B_WINDOW = 128
C_HEADS = 8
N_EXPERTS = 8
REC_CHUNK = 128
REC_SUB = 16
REC_HEADS = 4
ROUTER_SUB = 256
REC_SAFE_DROP = -80.0
VMEM_LIMIT = 56 * 1024 * 1024


def _params(*sem):
    return pltpu.CompilerParams(dimension_semantics=sem, vmem_limit_bytes=VMEM_LIMIT)


def _rms(x, g):
    return x * lax.rsqrt(jnp.mean(x * x, axis=-1, keepdims=True) + EPS) * g


def _silu(x):
    return x / (1.0 + jnp.exp(-x))


def _dot(a, b):
    return jnp.dot(a, b, preferred_element_type=F32)


def _dot_nt(a, b):
    return lax.dot_general(a, b, (((1,), (1,)), ((), ())), preferred_element_type=F32)


def _dot_tn(a, b):
    return lax.dot_general(a, b, (((0,), (0,)), ((), ())), preferred_element_type=F32)


def _rows(j, size, count=1):
    if isinstance(j, int):
        return slice(j * size, (j + count) * size)
    return pl.ds(pl.multiple_of(j * size, size), count * size)


def _to_row_tiles(ref, x, row0=0):
    pieces = x.shape[1] // LANES
    for j in range(pieces):
        ref[pl.ds(row0 * pieces + j, x.shape[0], stride=pieces), :] = x[:, j * LANES:(j + 1) * LANES]


def _row_tile_piece(ref, j, m, pieces):
    return ref[pl.ds(j, m, stride=pieces), :]


def _rope_table_kernel(pos_ref, invf_ref, cos_ref, sin_ref):
    ang = pos_ref[...].astype(F32) * invf_ref[...]
    cos_ref[...] = jnp.cos(ang)
    sin_ref[...] = jnp.sin(ang)


def _rope_tables(positions):
    n = positions.size
    half = ROT_DIM // 2
    per_row = LANES // half
    inv_freq = jnp.power(ROPE_THETA, -jnp.arange(0, ROT_DIM, 2, dtype=F32) / ROT_DIM)
    pos_rep = jnp.repeat(positions.reshape(n // per_row, per_row), half, axis=1)
    invf_row = jnp.tile(inv_freq, per_row).reshape(1, LANES)
    rows = n // per_row
    tr = min(rows, 512)
    cos8, sin8 = pl.pallas_call(
        _rope_table_kernel,
        grid=(rows // tr,),
        in_specs=[pl.BlockSpec((tr, LANES), lambda i: (i, 0)),
                  pl.BlockSpec((1, LANES), lambda i: (0, 0))],
        out_specs=[pl.BlockSpec((tr, LANES), lambda i: (i, 0))] * 2,
        out_shape=[jax.ShapeDtypeStruct((rows, LANES), F32)] * 2,
        compiler_params=_params("parallel"),
        name="rope_tables",
    )(pos_rep, invf_row)
    cos8 = cos8.reshape(n, half)
    sin8 = sin8.reshape(n, half)
    rest = HEAD_DIM - ROT_DIM
    c64 = jnp.concatenate([cos8, cos8, jnp.ones((n, rest), F32)], axis=1)
    s64 = jnp.concatenate([-sin8, sin8, jnp.zeros((n, rest), F32)], axis=1)
    return jnp.tile(c64, (1, 2)), jnp.tile(s64, (1, 2))


A_BLOCKS = A_HEADS * HEAD_DIM // LANES
BQ_BLOCKS = B_Q_HEADS * HEAD_DIM // LANES
A_QKV = 3 * A_BLOCKS * LANES
B_QKV = (BQ_BLOCKS + 2 * B_KV_HEADS) * LANES


def _attn_inproj_kernel(x_ref, g_ref, w_ref, c_ref, s_ref, o1_ref, o4_ref, o16_ref, ob_ref, sc_ref):
    h = _rms(x_ref[0], g_ref[...]).astype(BF16)
    acc = _dot(h, w_ref[...])
    tm = acc.shape[0]
    c = c_ref[0]
    s = s_ref[0]
    lane = lax.broadcasted_iota(I32, c.shape, 1)
    first = (lane % HEAD_DIM) < (ROT_DIM // 2)
    lo_half = lane < HEAD_DIM

    def rope(blk):
        up = pltpu.roll(blk, LANES - ROT_DIM // 2, 1)
        dn = pltpu.roll(blk, ROT_DIM // 2, 1)
        return blk * c + jnp.where(first, up, dn) * s

    def col(cb):
        return acc[:, cb * LANES:(cb + 1) * LANES]

    scale = HEAD_DIM ** -0.5
    for cb in range(3 * A_BLOCKS):
        blk = col(cb)
        if cb < A_BLOCKS:
            blk = rope(blk) * scale
        elif cb < 2 * A_BLOCKS:
            blk = rope(blk)
        sc_ref[cb] = blk
        o1_ref[0, :, cb * LANES:(cb + 1) * LANES] = blk.astype(BF16)
    for o_ref in (o4_ref, o16_ref):
        dil = o_ref.shape[1]
        for r in range(dil):
            for cb in range(3 * A_BLOCKS):
                o_ref[0, r, :, cb * LANES:(cb + 1) * LANES] = (
                    sc_ref[cb, pl.ds(r, tm // dil, stride=dil), :].astype(BF16))
    base = 3 * A_BLOCKS
    for j in range(BQ_BLOCKS):
        ob_ref[0, :, j * LANES:(j + 1) * LANES] = (rope(col(base + j)) * scale).astype(BF16)
    for j, blk in enumerate((rope(col(base + BQ_BLOCKS)), col(base + BQ_BLOCKS + 1))):
        swapped = pltpu.roll(blk, HEAD_DIM, 1)
        for g, dup in enumerate((jnp.where(lo_half, blk, swapped), jnp.where(lo_half, swapped, blk))):
            cb = BQ_BLOCKS + j * B_KV_HEADS + g
            ob_ref[0, :, cb * LANES:(cb + 1) * LANES] = dup.astype(BF16)


def _attn_inproj(x, g, w, cos_t, sin_t, tm=512):
    bsz, seq, d = x.shape
    width = w.shape[1]
    assert B_KV_HEADS * HEAD_DIM == LANES and width == A_QKV + (BQ_BLOCKS + 2) * LANES
    tile = lambda b, i: (b, i, 0)
    fix = lambda b, i: (0, 0)
    d4, d16 = A_PATTERNS[1][1], A_PATTERNS[2][1]
    perm = lambda dil: pl.BlockSpec((1, dil, tm // dil, A_QKV), lambda b, i: (b, 0, i, 0))
    return pl.pallas_call(
        _attn_inproj_kernel,
        grid=(bsz, seq // tm),
        in_specs=[pl.BlockSpec((1, tm, d), tile),
                  pl.BlockSpec((1, d), fix),
                  pl.BlockSpec((d, width), fix),
                  pl.BlockSpec((1, tm, LANES), tile),
                  pl.BlockSpec((1, tm, LANES), tile)],
        out_specs=[pl.BlockSpec((1, tm, A_QKV), tile), perm(d4), perm(d16),
                   pl.BlockSpec((1, tm, B_QKV), tile)],
        out_shape=[jax.ShapeDtypeStruct((bsz, seq, A_QKV), BF16),
                   jax.ShapeDtypeStruct((bsz, d4, seq // d4, A_QKV), BF16),
                   jax.ShapeDtypeStruct((bsz, d16, seq // d16, A_QKV), BF16),
                   jax.ShapeDtypeStruct((bsz, seq, B_QKV), BF16)],
        scratch_shapes=[pltpu.VMEM((3 * A_BLOCKS, tm, LANES), F32)],
        compiler_params=_params("parallel", "parallel"),
        name="attn_inproj",
    )(x, g.reshape(1, d), w, cos_t.reshape(bsz, seq, LANES), sin_t.reshape(bsz, seq, LANES))


def _band_mask(n_back):
    qi = lax.broadcasted_iota(I32, (ATTN_BLOCK, 2 * ATTN_BLOCK), 0)
    kj = lax.broadcasted_iota(I32, (ATTN_BLOCK, 2 * ATTN_BLOCK), 1)
    dist = ATTN_BLOCK + qi - kj
    return (dist >= 0) & (dist <= n_back), kj >= ATTN_BLOCK


def _band_block(q_pairs, kk, vv, valid, sinks=None, want_lse=True):
    lane = lax.broadcasted_iota(I32, (ATTN_BLOCK, LANES), 1)
    lo_half = lane < HEAD_DIM
    zero = jnp.zeros((ATTN_BLOCK, LANES), BF16)
    lhs = []
    for q in q_pairs:
        lhs += [jnp.where(lo_half, q, zero), jnp.where(lo_half, zero, q)]
    s_all = _dot_nt(jnp.concatenate(lhs, axis=0), kk)
    ps, inv_l, lses = [], [], []
    for u in range(len(lhs)):
        s = jnp.where(valid, s_all[u * ATTN_BLOCK:(u + 1) * ATTN_BLOCK], -jnp.inf)
        m = jnp.max(s, axis=-1, keepdims=True)
        if sinks is not None:
            m = jnp.maximum(m, sinks[u])
        e = jnp.exp(s - m)
        l = jnp.sum(e, axis=-1, keepdims=True)
        if sinks is not None:
            l = l + jnp.exp(sinks[u] - m)
        ps.append(e.astype(BF16))
        inv_l.append(1.0 / l)
        lses.append(m + jnp.log(l) if want_lse else None)
    o_all = _dot(jnp.concatenate(ps, axis=0), vv)
    outs = []
    for j in range(len(q_pairs)):
        o0 = o_all[(2 * j) * ATTN_BLOCK:(2 * j + 1) * ATTN_BLOCK] * inv_l[2 * j]
        o1 = o_all[(2 * j + 1) * ATTN_BLOCK:(2 * j + 2) * ATTN_BLOCK] * inv_l[2 * j + 1]
        lse = jnp.where(lo_half, lses[2 * j], lses[2 * j + 1]) if want_lse else None
        outs.append((jnp.where(lo_half, o0, o1), lse))
    return outs


def _for_each_block(nq, first_fn, rest_fn):
    first_fn()
    rest = nq - 1
    if rest <= ATTN_UNROLL:
        for jb in range(1, nq):
            rest_fn(jb)
        return
    assert rest % ATTN_UNROLL == 0

    def body(it, carry):
        for u in range(ATTN_UNROLL):
            rest_fn(1 + it * ATTN_UNROLL + u)
        return carry

    lax.fori_loop(0, rest // ATTN_UNROLL, body, 0)


def _dilated_kernel(q1, kp1, kc1, vp1, vc1, q4, kp4, kc4, vp4, vc4, q16, kp16, kc16, vp16, vc16,
                    o_ref, o4_s, l4_s, o16_s, l16_s, *, n_backs):
    tile = o_ref.shape[1]
    not_first = pl.program_id(2) > 0

    def masks(n_back):
        band, in_cur = _band_mask(n_back)
        return band & (in_cur | not_first), band

    def halo(p_ref, c_ref, idx):
        return jnp.concatenate([p_ref[idx], c_ref[idx + (slice(0, ATTN_BLOCK),)]], axis=0)

    def window(c_ref, idx, jb):
        return c_ref[idx + (_rows(jb - 1, ATTN_BLOCK, 2),)]

    def rows(jb):
        return _rows(jb, ATTN_BLOCK)

    for (q, kp, kc, vp, vc, o_s, l_s), n_back in zip(
            ((q16, kp16, kc16, vp16, vc16, o16_s, l16_s), (q4, kp4, kc4, vp4, vc4, o4_s, l4_s)),
            (n_backs[2], n_backs[1])):
        dil = q.shape[1]
        nq = q.shape[2] // ATTN_BLOCK
        valid0, valid = masks(n_back)
        for r in range(dil):
            idx = (0, r)

            def put(jb, res, r=r, dil=dil, o_s=o_s, l_s=l_s):
                (o, lse), = res
                dst = pl.ds(jb * ATTN_BLOCK * dil + r, ATTN_BLOCK, stride=dil)
                o_s[dst, :] = o
                l_s[dst, :] = lse

            def first(q=q, kp=kp, kc=kc, vp=vp, vc=vc, idx=idx, put=put, valid0=valid0):
                put(0, _band_block([q[idx + (slice(0, ATTN_BLOCK),)]], halo(kp, kc, idx),
                                   halo(vp, vc, idx), valid0))

            def rest(jb, q=q, kc=kc, vc=vc, idx=idx, put=put, valid=valid):
                put(jb, _band_block([q[idx + (rows(jb),)]], window(kc, idx, jb),
                                    window(vc, idx, jb), valid))

            _for_each_block(nq, first, rest)

    valid0, valid = masks(n_backs[0])
    idx = (0,)

    def merge(jb, res):
        (o, lse), = res
        dst = rows(jb)
        l4 = l4_s[dst, :]
        l16 = l16_s[dst, :]
        mx = jnp.maximum(jnp.maximum(lse, l4), l16)
        w1 = jnp.exp(lse - mx)
        w4 = jnp.exp(l4 - mx)
        w16 = jnp.exp(l16 - mx)
        num = w1 * o + w4 * o4_s[dst, :] + w16 * o16_s[dst, :]
        o_ref[0, dst, :] = (num / (w1 + w4 + w16)).astype(o_ref.dtype)

    _for_each_block(
        tile // ATTN_BLOCK,
        lambda: merge(0, _band_block([q1[0, 0:ATTN_BLOCK]], halo(kp1, kc1, idx), halo(vp1, vc1, idx), valid0)),
        lambda jb: merge(jb, _band_block([q1[0, rows(jb)]], window(kc1, idx, jb), window(vc1, idx, jb), valid)))


def _dilated_attention(qkv1, qkv4, qkv16):
    bsz, seq, _ = qkv1.shape
    dils = tuple(p[1] for p in A_PATTERNS)
    assert dils[0] == 1 and qkv4.shape[1] == dils[1] and qkv16.shape[1] == dils[2]
    tile = dils[2] * ATTN_BLOCK
    nt = seq // tile

    def specs(dil):
        rows = tile // dil
        nb = rows // ATTN_BLOCK
        if dil == 1:
            cur = lambda off: pl.BlockSpec((1, rows, LANES), lambda b, hp, i: (b, i, off + hp))
            prev = lambda off: pl.BlockSpec(
                (1, ATTN_BLOCK, LANES), lambda b, hp, i: (b, jnp.maximum(i * nb - 1, 0), off + hp))
        else:
            cur = lambda off: pl.BlockSpec((1, dil, rows, LANES), lambda b, hp, i: (b, 0, i, off + hp))
            prev = lambda off: pl.BlockSpec(
                (1, dil, ATTN_BLOCK, LANES), lambda b, hp, i: (b, 0, jnp.maximum(i * nb - 1, 0), off + hp))
        return [cur(0), prev(A_BLOCKS), cur(A_BLOCKS), prev(2 * A_BLOCKS), cur(2 * A_BLOCKS)]

    kern = functools.partial(_dilated_kernel, n_backs=tuple(w // d for w, d in A_PATTERNS))
    return pl.pallas_call(
        kern,
        grid=(bsz, A_BLOCKS, nt),
        in_specs=specs(1) + specs(dils[1]) + specs(dils[2]),
        out_specs=pl.BlockSpec((1, tile, LANES), lambda b, hp, i: (b, i, hp)),
        out_shape=jax.ShapeDtypeStruct((bsz, seq, A_BLOCKS * LANES), BF16),
        scratch_shapes=[pltpu.VMEM((tile, LANES), F32)] * 4,
        compiler_params=_params("parallel", "parallel", "arbitrary"),
        name="dilated_attn",
    )(*([qkv1] * 5 + [qkv4] * 5 + [qkv16] * 5))


def _swa_kernel(sink_ref, q_ref, *rest, n_back, nq):
    kv = rest[:4 * B_KV_HEADS]
    o_ref = rest[4 * B_KV_HEADS]
    not_first = pl.program_id(1) > 0
    band, in_cur = _band_mask(n_back)
    valid0 = band & (in_cur | not_first)
    pairs = BQ_BLOCKS // B_KV_HEADS

    def run(q_rows, kv_of, valid):
        for g in range(B_KV_HEADS):
            cbs = [g * pairs + j for j in range(pairs)]
            qs = [q_ref[0, q_rows, cb * LANES:(cb + 1) * LANES] for cb in cbs]
            sinks = [sink_ref[2 * cb + p] for cb in cbs for p in range(2)]
            kk, vv = kv_of(g)
            res = _band_block(qs, kk, vv, valid, sinks, want_lse=False)
            for cb, (o, _) in zip(cbs, res):
                o_ref[0, q_rows, cb * LANES:(cb + 1) * LANES] = o.astype(o_ref.dtype)

    def halo_kv(g):
        kp, kc, vp, vc = kv[4 * g:4 * g + 4]
        return (jnp.concatenate([kp[0], kc[0, 0:ATTN_BLOCK]], axis=0),
                jnp.concatenate([vp[0], vc[0, 0:ATTN_BLOCK]], axis=0))

    def window_kv(jb):
        win = _rows(jb - 1, ATTN_BLOCK, 2)
        return lambda g: (kv[4 * g + 1][0, win], kv[4 * g + 3][0, win])

    _for_each_block(
        nq,
        lambda: run(_rows(0, ATTN_BLOCK), halo_kv, valid0),
        lambda jb: run(_rows(jb, ATTN_BLOCK), window_kv(jb), band))


def _swa_gqa(qkvb, sinks, tq=512):
    bsz, seq, _ = qkvb.shape
    tq = min(seq, tq)
    nq = tq // ATTN_BLOCK
    bq_w = BQ_BLOCKS * LANES
    in_specs = [pl.BlockSpec(memory_space=pltpu.SMEM),
                pl.BlockSpec((1, tq, bq_w), lambda b, i: (b, i, 0))]
    for g in range(B_KV_HEADS):
        for section in range(2):
            cb = BQ_BLOCKS + section * B_KV_HEADS + g
            in_specs += [pl.BlockSpec((1, ATTN_BLOCK, LANES),
                                      lambda b, i, cb=cb: (b, jnp.maximum(i * nq - 1, 0), cb)),
                         pl.BlockSpec((1, tq, LANES), lambda b, i, cb=cb: (b, i, cb))]
    args = [sinks.astype(F32)] + [qkvb] * (1 + 4 * B_KV_HEADS)
    kern = functools.partial(_swa_kernel, n_back=B_WINDOW - 1, nq=nq)
    return pl.pallas_call(
        kern,
        grid=(bsz, seq // tq),
        in_specs=in_specs,
        out_specs=pl.BlockSpec((1, tq, bq_w), lambda b, i: (b, i, 0)),
        out_shape=jax.ShapeDtypeStruct((bsz, seq, bq_w), BF16),
        compiler_params=_params("parallel", "arbitrary"),
        name="swa_gqa",
    )(*args)


def _outproj_ffn_kernel(x_ref, a_ref, b_ref, wa_ref, wb_ref, g_ref, wg_ref, wu_ref, wd_ref,
                        o_ref, h_ref):
    f = pl.program_id(1)

    @pl.when(f == 0)
    def _():
        x1 = x_ref[...] + _dot(a_ref[...], wa_ref[...]) + _dot(b_ref[...], wb_ref[...])
        o_ref[...] = x1
        h_ref[...] = _rms(x1, g_ref[...]).astype(BF16)

    h = h_ref[...]
    act = _silu(_dot(h, wg_ref[...])) * _dot(h, wu_ref[...])
    o_ref[...] += _dot(act.astype(BF16), wd_ref[...])


def _outproj_ffn(x, mix_a, mix_b, w_a, w_b, g, w_gate, w_up, w_down, tm=1024, tf=512):
    n, d = x.shape
    ff = w_gate.shape[1]
    ka, kb = mix_a.shape[1], mix_b.shape[1]
    return pl.pallas_call(
        _outproj_ffn_kernel,
        grid=(n // tm, ff // tf),
        in_specs=[pl.BlockSpec((tm, d), lambda i, f: (i, 0)),
                  pl.BlockSpec((tm, ka), lambda i, f: (i, 0)),
                  pl.BlockSpec((tm, kb), lambda i, f: (i, 0)),
                  pl.BlockSpec((ka, d), lambda i, f: (0, 0)),
                  pl.BlockSpec((kb, d), lambda i, f: (0, 0)),
                  pl.BlockSpec((1, d), lambda i, f: (0, 0)),
                  pl.BlockSpec((d, tf), lambda i, f: (0, f)),
                  pl.BlockSpec((d, tf), lambda i, f: (0, f)),
                  pl.BlockSpec((tf, d), lambda i, f: (f, 0))],
        out_specs=pl.BlockSpec((tm, d), lambda i, f: (i, 0)),
        out_shape=jax.ShapeDtypeStruct((n, d), F32),
        scratch_shapes=[pltpu.VMEM((tm, d), BF16)],
        compiler_params=_params("parallel", "arbitrary"),
        name="outproj_ffn",
    )(x, mix_a, mix_b, w_a, w_b, g.reshape(1, d), w_gate, w_up, w_down)


def _rec_inproj_kernel(x_ref, g_ref, w_ref, lb_ref, q_ref, k_ref, lf_ref, v_ref, gate_ref):
    h = _rms(x_ref[...], g_ref[...]).astype(BF16)
    acc = _dot(h, w_ref[...])
    kw = q_ref.shape[1]
    vw = v_ref.shape[1]
    lb = lb_ref[...]
    q_ref[...] = _silu(acc[:, :kw]).astype(BF16)
    fg = lb + (1.0 - lb) / (1.0 + jnp.exp(-acc[:, kw:2 * kw]))
    k_ref[...] = (1.0 - fg).astype(BF16)
    lf_ref[...] = jnp.log(fg)
    v_ref[...] = acc[:, 2 * kw:2 * kw + vw].astype(BF16)
    gate_ref[...] = _silu(acc[:, 2 * kw + vw:]).astype(BF16)


def _rec_inproj(x, g, w, lb, kw, vw, tm=512):
    n, d = x.shape
    width = w.shape[1]
    row = lambda i: (i, 0)
    fix = lambda i: (0, 0)
    return pl.pallas_call(
        _rec_inproj_kernel,
        grid=(n // tm,),
        in_specs=[pl.BlockSpec((tm, d), row),
                  pl.BlockSpec((1, d), fix),
                  pl.BlockSpec((d, width), fix),
                  pl.BlockSpec((1, kw), fix)],
        out_specs=[pl.BlockSpec((tm, kw), row), pl.BlockSpec((tm, kw), row),
                   pl.BlockSpec((tm, kw), row), pl.BlockSpec((tm, vw), row),
                   pl.BlockSpec((tm, vw), row)],
        out_shape=[jax.ShapeDtypeStruct((n, kw), BF16), jax.ShapeDtypeStruct((n, kw), BF16),
                   jax.ShapeDtypeStruct((n, kw), F32), jax.ShapeDtypeStruct((n, vw), BF16),
                   jax.ShapeDtypeStruct((n, vw), BF16)],
        compiler_params=_params("parallel"),
        name="rec_inproj",
    )(x, g.reshape(1, d), w, lb.reshape(1, kw))


def _hgrn_kernel(q_ref, k_ref, lf_ref, v_ref, gate_ref, ng_ref, o_ref, st_ref, b_ref, kf_ref, *,
                 n_chunks):
    @pl.when(pl.program_id(2) == 0)
    def _():
        st_ref[...] = jnp.zeros_like(st_ref)

    c_len, sub = REC_CHUNK, REC_SUB
    heads = [slice(hh * LANES, (hh + 1) * LANES) for hh in range(q_ref.shape[2] // LANES)]
    r_i = lax.broadcasted_iota(I32, (c_len, c_len), 0)
    c_i = lax.broadcasted_iota(I32, (c_len, c_len), 1)
    causal = c_i <= r_i
    tril = causal.astype(BF16)

    drop = jnp.zeros((1, LANES), F32)
    for c in range(n_chunks):
        rows = slice(c * c_len, (c + 1) * c_len)
        for hh, cols in enumerate(heads):
            lf = lf_ref[0, rows, cols]
            lf1 = lf.astype(BF16)
            rem = lf - lf1.astype(F32)
            lf2 = rem.astype(BF16)
            lf3 = (rem - lf2.astype(F32)).astype(BF16)
            b = _dot(tril, lf1) + _dot(tril, lf2) + _dot(tril, lf3)
            b_ref[hh, rows, :] = b
            for i in range(c_len // sub):
                end = b[(i + 1) * sub - 1:(i + 1) * sub, :]
                drop = jnp.minimum(drop, end - b[i * sub - 1:i * sub, :] if i > 0 else end)
    safe = jnp.min(drop) > REC_SAFE_DROP

    def finish(c, hh, att, st):
        rows = _rows(c, c_len)
        cols = heads[hh]
        b = b_ref[hh, rows, :]
        q = q_ref[0, rows, cols].astype(F32)
        k = k_ref[0, rows, cols].astype(F32)
        v = v_ref[0, rows, cols]
        b_last = b[c_len - 1:c_len, :]
        qe = (q * jnp.exp(b)).astype(BF16)
        o = _dot(att.astype(BF16), v) + _dot_nt(qe, st.astype(BF16))
        kd = (k * jnp.exp(b_last - b)).astype(BF16)
        st = st * jnp.exp(b_last) + _dot_tn(v, kd)
        y = o * lax.rsqrt(jnp.mean(o * o, axis=-1, keepdims=True) + EPS)
        o_ref[0, rows, cols] = (y * ng_ref[:, cols] * gate_ref[0, rows, cols].astype(F32)).astype(o_ref.dtype)
        return st

    @pl.when(safe)
    def _():
        sts = [st_ref[hh] for hh in range(len(heads))]
        for c in range(n_chunks):
            rows = slice(c * c_len, (c + 1) * c_len)
            for hh, cols in enumerate(heads):
                b = b_ref[hh, rows, :]
                q = q_ref[0, rows, cols].astype(F32)
                k = k_ref[0, rows, cols].astype(F32)
                att_rows = []
                for i in range(c_len // sub):
                    lo, hi = i * sub, (i + 1) * sub
                    ref = b[lo - 1:lo, :] if i > 0 else jnp.zeros((1, LANES), F32)
                    qt = (q[lo:hi] * jnp.exp(b[lo:hi] - ref)).astype(BF16)
                    kt = (k[:hi] * jnp.exp(ref - b[:hi])).astype(BF16)
                    if hi < c_len:
                        kt = jnp.concatenate([kt, jnp.zeros((c_len - hi, LANES), BF16)], axis=0)
                    att_rows.append(_dot_nt(qt, kt))
                att = jnp.where(causal, jnp.concatenate(att_rows, axis=0), 0.0)
                sts[hh] = finish(c, hh, att, sts[hh])
        for hh in range(len(heads)):
            st_ref[hh] = sts[hh]

    @pl.when(jnp.logical_not(safe))
    def _():
        for hh, cols in enumerate(heads):
            def chunk(c, st, hh=hh, cols=cols):
                rows = _rows(c, c_len)
                b = b_ref[hh, rows, :]
                q = q_ref[0, rows, cols].astype(F32)
                kf_ref[...] = k_ref[0, rows, cols].astype(F32)

                def key(s, att):
                    b_s = b_ref[hh, pl.ds(c * c_len + s, 1), :]
                    dec = jnp.exp(jnp.minimum(b - b_s, 0.0))
                    col = jnp.sum(q * kf_ref[pl.ds(s, 1), :] * dec, axis=-1, keepdims=True)
                    return jnp.where(c_i == s, col, att)

                att = lax.fori_loop(0, c_len, key, jnp.zeros((c_len, c_len), F32))
                return finish(c, hh, jnp.where(causal, att, 0.0), st)

            st_ref[hh] = lax.fori_loop(0, n_chunks, chunk, st_ref[hh])


def _hgrn(q, k, lf, v, gate, norm_g, bsz, seq, ts=512):
    kw = q.shape[-1] // C_HEADS
    vw = v.shape[-1] // C_HEADS
    assert kw == LANES and vw == LANES
    ts = min(ts, seq)
    v3 = lambda t: t.reshape(bsz, seq, t.shape[-1])
    wide = REC_HEADS * LANES
    blk = pl.BlockSpec((1, ts, wide), lambda b, h, c: (b, c, h))
    kern = functools.partial(_hgrn_kernel, n_chunks=ts // REC_CHUNK)
    out = pl.pallas_call(
        kern,
        grid=(bsz, C_HEADS // REC_HEADS, seq // ts),
        in_specs=[blk, blk, blk, blk, blk, pl.BlockSpec((1, wide), lambda b, h, c: (0, h))],
        out_specs=blk,
        out_shape=jax.ShapeDtypeStruct((bsz, seq, C_HEADS * vw), BF16),
        scratch_shapes=[pltpu.VMEM((REC_HEADS, vw, kw), F32), pltpu.VMEM((REC_HEADS, ts, LANES), F32),
                        pltpu.VMEM((REC_CHUNK, LANES), F32)],
        compiler_params=_params("parallel", "parallel", "arbitrary"),
        name="hgrn2",
    )(v3(q), v3(k), v3(lf), v3(v), v3(gate), norm_g.reshape(1, -1))
    return out.reshape(bsz * seq, C_HEADS * vw)


ROUTE_E, ROUTE_RANK, ROUTE_GATE = 0, 2, 4


def _router_kernel(x_ref, a_ref, wo_ref, g_ref, wr_hi_ref, wr_lo_ref, x3_ref, h_ref, route_ref,
                   cnt_ref, base_ref):
    @pl.when(pl.program_id(0) == 0)
    def _():
        base_ref[...] = jnp.zeros_like(base_ref)

    sub = min(ROUTER_SUB, x_ref.shape[0])
    lane = lax.broadcasted_iota(I32, (sub, LANES), 1)
    lane_f = lane.astype(F32)
    r_i = lax.broadcasted_iota(I32, (sub, sub), 0)
    c_i = lax.broadcasted_iota(I32, (sub, sub), 1)
    earlier = (c_i < r_i).astype(BF16)
    base = base_ref[...]
    for r0 in range(0, x_ref.shape[0], sub):
        rows = slice(r0, r0 + sub)
        x3 = x_ref[rows, :] + _dot(a_ref[rows, :], wo_ref[...])
        x3_ref[rows, :] = x3
        h = _rms(x3, g_ref[...])
        _to_row_tiles(h_ref, h, r0)
        h_hi = h.astype(BF16)
        h_lo = (h - h_hi.astype(F32)).astype(BF16)
        logits = _dot(h_hi, wr_hi_ref[...]) + _dot(h_hi, wr_lo_ref[...]) + _dot(h_lo, wr_hi_ref[...])
        logits = jnp.where(lane < N_EXPERTS, logits, -jnp.inf)
        v1 = jnp.max(logits, axis=-1, keepdims=True)
        e1 = jnp.min(jnp.where(logits == v1, lane_f, float(LANES)), axis=-1, keepdims=True)
        hot1 = lane_f == e1
        rest = jnp.where(hot1, -jnp.inf, logits)
        v2 = jnp.max(rest, axis=-1, keepdims=True)
        e2 = jnp.min(jnp.where(rest == v2, lane_f, float(LANES)), axis=-1, keepdims=True)
        hot2 = lane_f == e2
        t = jnp.exp(v2 - v1)
        g1 = 1.0 / (1.0 + t)
        g2 = t / (1.0 + t)
        member = (hot1 | hot2).astype(BF16)
        before = _dot(earlier, member) + base
        rank1 = jnp.sum(jnp.where(hot1, before, 0.0), axis=-1, keepdims=True).astype(I32)
        rank2 = jnp.sum(jnp.where(hot2, before, 0.0), axis=-1, keepdims=True).astype(I32)
        base = base + jnp.sum(member.astype(F32), axis=0, keepdims=True)
        route = jnp.where(lane == ROUTE_E, e1.astype(I32), 0)
        route = jnp.where(lane == ROUTE_E + 1, e2.astype(I32), route)
        route = jnp.where(lane == ROUTE_RANK, rank1, route)
        route = jnp.where(lane == ROUTE_RANK + 1, rank2, route)
        gate_bits = lax.bitcast_convert_type(jnp.where(lane == ROUTE_GATE, g1, g2), I32)
        route = jnp.where((lane == ROUTE_GATE) | (lane == ROUTE_GATE + 1), gate_bits, route)
        route_ref[rows, :] = route
    base_ref[...] = base
    cnt_ref[...] = base.astype(I32)


def _router(x, a, wo, g, w_router, tm=512):
    n, d = x.shape
    ka = a.shape[1]
    wr = jnp.zeros((d, LANES), F32).at[:, :N_EXPERTS].set(w_router)
    wr_hi = wr.astype(BF16)
    wr_lo = (wr - wr_hi.astype(F32)).astype(BF16)
    row = lambda i: (i, 0)
    fix = lambda i: (0, 0)
    return pl.pallas_call(
        _router_kernel,
        grid=(n // tm,),
        in_specs=[pl.BlockSpec((tm, d), row), pl.BlockSpec((tm, ka), row),
                  pl.BlockSpec((ka, d), fix), pl.BlockSpec((1, d), fix),
                  pl.BlockSpec((d, LANES), fix), pl.BlockSpec((d, LANES), fix)],
        out_specs=[pl.BlockSpec((tm, d), row), pl.BlockSpec((tm * d // LANES, LANES), row),
                   pl.BlockSpec((tm, LANES), row), pl.BlockSpec((1, LANES), fix)],
        out_shape=[jax.ShapeDtypeStruct((n, d), F32), jax.ShapeDtypeStruct((n * d // LANES, LANES), F32),
                   jax.ShapeDtypeStruct((n, LANES), I32), jax.ShapeDtypeStruct((1, LANES), I32)],
        scratch_shapes=[pltpu.VMEM((1, LANES), F32)],
        compiler_params=_params("arbitrary"),
        name="router",
    )(x, a, wo, g.reshape(1, d), wr_hi, wr_lo)


def _experts_kernel(be_ref, nb_ref, src0_ref, srcn_ref, slot_ref, h_hbm, wg_ref, wu_ref, wd_ref,
                    y_hbm, in_ref, out_ref, acc_ref, hb_ref, gsem, ssem, *, rps, nf, n_tok, second):
    i = pl.program_id(0)
    f = pl.program_id(1)
    nb = nb_ref[0]
    tm, d = hb_ref.shape
    pieces = d // LANES
    assert rps * nf == tm

    def tile(r):
        return pl.ds(pl.multiple_of(r * pieces, pieces), pieces)

    def gather_row(src_ref, r):
        return pltpu.make_async_copy(h_hbm.at[tile(src_ref[r])], in_ref.at[tile(r)], gsem)

    def scatter_row(r):
        return pltpu.make_async_copy(out_ref.at[tile(r)], y_hbm.at[tile(slot_ref[r])], ssem)

    def spare_fill(region, k):
        rows = pl.ds((region * second + n_tok + k * tm) * pieces, tm * pieces)
        return pltpu.make_async_copy(out_ref, y_hbm.at[rows], ssem)

    def wait_gather():
        pltpu.make_async_copy(h_hbm.at[pl.ds(0, tm * pieces)], in_ref, gsem).wait()

    def wait_scatter():
        pltpu.make_async_copy(out_ref, y_hbm.at[pl.ds(0, tm * pieces)], ssem).wait()

    @pl.when((i == 0) & (f == 0))
    def _():
        out_ref[...] = jnp.zeros_like(out_ref)
        fills = [spare_fill(region, k) for region in range(2) for k in range((second - n_tok) // tm)]
        for cp in fills:
            cp.start()
        for cp in fills:
            cp.wait()

        def body(it, carry):
            for u in range(8):
                gather_row(src0_ref, it * 8 + u).start()
            return carry

        lax.fori_loop(0, tm // 8, body, 0)

    def start_rows(with_gather):
        for u in range(rps):
            r = f * rps + u
            if with_gather:
                gather_row(srcn_ref, r).start()
            scatter_row(r).start(priority=u % 2)

    @pl.when(i < nb)
    def _():
        @pl.when(f == 0)
        def _():
            wait_gather()
            for j in range(pieces):
                hb_ref[:, j * LANES:(j + 1) * LANES] = _row_tile_piece(in_ref, j, tm, pieces).astype(BF16)
            acc_ref[...] = jnp.zeros_like(acc_ref)

        start_rows(True)
        h = hb_ref[...]
        act = _silu(_dot(h, wg_ref[...])) * _dot(h, wu_ref[...])
        acc_ref[...] += _dot(act.astype(BF16), wd_ref[...])

        @pl.when(f == nf - 1)
        def _():
            wait_scatter()
            _to_row_tiles(out_ref, acc_ref[...])

    @pl.when(i == nb)
    def _():
        @pl.when(f == 0)
        def _():
            wait_gather()

        start_rows(False)

        @pl.when(f == nf - 1)
        def _():
            wait_scatter()


def _experts(h, src, slot, blk_e, nb_used, w_gate, w_up, w_down, tm, second, tf=896):
    d = w_gate.shape[1]
    pieces = d // LANES
    n = h.shape[0] // pieces
    ff = w_gate.shape[2]
    nf = ff // tf
    n_blocks = src.shape[0] // tm
    rps = tm // nf

    def clamp(i, f, be, nb):
        live = i < nb[0]
        return be[jnp.minimum(i, nb[0] - 1)], jnp.where(live, f, nf - 1)

    def wg_map(i, f, be, nb):
        e, fe = clamp(i, f, be, nb)
        return (e, 0, fe)

    def wd_map(i, f, be, nb):
        e, fe = clamp(i, f, be, nb)
        return (e, fe, 0)

    smem = lambda index_map: pl.BlockSpec((tm,), index_map, memory_space=pltpu.SMEM)
    grid_spec = pltpu.PrefetchScalarGridSpec(
        num_scalar_prefetch=2,
        grid=(n_blocks + 1, nf),
        in_specs=[smem(lambda i, f, be, nb: (0,)),
                  smem(lambda i, f, be, nb: (jnp.minimum(i + 1, nb[0] - 1),)),
                  smem(lambda i, f, be, nb: (jnp.clip(i - 1, 0, nb[0] - 1),)),
                  pl.BlockSpec(memory_space=pl.ANY),
                  pl.BlockSpec((None, d, tf), wg_map),
                  pl.BlockSpec((None, d, tf), wg_map),
                  pl.BlockSpec((None, tf, d), wd_map)],
        out_specs=pl.BlockSpec(memory_space=pl.ANY),
        scratch_shapes=[pltpu.VMEM((tm * pieces, LANES), F32), pltpu.VMEM((tm * pieces, LANES), F32),
                        pltpu.VMEM((tm, d), F32), pltpu.VMEM((tm, d), BF16),
                        pltpu.SemaphoreType.DMA(()), pltpu.SemaphoreType.DMA(())],
    )
    kern = functools.partial(_experts_kernel, rps=rps, nf=nf, n_tok=n, second=second)
    return pl.pallas_call(
        kern,
        grid_spec=grid_spec,
        out_shape=jax.ShapeDtypeStruct((2 * second * pieces, LANES), F32),
        compiler_params=_params("arbitrary", "arbitrary"),
        name="moe_experts",
    )(blk_e, nb_used, src, src, slot, h, w_gate, w_up, w_down)


def _combine_kernel(x_ref, route_ref, g_ref, ya_ref, yb_ref, o_ref):
    route = route_ref[...]
    lane = lax.broadcasted_iota(I32, route.shape, 1)
    gates = lax.bitcast_convert_type(route, F32)
    g1 = jnp.sum(jnp.where(lane == ROUTE_GATE, gates, 0.0), axis=-1, keepdims=True)
    g2 = jnp.sum(jnp.where(lane == ROUTE_GATE + 1, gates, 0.0), axis=-1, keepdims=True)
    tm, d = x_ref.shape
    pieces = d // LANES
    x4 = []
    for j in range(pieces):
        ya = _row_tile_piece(ya_ref, j, tm, pieces)
        yb = _row_tile_piece(yb_ref, j, tm, pieces)
        x4.append(x_ref[:, j * LANES:(j + 1) * LANES] + (ya * g1 + yb * g2))
    ms = sum(jnp.sum(p * p, axis=-1, keepdims=True) for p in x4) * (1.0 / d)
    inv = lax.rsqrt(ms + EPS)
    for j in range(pieces):
        cols = slice(j * LANES, (j + 1) * LANES)
        o_ref[:, cols] = x4[j] * inv * g_ref[:, cols]


def _combine(x, route, y, g, second, tm=512):
    n, d = x.shape
    off = second // tm
    pieces = d // LANES
    return pl.pallas_call(
        _combine_kernel,
        grid=(n // tm,),
        in_specs=[pl.BlockSpec((tm, d), lambda i: (i, 0)),
                  pl.BlockSpec((tm, LANES), lambda i: (i, 0)),
                  pl.BlockSpec((1, d), lambda i: (0, 0)),
                  pl.BlockSpec((tm * pieces, LANES), lambda i: (i, 0)),
                  pl.BlockSpec((tm * pieces, LANES), lambda i: (i + off, 0))],
        out_specs=pl.BlockSpec((tm, d), lambda i: (i, 0)),
        out_shape=jax.ShapeDtypeStruct((n, d), F32),
        compiler_params=_params("parallel"),
        name="moe_combine",
    )(x, route, g.reshape(1, d), y, y)


def _moe_layout(route, counts, n, tm):
    e = route[:, ROUTE_E:ROUTE_E + 2]
    rank = route[:, ROUTE_RANK:ROUTE_RANK + 2]
    sizes = counts[0, :N_EXPERTS]
    padded = ((sizes + tm - 1) // tm) * tm
    pend = jnp.cumsum(padded)
    pstart = pend - padded
    start_of = jnp.zeros_like(e)
    for j in range(N_EXPERTS):
        start_of = jnp.where(e == j, pstart[j], start_of)
    dest = (start_of + rank).astype(I32)
    n_blocks = (2 * n) // tm + N_EXPERTS
    rows = n_blocks * tm
    blk_e = jnp.minimum(jnp.searchsorted(pend, jnp.arange(n_blocks, dtype=I32) * tm, side='right'),
                        N_EXPERTS - 1).astype(I32)
    nb_used = (pend[-1:] // tm).astype(I32)
    second = n + N_EXPERTS * tm
    pair_slot = jnp.arange(n, dtype=I32)[:, None] + jnp.array([0, second], I32)[None, :]
    row = jnp.arange(rows, dtype=I32)
    row_e = jnp.repeat(blk_e, tm)
    pad_j = jnp.clip(row - (pstart + sizes)[row_e], 0, tm - 1)
    spare = n + row_e * tm + pad_j
    slot = spare.at[dest.reshape(-1)].set(pair_slot.reshape(-1))
    token = slot % second
    src = jnp.where(token < n, token, 0)
    return src, slot, blk_e, nb_used, second


def kernel(x, positions, g_mix, g_ffn, g_final, w_in_attn, w_out_attn, attn_sinks, w_in_rec,
           rec_lower_bounds, rec_norm_g, w_out_rec, w_gate_dense, w_up_dense, w_down_dense,
           w_router, w_gate_moe, w_up_moe, w_down_moe):
    bsz, seq, d = x.shape
    n = bsz * seq
    x0 = x.reshape(n, d)
    bf = lambda w: w.astype(BF16)

    cos_t, sin_t = _rope_tables(positions)
    qkv1, qkv4, qkv16, qkvb = _attn_inproj(x, g_mix[0], bf(w_in_attn[0]), cos_t, sin_t)
    mix_a = _dilated_attention(qkv1, qkv4, qkv16)
    mix_b = _swa_gqa(qkvb, attn_sinks[0])
    a_w = A_HEADS * HEAD_DIM
    w_out = bf(w_out_attn[0])
    x2 = _outproj_ffn(x0, mix_a.reshape(n, -1), mix_b.reshape(n, -1), w_out[:a_w], w_out[a_w:],
                      g_ffn[0], bf(w_gate_dense[0]), bf(w_up_dense[0]), bf(w_down_dense[0]))

    lb = jax.nn.softmax(rec_lower_bounds.astype(F32), axis=0)
    lb1 = (jnp.cumsum(lb, axis=0) - lb[0])[1]
    kw = lb1.shape[0]
    vw = rec_norm_g.shape[1]
    q, k, lf, v, gate = _rec_inproj(x2, g_mix[1], bf(w_in_rec[0]), lb1, kw, vw)
    rec = _hgrn(q, k, lf, v, gate, rec_norm_g[0], bsz, seq)
    x3, h, route, counts = _router(x2, rec, bf(w_out_rec[0]), g_ffn[1], w_router[0])
    tm_moe = min(1024, n)
    src, slot, blk_e, nb_used, second = _moe_layout(route, counts, n, tm_moe)
    ys = _experts(h, src, slot, blk_e, nb_used, bf(w_gate_moe[0]), bf(w_up_moe[0]),
                  bf(w_down_moe[0]), tm_moe, second)
    out = _combine(x3, route, ys, g_final, second)
    return out.reshape(bsz, seq, d)
```
